```python
import math
import jax
import jax.numpy as jnp
from jax import lax
import numpy as np

D_MODEL = 1024
BATCH = 16
SEQ = 2048
DEPTH = 2
DEC_BATCH = 32
DEC_SEQ = 8
PAST_LEN = 16384
PAGE_SIZE = 128

EPS = 1e-6
N_MOD = 6
SSD_HEADS = 16
SSD_HEAD_DIM = 64
SSD_INNER = SSD_HEADS * SSD_HEAD_DIM
SSD_GROUPS = 4
SSD_STATE = 64
SSD_CONV = 4
SSD_CONV_DIM = SSD_INNER + 2 * SSD_GROUPS * SSD_STATE
SCAN_CHUNK = 128
RET_HEADS = 8
RET_DK = 64
RET_DV = 128
RET_QK = RET_HEADS * RET_DK
RET_V = RET_HEADS * RET_DV
EVEN_SPLITS = (SSD_INNER,
               SSD_INNER + SSD_CONV_DIM,
               SSD_INNER + SSD_CONV_DIM + SSD_HEADS,
               SSD_INNER + SSD_CONV_DIM + SSD_HEADS + RET_QK,
               SSD_INNER + SSD_CONV_DIM + SSD_HEADS + 2 * RET_QK,
               SSD_INNER + SSD_CONV_DIM + SSD_HEADS + 2 * RET_QK + RET_V)
EVEN_IN = SSD_INNER + SSD_CONV_DIM + SSD_HEADS + 2 * RET_QK + 2 * RET_V
EVEN_OUT = SSD_INNER + RET_V
ATT_HEADS = 16
ATT_HEAD_DIM = 64
ATT_WIDTH = ATT_HEADS * ATT_HEAD_DIM
DILATED_PATTERNS = ((128, 1), (512, 4), (2048, 16))
ATT_WINDOW = 2048
ATT_BLOCK = 128
ROPE_THETA = 10000.0
MOE_GROUPS = 4
MOE_PER_GROUP = 8
MOE_EXPERTS = MOE_GROUPS * MOE_PER_GROUP
MOE_TOP_K = 2
MOE_HIDDEN = 256
MOE_TOKEN_BLOCK = 1024
N_EVEN = (DEPTH + 1) // 2
N_ODD = DEPTH // 2

kernel_name = 'hybrid_ssd_retention_dilated_hmoe_step'


def rmsnorm(x, g):
    xf = x.astype(jnp.float32)
    return xf * lax.rsqrt(jnp.mean(xf * xf, axis=-1, keepdims=True) + EPS) * g


def rope(x, pos):
    half = x.shape[-1] // 2
    inv_freq = ROPE_THETA ** (-jnp.arange(half, dtype=jnp.float32) / half)
    ang = pos.astype(jnp.float32)[:, None] * inv_freq[None, :]
    cos = jnp.cos(ang)[None, :, None, :]
    sin = jnp.sin(ang)[None, :, None, :]
    x1 = x[..., :half].astype(jnp.float32)
    x2 = x[..., half:].astype(jnp.float32)
    return jnp.concatenate([x1 * cos - x2 * sin, x1 * sin + x2 * cos], axis=-1)


def chunked_decay_scan(xs, log_a, bk, cq, h0):
    f32 = jnp.float32
    b, L, H, P = xs.shape
    N = bk.shape[-1]
    cs = math.gcd(L, SCAN_CHUNK)
    nc = L // cs
    xs = xs.astype(f32).reshape(b, nc, cs, H, P)
    bk = bk.astype(f32).reshape(b, nc, cs, H, N)
    cq = cq.astype(f32).reshape(b, nc, cs, H, N)
    acum = jnp.cumsum(log_a.astype(f32).reshape(b, nc, cs, H), axis=2)
    acum_h = acum.transpose(0, 1, 3, 2)
    causal = jnp.tril(jnp.ones((cs, cs), dtype=bool))
    diff = acum_h[..., :, None] - acum_h[..., None, :]
    decay = jnp.exp(jnp.where(causal, diff, -jnp.inf))
    scores = jnp.einsum('bclhn,bcshn->bchls', cq, bk) * decay
    y_diag = jnp.einsum('bchls,bcshp->bclhp', scores, xs)
    to_end = jnp.exp(acum[:, :, -1:, :] - acum)
    chunk_states = jnp.einsum('bclhn,bclh,bclhp->bchpn', bk, to_end, xs)
    chunk_decay = jnp.exp(acum[:, :, -1, :])

    def step(h, inp):
        st, dec = inp
        return h * dec[:, :, None, None] + st, h

    h_final, h_prev = lax.scan(step, h0.astype(f32),
                               (chunk_states.transpose(1, 0, 2, 3, 4), chunk_decay.transpose(1, 0, 2)))
    h_prev = h_prev.transpose(1, 0, 2, 3, 4)
    y_off = jnp.einsum('bclhn,bchpn,bclh->bclhp', cq, h_prev, jnp.exp(acum))
    return (y_diag + y_off).reshape(b, L, H, P), h_final


def causal_dwconv(xbc, buf, w, bias):
    L = xbc.shape[1]
    xpad = jnp.concatenate([buf.astype(xbc.dtype), xbc], axis=1)
    out = bias + xpad[:, 0:L] * w[0]
    for tap in range(1, SSD_CONV):
        out = out + xpad[:, tap:tap + L] * w[tap]
    return jax.nn.silu(out), xpad[:, L:]


def ssd_mixer(z, xbc, dt_raw, conv_buf, h0, conv_w, conv_b, dt_bias, a_log, d_skip, norm_g):
    b, L, _ = xbc.shape
    xbc, new_buf = causal_dwconv(xbc, conv_buf, conv_w, conv_b)
    gn = SSD_GROUPS * SSD_STATE
    xh = xbc[..., :SSD_INNER].reshape(b, L, SSD_HEADS, SSD_HEAD_DIM)
    bg = xbc[..., SSD_INNER:SSD_INNER + gn].reshape(b, L, SSD_GROUPS, SSD_STATE)
    cg = xbc[..., SSD_INNER + gn:].reshape(b, L, SSD_GROUPS, SSD_STATE)
    rep = SSD_HEADS // SSD_GROUPS
    bh = jnp.repeat(bg, rep, axis=2)
    ch = jnp.repeat(cg, rep, axis=2)
    dt = jax.nn.softplus(dt_raw.astype(jnp.float32) + dt_bias)
    a = -jnp.exp(a_log.astype(jnp.float32))
    y, h_final = chunked_decay_scan(xh * dt[..., None], dt * a, bh, ch, h0)
    y = y + d_skip[:, None] * xh
    y = y.reshape(b, L, SSD_INNER) * jax.nn.silu(z)
    return rmsnorm(y, norm_g), new_buf, h_final


def retention_mixer(q, k, v, g, pos, s0, norm_g):
    b, L, _ = q.shape
    q = rope(q.reshape(b, L, RET_HEADS, RET_DK), pos)
    k = rope(k.reshape(b, L, RET_HEADS, RET_DK), pos) * RET_DK ** -0.5
    v = v.reshape(b, L, RET_HEADS, RET_DV)
    log_gamma = jnp.log1p(-jnp.exp2(-5.0 - jnp.arange(RET_HEADS, dtype=jnp.float32)))
    log_a = jnp.broadcast_to(log_gamma, (b, L, RET_HEADS))
    o, s_final = chunked_decay_scan(v, log_a, k, q, s0)
    o = rmsnorm(o, norm_g.reshape(RET_HEADS, RET_DV)).reshape(b, L, RET_V)
    return jax.nn.silu(g) * o, s_final


def even_mixer(h, pos, ssd0, conv0, ret0, w_in, w_out, conv_w, conv_b, dt_bias, a_log, d_skip,
               ssd_norm_g, ret_norm_g):
    p = h @ w_in
    z, xbc, dt_raw, rq, rk, rv, rg = jnp.split(p, list(EVEN_SPLITS), axis=-1)
    y_a, conv_new, ssd_new = ssd_mixer(z, xbc, dt_raw, conv0, ssd0, conv_w, conv_b, dt_bias, a_log,
                                       d_skip, ssd_norm_g)
    y_b, ret_new = retention_mixer(rq, rk, rv, rg, pos, ret0, ret_norm_g)
    return jnp.concatenate([y_a, y_b], axis=-1) @ w_out, ssd_new, conv_new, ret_new


def masked_softmax(s, mask):
    s = jnp.where(mask, s.astype(jnp.float32), -jnp.inf)
    m = jnp.max(s, axis=-1, keepdims=True)
    p = jnp.exp(s - m)
    den = jnp.sum(p, axis=-1, keepdims=True)
    return p / den, (m + jnp.log(den))[..., 0]


def dilated_branch_prompt(q, k, v, window, dil):
    b, S, H, E = q.shape
    nk = window // dil
    L = S // dil
    qb = math.gcd(L, ATT_BLOCK)
    nb = L // qb
    qs = q.reshape(b, nb, qb, dil, H, E)
    pad = ((0, 0), (nk, 0), (0, 0), (0, 0), (0, 0))
    kpad = jnp.pad(k.reshape(b, L, dil, H, E), pad)
    vpad = jnp.pad(v.reshape(b, L, dil, H, E), pad)
    idx = (jnp.arange(nb) * qb)[:, None] + jnp.arange(qb + nk)[None, :]
    kw = kpad[:, idx]
    vw = vpad[:, idx]
    dist = jnp.arange(qb)[:, None] + nk - jnp.arange(qb + nk)[None, :]
    mask = (dist >= 0) & (dist <= nk) & ((idx - nk) >= 0)[:, None, :]
    s = jnp.einsum('bnqrhe,bnkrhe->bnrhqk', qs, kw)
    p, lse = masked_softmax(s, mask[None, :, None, None])
    o = jnp.einsum('bnrhqk,bnkrhe->bnqrhe', p, vw).reshape(b, S, H, E)
    lse = lse.transpose(0, 1, 4, 2, 3).reshape(b, S, H)
    return o, lse


def dilated_branch_sample(q, kall, vall, past, window, dil):
    b, T, H, E = q.shape
    nk = window // dil
    idx = past + jnp.arange(T)[:, None] - dil * jnp.arange(nk + 1)[None, :]
    valid = idx >= 0
    idx = jnp.maximum(idx, 0)
    kg = kall[:, idx]
    vg = vall[:, idx]
    s = jnp.einsum('bthe,btkhe->bhtk', q, kg)
    p, lse = masked_softmax(s, valid[None, None])
    o = jnp.einsum('bhtk,btkhe->bthe', p, vg)
    return o, lse.transpose(0, 2, 1)


def dilated_attention(q, k, v, past):
    outs, lses = [], []
    for window, dil in DILATED_PATTERNS:
        if past is None:
            o, l = dilated_branch_prompt(q, k, v, window, dil)
        else:
            o, l = dilated_branch_sample(q, k, v, past, window, dil)
        outs.append(o)
        lses.append(l)
    w = jax.nn.softmax(jnp.stack(lses, axis=0), axis=0)
    return jnp.einsum('gbth,gbthe->bthe', w, jnp.stack(outs, axis=0))


def odd_mixer(h, pos, k_buf, v_buf, w_qkv, w_out):
    b, L, _ = h.shape
    qkv = (h @ w_qkv).reshape(b, L, 3, ATT_HEADS, ATT_HEAD_DIM)
    q = rope(qkv[:, :, 0], pos) * ATT_HEAD_DIM ** -0.5
    k = rope(qkv[:, :, 1], pos)
    v = qkv[:, :, 2].astype(jnp.float32)
    if k_buf is None:
        o = dilated_attention(q, k, v, None)
        keep = min(ATT_WINDOW, L)
        new_k, new_v = k[:, L - keep:], v[:, L - keep:]
    else:
        kall = jnp.concatenate([k_buf.astype(jnp.float32), k], axis=1)
        vall = jnp.concatenate([v_buf.astype(jnp.float32), v], axis=1)
        o = dilated_attention(q, kall, vall, k_buf.shape[1])
        new_k, new_v = kall[:, L:], vall[:, L:]
    return o.reshape(b, L, ATT_WIDTH) @ w_out, new_k, new_v


def hier_moe(h, wg, bg, we, be, w1, w3, w2):
    b, L, D = h.shape
    n = b * L
    xt = h.reshape(n, D)
    g_logits = (xt @ wg + bg).astype(jnp.float32)
    g_prob = jax.nn.softmax(g_logits, axis=-1)
    g_idx = jnp.argmax(g_logits, axis=-1)
    g_w = jnp.take_along_axis(g_prob, g_idx[:, None], axis=-1)
    e_logits = (xt @ we + be).astype(jnp.float32).reshape(n, MOE_GROUPS, MOE_PER_GROUP)
    e_sel = jnp.take_along_axis(e_logits, g_idx[:, None, None], axis=1)[:, 0]
    top_v, top_i = lax.top_k(e_sel, MOE_TOP_K)
    top_w = jax.nn.softmax(top_v, axis=-1)
    in_group = jnp.sum(jax.nn.one_hot(top_i, MOE_PER_GROUP, dtype=jnp.float32) * top_w[..., None], axis=1)
    gate = (jax.nn.one_hot(g_idx, MOE_GROUPS, dtype=jnp.float32)[:, :, None]
            * in_group[:, None, :] * g_w[:, :, None]).reshape(n, MOE_EXPERTS)
    blk = math.gcd(n, MOE_TOKEN_BLOCK)

    def expert_block(args):
        xb, gb = args
        a = jnp.einsum('nd,edf->nef', xb, w1)
        u = jnp.einsum('nd,edf->nef', xb, w3)
        return jnp.einsum('nef,efd->nd', jax.nn.silu(a) * u * gb[:, :, None], w2)

    out = lax.map(expert_block, (xt.reshape(n // blk, blk, D), gate.reshape(n // blk, blk, MOE_EXPERTS)))
    return out.reshape(b, L, D)


def setup_inputs(seed: int = 0) -> dict:
    key = jax.random.key(seed)
    ks = iter(jax.random.split(key, 48))
    f32 = jnp.float32

    def nrm(shape, scale):
        return jax.random.normal(next(ks), shape, f32) * scale

    def gain(shape):
        return 1.0 + nrm(shape, 0.05)

    wb = min(ATT_WINDOW, PAST_LEN)
    dt0 = jnp.exp(jax.random.uniform(next(ks), (N_EVEN, SSD_HEADS), f32, math.log(1e-3), math.log(1e-1)))
    a0 = jax.random.uniform(next(ks), (N_EVEN, SSD_HEADS), f32, 1.0, 16.0)
    return {
        'x_prompt': nrm((BATCH, SEQ, D_MODEL), 1.0),
        'x_sample': nrm((DEC_BATCH, DEC_SEQ, D_MODEL), 1.0),
        'state_ssd': nrm((N_EVEN, DEC_BATCH, SSD_HEADS, SSD_HEAD_DIM, SSD_STATE), 0.1),
        'state_conv': nrm((N_EVEN, DEC_BATCH, SSD_CONV - 1, SSD_CONV_DIM), 1.0),
        'state_ret': nrm((N_EVEN, DEC_BATCH, RET_HEADS, RET_DV, RET_DK), 1.0),
        'cache_k': nrm((N_ODD, DEC_BATCH, wb, ATT_HEADS, ATT_HEAD_DIM), 1.0),
        'cache_v': nrm((N_ODD, DEC_BATCH, wb, ATT_HEADS, ATT_HEAD_DIM), 1.0),
        'c_prompt': nrm((BATCH, D_MODEL), 1.0),
        'c_sample': nrm((DEC_BATCH, D_MODEL), 1.0),
        'ada_w': nrm((DEPTH, D_MODEL, N_MOD * D_MODEL), 0.5 * D_MODEL ** -0.5),
        'ada_b': nrm((DEPTH, N_MOD * D_MODEL), 0.02),
        'norm_mix_g': gain((DEPTH, D_MODEL)),
        'norm_ffn_g': gain((DEPTH, D_MODEL)),
        'final_norm_g': gain((D_MODEL,)),
        'even_w_in': nrm((N_EVEN, D_MODEL, EVEN_IN), D_MODEL ** -0.5),
        'even_w_out': nrm((N_EVEN, EVEN_OUT, D_MODEL), EVEN_OUT ** -0.5),
        'ssd_conv_w': nrm((N_EVEN, SSD_CONV, SSD_CONV_DIM), SSD_CONV ** -0.5),
        'ssd_conv_b': nrm((N_EVEN, SSD_CONV_DIM), 0.02),
        'ssd_dt_bias': dt0 + jnp.log(-jnp.expm1(-dt0)),
        'ssd_a_log': jnp.log(a0),
        'ssd_d': gain((N_EVEN, SSD_HEADS)),
        'ssd_norm_g': gain((N_EVEN, SSD_INNER)),
        'ret_norm_g': gain((N_EVEN, RET_V)),
        'odd_w_qkv': nrm((N_ODD, D_MODEL, 3 * ATT_WIDTH), D_MODEL ** -0.5),
        'odd_w_out': nrm((N_ODD, ATT_WIDTH, D_MODEL), ATT_WIDTH ** -0.5),
        'moe_wg': nrm((DEPTH, D_MODEL, MOE_GROUPS), D_MODEL ** -0.5),
        'moe_bg': nrm((DEPTH, MOE_GROUPS), 0.01),
        'moe_we': nrm((DEPTH, D_MODEL, MOE_EXPERTS), D_MODEL ** -0.5),
        'moe_be': nrm((DEPTH, MOE_EXPERTS), 0.01),
        'moe_w1': nrm((DEPTH, MOE_EXPERTS, D_MODEL, MOE_HIDDEN), D_MODEL ** -0.5),
        'moe_w3': nrm((DEPTH, MOE_EXPERTS, D_MODEL, MOE_HIDDEN), D_MODEL ** -0.5),
        'moe_w2': nrm((DEPTH, MOE_EXPERTS, MOE_HIDDEN, D_MODEL), MOE_HIDDEN ** -0.5),
    }


def reference(x_prompt, x_sample, state_ssd, state_conv, state_ret, cache_k, cache_v, c_prompt, c_sample,
              ada_w, ada_b, norm_mix_g, norm_ffn_g, final_norm_g,
              even_w_in, even_w_out, ssd_conv_w, ssd_conv_b, ssd_dt_bias, ssd_a_log, ssd_d, ssd_norm_g,
              ret_norm_g, odd_w_qkv, odd_w_out,
              moe_wg, moe_bg, moe_we, moe_be, moe_w1, moe_w3, moe_w2):
    f32 = jnp.float32

    def run_group(x, c, pos, ssd_in, conv_in, ret_in, k_in, v_in):
        ssd_out, conv_out, ret_out, k_out, v_out = [], [], [], [], []
        cond = jax.nn.silu(c.astype(f32))
        for i in range(DEPTH):
            mod = (cond @ ada_w[i] + ada_b[i])[:, None, :]
            sh1, sc1, gt1, sh2, sc2, gt2 = jnp.split(mod, N_MOD, axis=-1)
            h = rmsnorm(x, norm_mix_g[i]) * (1.0 + sc1) + sh1
            if i % 2 == 0:
                e = i // 2
                mix, s_new, cv_new, r_new = even_mixer(
                    h, pos, ssd_in[e], conv_in[e], ret_in[e], even_w_in[e], even_w_out[e],
                    ssd_conv_w[e], ssd_conv_b[e], ssd_dt_bias[e], ssd_a_log[e], ssd_d[e],
                    ssd_norm_g[e], ret_norm_g[e])
                ssd_out.append(s_new)
                conv_out.append(cv_new)
                ret_out.append(r_new)
            else:
                o = i // 2
                kb = None if k_in is None else k_in[o]
                vb = None if v_in is None else v_in[o]
                mix, kn, vn = odd_mixer(h, pos, kb, vb, odd_w_qkv[o], odd_w_out[o])
                k_out.append(kn)
                v_out.append(vn)
            x = x + gt1 * mix
            h = rmsnorm(x, norm_ffn_g[i]) * (1.0 + sc2) + sh2
            x = x + gt2 * hier_moe(h, moe_wg[i], moe_bg[i], moe_we[i], moe_be[i],
                                   moe_w1[i], moe_w3[i], moe_w2[i])
        return (rmsnorm(x, final_norm_g), jnp.stack(ssd_out), jnp.stack(conv_out), jnp.stack(ret_out),
                jnp.stack(k_out), jnp.stack(v_out))

    bp, sp = x_prompt.shape[0], x_prompt.shape[1]
    ssd0 = jnp.zeros((N_EVEN, bp, SSD_HEADS, SSD_HEAD_DIM, SSD_STATE), f32)
    conv0 = jnp.zeros((N_EVEN, bp, SSD_CONV - 1, SSD_CONV_DIM), f32)
    ret0 = jnp.zeros((N_EVEN, bp, RET_HEADS, RET_DV, RET_DK), f32)
    pos_p = jnp.arange(sp, dtype=jnp.int32)
    y_prompt, p_ssd, p_conv, p_ret, p_k, p_v = run_group(
        x_prompt, c_prompt, pos_p, ssd0, conv0, ret0, None, None)
    pos_s = PAST_LEN + jnp.arange(x_sample.shape[1], dtype=jnp.int32)
    y_sample, s_ssd, s_conv, s_ret, s_k, s_v = run_group(
        x_sample, c_sample, pos_s, state_ssd, state_conv, state_ret, cache_k, cache_v)
    return (y_prompt, y_sample, p_ssd, p_conv, p_ret, p_k, p_v, s_ssd, s_conv, s_ret, s_k, s_v)
```

```python
import functools
import math

import jax
import jax.numpy as jnp
from jax import lax
from jax.experimental import pallas as pl
from jax.experimental.pallas import tpu as pltpu

F32 = jnp.float32
BF16 = jnp.bfloat16
I32 = jnp.int32

D_MODEL = 1024
EPS = 1e-6
N_MOD = 6
SSD_HEADS = 16
SSD_HEAD_DIM = 64
SSD_INNER = SSD_HEADS * SSD_HEAD_DIM
SSD_GROUPS = 4
SSD_STATE = 64
SSD_CONV = 4
SSD_CONV_DIM = SSD_INNER + 2 * SSD_GROUPS * SSD_STATE
RET_HEADS = 8
RET_DK = 64
RET_DV = 128
RET_QK = RET_HEADS * RET_DK
RET_V = RET_HEADS * RET_DV
EVEN_OUT = SSD_INNER + RET_V
ATT_HEADS = 16
ATT_HEAD_DIM = 64
ATT_WIDTH = ATT_HEADS * ATT_HEAD_DIM
DILATED_PATTERNS = ((128, 1), (512, 4), (2048, 16))
ATT_WINDOW = 2048
PAST_LEN = 16384
ROPE_THETA = 10000.0
MOE_GROUPS = 4
MOE_PER_GROUP = 8
MOE_EXPERTS = MOE_GROUPS * MOE_PER_GROUP
MOE_HIDDEN = 256

LANES = 128
SUBLANES = 8
VMEM_LIMIT = 56 * 1024 * 1024

P_Z = 0
P_XBC = P_Z + SSD_INNER
P_Q = P_XBC + SSD_CONV_DIM
P_K = P_Q + RET_QK
P_V = P_K + RET_QK
P_G = P_V + RET_V
P_DT = P_G + RET_V
P_WIDTH = P_DT + LANES

SCAN_CHUNK = 128
ATT_BLOCK = 128
MOE_TILE = 256


def _cparams(sem):
    return pltpu.CompilerParams(dimension_semantics=sem, vmem_limit_bytes=VMEM_LIMIT)


def _row_tiling(nb, sl, target):
    if sl >= target:
        assert sl % target == 0
        return 1, target
    bt = min(nb, target // sl)
    assert nb % bt == 0
    return bt, sl


def _silu(x):
    return x * jax.nn.sigmoid(x)


def _norm_mod(x, g, sc, sh):
    ms = jnp.mean(x * x, axis=-1, keepdims=True)
    return x * lax.rsqrt(ms + EPS) * g * (1.0 + sc) + sh


def _split3(x):
    hi = x.astype(BF16)
    r1 = x - hi.astype(F32)
    mid = r1.astype(BF16)
    lo = (r1 - mid.astype(F32)).astype(BF16)
    return hi, mid, lo


def _dot_exact_rhs(x, m_bf16):
    hi, mid, lo = _split3(x)
    dot = functools.partial(jnp.dot, preferred_element_type=F32)
    return dot(hi, m_bf16) + dot(mid, m_bf16) + dot(lo, m_bf16)


def _dot_exact_lhs(m_bf16, x):
    hi, mid, lo = _split3(x)
    dot = functools.partial(jnp.dot, preferred_element_type=F32)
    return dot(m_bf16, hi) + dot(m_bf16, mid) + dot(m_bf16, lo)


def _dot_nt(a, b):
    return lax.dot_general(a, b, (((1,), (1,)), ((), ())), preferred_element_type=F32)


def _rope(a, cos, sin_signed):
    w = a.shape[-1]
    lane = lax.broadcasted_iota(I32, a.shape, 1)
    first = (lane & (ATT_HEAD_DIM // 2)) == 0
    rot = jnp.where(first, pltpu.roll(a, w - ATT_HEAD_DIM // 2, 1), pltpu.roll(a, ATT_HEAD_DIM // 2, 1))
    return a * cos + rot * sin_signed


def _adaln_body(c_ref, w_ref, b_ref, o_ref):
    a = _silu(c_ref[...]).astype(BF16)
    o_ref[0] = jnp.dot(a, w_ref[0].astype(BF16), preferred_element_type=F32) + b_ref[0]


def _adaln(c_all, ada_w, ada_b):
    nb = c_all.shape[0]
    depth, _, n6 = ada_w.shape
    tn = 1024
    return pl.pallas_call(
        _adaln_body,
        grid=(depth, n6 // tn),
        in_specs=[
            pl.BlockSpec((nb, D_MODEL), lambda i, j: (0, 0)),
            pl.BlockSpec((1, D_MODEL, tn), lambda i, j: (i, 0, j)),
            pl.BlockSpec((1, 1, tn), lambda i, j: (i, 0, j)),
        ],
        out_specs=pl.BlockSpec((1, nb, tn), lambda i, j: (i, 0, j)),
        out_shape=jax.ShapeDtypeStruct((depth, nb, n6), F32),
        compiler_params=_cparams(("parallel", "parallel")),
        name="adaln",
    )(c_all, ada_w, ada_b.reshape(depth, 1, n6))


def _even_in_body(x_ref, g_ref, sc_ref, sh_ref, w_ref, o_ref, h_scr):
    @pl.when(pl.program_id(1) == 0)
    def _():
        h = _norm_mod(x_ref[...], g_ref[...], sc_ref[...], sh_ref[...])
        h_scr[...] = h.reshape(h_scr.shape).astype(BF16)

    o_ref[...] = jnp.dot(h_scr[...], w_ref[...], preferred_element_type=F32)


def _even_in(x, g, sc, sh, w):
    nb, sl, _ = x.shape
    n = w.shape[1]
    bt, lt = _row_tiling(nb, sl, 512)
    nl = sl // lt
    tm = bt * lt
    tn = 1920
    assert n % tn == 0
    return pl.pallas_call(
        _even_in_body,
        grid=((nb // bt) * nl, n // tn),
        in_specs=[
            pl.BlockSpec((bt, lt, D_MODEL), lambda i, j: (i // nl, i % nl, 0)),
            pl.BlockSpec((1, 1, D_MODEL), lambda i, j: (0, 0, 0)),
            pl.BlockSpec((bt, 1, D_MODEL), lambda i, j: (i // nl, 0, 0)),
            pl.BlockSpec((bt, 1, D_MODEL), lambda i, j: (i // nl, 0, 0)),
            pl.BlockSpec((D_MODEL, tn), lambda i, j: (0, j)),
        ],
        out_specs=pl.BlockSpec((tm, tn), lambda i, j: (i, j)),
        out_shape=jax.ShapeDtypeStruct((nb * sl, n), F32),
        scratch_shapes=[pltpu.VMEM((tm, D_MODEL), BF16)],
        compiler_params=_cparams(("parallel", "arbitrary")),
        name="even_in",
    )(x, g.reshape(1, 1, D_MODEL), sc, sh, w)


def _even_core_body(p_ref, cos_ref, sin_ref, cw_ref, cb_ref, dtb_ref, alog_ref, dskip_ref, ssdg_ref, retg_ref,
                    expand_ref, ssd0_ref, conv0_ref, ret0_ref,
                    y_ref, ssd_out_ref, conv_out_ref, ret_out_ref,
                    xpad_scr, ssd_scr, ret_scr, yacc_scr, *, valid_len):
    cs = SCAN_CHUNK
    c = pl.program_id(1)
    nc = pl.num_programs(1)
    hd = SSD_HEAD_DIM

    @pl.when(c == 0)
    def _():
        ssd_scr[...] = ssd0_ref[0]
        ret_scr[...] = ret0_ref[0]
        xpad_scr[0:SUBLANES, :] = conv0_ref[0]

    row = lax.broadcasted_iota(I32, (cs, LANES), 0)
    col = lax.broadcasted_iota(I32, (cs, cs), 1)
    rowc = lax.broadcasted_iota(I32, (cs, cs), 0)
    causal = col <= rowc
    live = row < valid_len

    xpad_scr[SUBLANES:SUBLANES + cs, :] = p_ref[0, :, P_XBC:P_XBC + SSD_CONV_DIM]
    conv = cb_ref[...] + xpad_scr[SUBLANES:SUBLANES + cs, :] * cw_ref[SSD_CONV - 1:SSD_CONV, :]
    for back in range(1, SSD_CONV):
        tap = SSD_CONV - 1 - back
        conv = conv + xpad_scr[SUBLANES - back:SUBLANES - back + cs, :] * cw_ref[tap:tap + 1, :]
    xc = _silu(conv)

    @pl.when(c == nc - 1)
    def _():
        conv_out_ref[0] = xpad_scr[valid_len:valid_len + SUBLANES, :]

    xpad_scr[0:SUBLANES, :] = xpad_scr[cs:cs + SUBLANES, :]

    xh = xc[:, 0:SSD_INNER]
    if valid_len < cs:
        xh = jnp.where(live[:, 0:1], xh, 0.0)

    dt_in = p_ref[0, :, P_DT:P_DT + LANES] + dtb_ref[...]
    dt = jnp.maximum(dt_in, 0.0) + jnp.log1p(jnp.exp(-jnp.abs(dt_in)))
    la = dt * (-jnp.exp(alog_ref[...]))
    if valid_len < cs:
        la = jnp.where(live, la, 0.0)
    tri = causal.astype(BF16)
    acum = _dot_exact_lhs(tri, la)
    acum_last = acum[cs - 1:cs, :]
    to_end = jnp.exp(acum_last - acum)
    eacum = jnp.exp(acum)
    cdecay = jnp.broadcast_to(jnp.exp(acum_last), (2 * SUBLANES, LANES))
    stacked = jnp.concatenate([dt, to_end, eacum, cdecay], axis=0)
    stacked_x = _dot_exact_rhs(stacked, expand_ref[...])
    dt_x = stacked_x[0:cs]
    to_end_x = stacked_x[cs:2 * cs]
    eacum_x = stacked_x[2 * cs:3 * cs]
    cdecay_x = stacked_x[3 * cs:3 * cs + 1]
    acum_t = acum.T

    xs = xh * dt_x
    xs_b = xs.astype(BF16)
    xs_te_t = (xs * to_end_x).T.astype(BF16)

    for grp in range(SSD_GROUPS):
        b_g = xc[:, SSD_INNER + grp * SSD_STATE:SSD_INNER + (grp + 1) * SSD_STATE].astype(BF16)
        c_g = xc[:, SSD_INNER + (SSD_GROUPS + grp) * SSD_STATE:SSD_INNER + (SSD_GROUPS + grp + 1) * SSD_STATE].astype(BF16)
        s_g = _dot_nt(c_g, b_g)
        for hh in range(SSD_HEADS // SSD_GROUPS):
            h = grp * (SSD_HEADS // SSD_GROUPS) + hh
            lo, hi = h * hd, (h + 1) * hd
            diff = acum[:, h:h + 1] - acum_t[h:h + 1, :]
            decay = jnp.exp(jnp.where(causal, diff, -jnp.inf))
            m = (s_g * decay).astype(BF16)
            y_h = jnp.dot(m, xs_b[:, lo:hi], preferred_element_type=F32)
            h_prev = ssd_scr[h]
            y_h = y_h + _dot_nt(c_g, h_prev.astype(BF16)) * eacum_x[:, lo:hi]
            yacc_scr[:, lo:hi] = y_h
            ssd_scr[h] = h_prev * cdecay_x[:, lo:hi] + jnp.dot(xs_te_t[lo:hi, :], b_g, preferred_element_type=F32)

    y = yacc_scr[...] + dskip_ref[...] * xh
    y = y * _silu(p_ref[0, :, P_Z:P_Z + SSD_INNER])
    y = y * lax.rsqrt(jnp.mean(y * y, axis=-1, keepdims=True) + EPS) * ssdg_ref[...]
    y_ref[0, :, 0:SSD_INNER] = y.astype(BF16)

    cos = cos_ref[:, 0:RET_QK]
    sin = sin_ref[:, 0:RET_QK]
    rq = _rope(p_ref[0, :, P_Q:P_Q + RET_QK], cos, sin).astype(BF16)
    rk = (_rope(p_ref[0, :, P_K:P_K + RET_QK], cos, sin) * RET_DK ** -0.5).astype(BF16)
    rv = p_ref[0, :, P_V:P_V + RET_V]
    if valid_len < cs:
        rv = jnp.where(live[:, 0:1], rv, 0.0)
    steps_r = jnp.minimum(rowc + 1, valid_len).astype(F32)
    steps_c = jnp.minimum(col + 1, valid_len).astype(F32)
    steps_l = jnp.minimum(row + 1, valid_len).astype(F32)
    for h in range(RET_HEADS):
        lg = math.log1p(-(2.0 ** (-5.0 - h)))
        q_h = rq[:, h * RET_DK:(h + 1) * RET_DK]
        k_h = rk[:, h * RET_DK:(h + 1) * RET_DK]
        v_h = rv[:, h * RET_DV:(h + 1) * RET_DV]
        decay = jnp.exp(jnp.where(causal, lg * (steps_r - steps_c), -jnp.inf))
        m = (_dot_nt(q_h, k_h) * decay).astype(BF16)
        o_h = jnp.dot(m, v_h.astype(BF16), preferred_element_type=F32)
        s_prev = ret_scr[h]
        o_h = o_h + _dot_nt(q_h, s_prev.astype(BF16)) * jnp.exp(lg * steps_l)
        v_te_t = (v_h * jnp.exp(lg * (float(valid_len) - steps_l))).T.astype(BF16)
        ret_scr[h] = s_prev * math.exp(lg * valid_len) + jnp.dot(v_te_t, k_h, preferred_element_type=F32)
        o_h = o_h * lax.rsqrt(jnp.mean(o_h * o_h, axis=-1, keepdims=True) + EPS) * retg_ref[:, h * RET_DV:(h + 1) * RET_DV]
        o_h = o_h * _silu(p_ref[0, :, P_G + h * RET_DV:P_G + (h + 1) * RET_DV])
        y_ref[0, :, SSD_INNER + h * RET_DV:SSD_INNER + (h + 1) * RET_DV] = o_h.astype(BF16)

    @pl.when(c == nc - 1)
    def _():
        ssd_out_ref[0] = ssd_scr[...]
        ret_out_ref[0] = ret_scr[...]


def _even_core(p, cos, sin, conv_w, conv_b, dt_bias, a_log, d_skip, ssd_norm_g, ret_norm_g, ssd0, conv0, ret0,
               valid_len):
    nb, sl, _ = p.shape
    cs = SCAN_CHUNK
    nc = sl // cs
    assert sl % cs == 0 and (valid_len == cs or nc == 1) and valid_len % SUBLANES == 0
    pad = LANES - SSD_HEADS
    dtb = jnp.pad(dt_bias, (0, pad)).reshape(1, LANES)
    alog = jnp.pad(a_log, (0, pad)).reshape(1, LANES)
    dskip_x = jnp.repeat(d_skip, SSD_HEAD_DIM).reshape(1, SSD_INNER)
    expand = (jnp.arange(LANES)[:, None] == (jnp.arange(SSD_INNER)[None, :] // SSD_HEAD_DIM)).astype(BF16)
    conv0p = jnp.pad(conv0, ((0, 0), (SUBLANES - (SSD_CONV - 1), 0), (0, 0)))
    const2 = lambda b, c: (0, 0)
    st4 = lambda b, c: (b, 0, 0, 0)
    y, ssd_new, conv_new, ret_new = pl.pallas_call(
        functools.partial(_even_core_body, valid_len=valid_len),
        grid=(nb, nc),
        in_specs=[
            pl.BlockSpec((1, cs, P_WIDTH), lambda b, c: (b, c, 0)),
            pl.BlockSpec((cs, ATT_WIDTH), lambda b, c: (c, 0)),
            pl.BlockSpec((cs, ATT_WIDTH), lambda b, c: (c, 0)),
            pl.BlockSpec((SSD_CONV, SSD_CONV_DIM), const2),
            pl.BlockSpec((1, SSD_CONV_DIM), const2),
            pl.BlockSpec((1, LANES), const2),
            pl.BlockSpec((1, LANES), const2),
            pl.BlockSpec((1, SSD_INNER), const2),
            pl.BlockSpec((1, SSD_INNER), const2),
            pl.BlockSpec((1, RET_V), const2),
            pl.BlockSpec((LANES, SSD_INNER), const2),
            pl.BlockSpec((1, SSD_HEADS, SSD_HEAD_DIM, SSD_STATE), st4),
            pl.BlockSpec((1, SUBLANES, SSD_CONV_DIM), lambda b, c: (b, 0, 0)),
            pl.BlockSpec((1, RET_HEADS, RET_DV, RET_DK), st4),
        ],
        out_specs=[
            pl.BlockSpec((1, cs, EVEN_OUT), lambda b, c: (b, c, 0)),
            pl.BlockSpec((1, SSD_HEADS, SSD_HEAD_DIM, SSD_STATE), st4),
            pl.BlockSpec((1, SUBLANES, SSD_CONV_DIM), lambda b, c: (b, 0, 0)),
            pl.BlockSpec((1, RET_HEADS, RET_DV, RET_DK), st4),
        ],
        out_shape=[
            jax.ShapeDtypeStruct((nb, sl, EVEN_OUT), BF16),
            jax.ShapeDtypeStruct((nb, SSD_HEADS, SSD_HEAD_DIM, SSD_STATE), F32),
            jax.ShapeDtypeStruct((nb, SUBLANES, SSD_CONV_DIM), F32),
            jax.ShapeDtypeStruct((nb, RET_HEADS, RET_DV, RET_DK), F32),
        ],
        scratch_shapes=[
            pltpu.VMEM((cs + SUBLANES, SSD_CONV_DIM), F32),
            pltpu.VMEM((SSD_HEADS, SSD_HEAD_DIM, SSD_STATE), F32),
            pltpu.VMEM((RET_HEADS, RET_DV, RET_DK), F32),
            pltpu.VMEM((cs, SSD_INNER), F32),
        ],
        compiler_params=_cparams(("parallel", "arbitrary")),
        name="even_core",
    )(p, cos, sin, conv_w, conv_b.reshape(1, SSD_CONV_DIM), dtb, alog, dskip_x,
      ssd_norm_g.reshape(1, SSD_INNER), ret_norm_g.reshape(1, RET_V), expand, ssd0, conv0p, ret0)
    return y, ssd_new, conv_new[:, SUBLANES - (SSD_CONV - 1):], ret_new


def _proj_res_body(y_ref, w_ref, x_ref, gt_ref, o_ref):
    m = jnp.dot(y_ref[...], w_ref[...], preferred_element_type=F32)
    o_ref[...] = x_ref[...] + gt_ref[...] * m.reshape(x_ref.shape)


def _proj_res(y, w, x, gt):
    nb, sl, _ = x.shape
    kdim = y.shape[1]
    bt, lt = _row_tiling(nb, sl, 512)
    nl = sl // lt
    xmap = lambda i: (i // nl, i % nl, 0)
    return pl.pallas_call(
        _proj_res_body,
        grid=((nb // bt) * nl,),
        in_specs=[
            pl.BlockSpec((bt * lt, kdim), lambda i: (i, 0)),
            pl.BlockSpec((kdim, D_MODEL), lambda i: (0, 0)),
            pl.BlockSpec((bt, lt, D_MODEL), xmap),
            pl.BlockSpec((bt, 1, D_MODEL), lambda i: (i // nl, 0, 0)),
        ],
        out_specs=pl.BlockSpec((bt, lt, D_MODEL), xmap),
        out_shape=jax.ShapeDtypeStruct(x.shape, F32),
        compiler_params=_cparams(("parallel",)),
        name="proj_res",
    )(y, w, x, gt)


def _qkv_body(x_ref, g_ref, sc_ref, sh_ref, w_ref, cos_ref, sin_ref, q_ref, k_ref, v_ref):
    h = _norm_mod(x_ref[...], g_ref[...], sc_ref[...], sh_ref[...])
    h = h.reshape(q_ref.shape).astype(BF16)
    cos = cos_ref[...]
    sin = sin_ref[...]
    dot = functools.partial(jnp.dot, preferred_element_type=F32)
    q_ref[...] = _rope(dot(h, w_ref[:, 0:ATT_WIDTH]), cos, sin) * ATT_HEAD_DIM ** -0.5
    k_ref[...] = _rope(dot(h, w_ref[:, ATT_WIDTH:2 * ATT_WIDTH]), cos, sin)
    v_ref[...] = dot(h, w_ref[:, 2 * ATT_WIDTH:3 * ATT_WIDTH])


def _qkv_rope(x, g, sc, sh, w, cos, sin):
    nb, sl, _ = x.shape
    bt, lt = _row_tiling(nb, sl, 256)
    nl = sl // lt
    tm = bt * lt
    assert cos.shape[0] in (sl, nb * sl) and (cos.shape[0] == sl) == (bt == 1)
    ntab = cos.shape[0] // tm
    out = jax.ShapeDtypeStruct((nb * sl, ATT_WIDTH), F32)
    ospec = pl.BlockSpec((tm, ATT_WIDTH), lambda i: (i, 0))
    tspec = pl.BlockSpec((tm, ATT_WIDTH), lambda i: (i % ntab, 0))
    return pl.pallas_call(
        _qkv_body,
        grid=((nb // bt) * nl,),
        in_specs=[
            pl.BlockSpec((bt, lt, D_MODEL), lambda i: (i // nl, i % nl, 0)),
            pl.BlockSpec((1, 1, D_MODEL), lambda i: (0, 0, 0)),
            pl.BlockSpec((bt, 1, D_MODEL), lambda i: (i // nl, 0, 0)),
            pl.BlockSpec((bt, 1, D_MODEL), lambda i: (i // nl, 0, 0)),
            pl.BlockSpec((D_MODEL, 3 * ATT_WIDTH), lambda i: (0, 0)),
            tspec, tspec,
        ],
        out_specs=[ospec, ospec, ospec],
        out_shape=[out, out, out],
        compiler_params=_cparams(("parallel",)),
        name="qkv_rope",
    )(x, g.reshape(1, 1, D_MODEL), sc, sh, w, cos, sin)


def _attn_prompt_body(q_ref, k_ref, v_ref, o_ref, ob_scr, lse_scr, *, seq):
    blk = ATT_BLOCK
    half = ATT_HEAD_DIM
    lane = lax.broadcasted_iota(I32, (blk, LANES), 1)
    head_a = lane < half
    rowi = lax.broadcasted_iota(I32, (blk, blk), 0)
    coli = lax.broadcasted_iota(I32, (blk, blk), 1)
    mask_cur = coli <= rowi
    mask_prev_tri = coli >= rowi

    for gi, (window, dil) in enumerate(DILATED_PATTERNS):
        assert window // dil == blk and (seq // dil) % blk == 0
        n_iter = seq // blk

        def body(t, carry, gi=gi, dil=dil):
            r = t % dil
            n = t // dil
            start = r + n * (blk * dil)
            has_prev = n > 0
            pstart = jnp.where(has_prev, start - blk * dil, start)

            def ld(ref, s0):
                if dil == 1:
                    return ref[0, pl.ds(s0, blk), :]
                return ref[0, pl.ds(s0, blk, stride=dil), :]

            q = ld(q_ref, start)
            kc = ld(k_ref, start).astype(BF16)
            vc = ld(v_ref, start).astype(BF16)
            kp = ld(k_ref, pstart).astype(BF16)
            vp = ld(v_ref, pstart).astype(BF16)
            mask_prev = jnp.logical_and(mask_prev_tri, has_prev)
            outs = []
            for keep in (head_a, jnp.logical_not(head_a)):
                qh = jnp.where(keep, q, 0.0).astype(BF16)
                sp = jnp.where(mask_prev, _dot_nt(qh, kp), -jnp.inf)
                sc = jnp.where(mask_cur, _dot_nt(qh, kc), -jnp.inf)
                m = jnp.maximum(jnp.max(sp, axis=1, keepdims=True), jnp.max(sc, axis=1, keepdims=True))
                pp = jnp.exp(sp - m)
                pc = jnp.exp(sc - m)
                den = jnp.sum(pp, axis=1, keepdims=True) + jnp.sum(pc, axis=1, keepdims=True)
                o = jnp.dot(pp.astype(BF16), vp, preferred_element_type=F32)
                o = o + jnp.dot(pc.astype(BF16), vc, preferred_element_type=F32)
                outs.append((o / den, m + jnp.log(den)))
            o = jnp.where(head_a, outs[0][0], outs[1][0])
            lse = jnp.where(head_a, outs[0][1], outs[1][1])
            if dil == 1:
                ob_scr[gi, pl.ds(start, blk), :] = o
                lse_scr[gi, pl.ds(start, blk), :] = lse
            else:
                ob_scr[gi, pl.ds(start, blk, stride=dil), :] = o
                lse_scr[gi, pl.ds(start, blk, stride=dil), :] = lse
            return carry

        lax.fori_loop(0, n_iter, body, 0)

    def merge(i, carry):
        rows = pl.ds(pl.multiple_of(i * blk, blk), blk)
        l0 = lse_scr[0, rows, :]
        l1 = lse_scr[1, rows, :]
        l2 = lse_scr[2, rows, :]
        m = jnp.maximum(jnp.maximum(l0, l1), l2)
        w0 = jnp.exp(l0 - m)
        w1 = jnp.exp(l1 - m)
        w2 = jnp.exp(l2 - m)
        o = (w0 * ob_scr[0, rows, :] + w1 * ob_scr[1, rows, :] + w2 * ob_scr[2, rows, :]) / (w0 + w1 + w2)
        o_ref[0, rows, :] = o.astype(BF16)
        return carry

    lax.fori_loop(0, seq // blk, merge, 0)


def _attn_prompt(q, k, v):
    nb, seq, _ = q.shape
    npair = ATT_WIDTH // LANES
    spec = pl.BlockSpec((1, seq, LANES), lambda b, h: (b, 0, h))
    return pl.pallas_call(
        functools.partial(_attn_prompt_body, seq=seq),
        grid=(nb, npair),
        in_specs=[spec, spec, spec],
        out_specs=spec,
        out_shape=jax.ShapeDtypeStruct((nb, seq, ATT_WIDTH), BF16),
        scratch_shapes=[
            pltpu.VMEM((len(DILATED_PATTERNS), seq, LANES), F32),
            pltpu.VMEM((len(DILATED_PATTERNS), seq, LANES), F32),
        ],
        compiler_params=_cparams(("parallel", "parallel")),
        name="attn_prompt",
    )(q, k, v)


def _attn_sample_body(q_ref, kn_ref, vn_ref, ck_ref, cv_ref, o_ref, nk_ref, nv_ref, kb_scr, vb_scr, *, past, new):
    half = ATT_HEAD_DIM
    keys = kb_scr.shape[0]
    nk_ref[0, 0:past - new, :] = ck_ref[0, new:past, :]
    nk_ref[0, past - new:past, :] = kn_ref[0]
    nv_ref[0, 0:past - new, :] = cv_ref[0, new:past, :]
    nv_ref[0, past - new:past, :] = vn_ref[0]
    kb_scr[0:past, :] = ck_ref[0].astype(BF16)
    vb_scr[0:past, :] = cv_ref[0].astype(BF16)
    tail = jnp.concatenate([kn_ref[0], jnp.zeros((keys - past - new, LANES), F32)], axis=0)
    kb_scr[past:keys, :] = tail.astype(BF16)
    tail = jnp.concatenate([vn_ref[0], jnp.zeros((keys - past - new, LANES), F32)], axis=0)
    vb_scr[past:keys, :] = tail.astype(BF16)

    q = q_ref[0]
    lane = lax.broadcasted_iota(I32, (new, LANES), 1)
    qa = jnp.where(lane < half, q, 0.0)
    qb = jnp.where(lane < half, 0.0, q)
    q2 = jnp.concatenate([qa, qb], axis=0).astype(BF16)
    s = _dot_nt(q2, kb_scr[...])
    t_idx = lax.broadcasted_iota(I32, (2 * new, keys), 0) % new
    k_idx = lax.broadcasted_iota(I32, (2 * new, keys), 1)
    dist = past + t_idx - k_idx
    ps, dens, lses = [], [], []
    for window, dil in DILATED_PATTERNS:
        ok = (dist >= 0) & (dist <= window) & ((dist & (dil - 1)) == 0)
        sg = jnp.where(ok, s, -jnp.inf)
        m = jnp.max(sg, axis=1, keepdims=True)
        p = jnp.exp(sg - m)
        den = jnp.sum(p, axis=1, keepdims=True)
        ps.append(p.astype(BF16))
        dens.append(den)
        lses.append(m + jnp.log(den))
    o_all = jnp.dot(jnp.concatenate(ps, axis=0), vb_scr[...], preferred_element_type=F32)
    m = jnp.maximum(jnp.maximum(lses[0], lses[1]), lses[2])
    ws = [jnp.exp(l - m) for l in lses]
    o = sum(w * o_all[i * 2 * new:(i + 1) * 2 * new] / d for i, (w, d) in enumerate(zip(ws, dens)))
    o = o / (ws[0] + ws[1] + ws[2])
    o_ref[0] = jnp.where(lane < half, o[0:new], o[new:2 * new]).astype(BF16)


def _attn_sample(q, kn, vn, cache_k, cache_v):
    nb, new, _ = q.shape
    past = cache_k.shape[1]
    assert past >= ATT_WINDOW and past % SUBLANES == 0 and new % SUBLANES == 0
    keys = -(-(past + new) // LANES) * LANES
    npair = ATT_WIDTH // LANES
    nspec = pl.BlockSpec((1, new, LANES), lambda b, h: (b, 0, h))
    cspec = pl.BlockSpec((1, past, LANES), lambda b, h: (b, 0, h))
    return pl.pallas_call(
        functools.partial(_attn_sample_body, past=past, new=new),
        grid=(nb, npair),
        in_specs=[nspec, nspec, nspec, cspec, cspec],
        out_specs=[nspec, cspec, cspec],
        out_shape=[
            jax.ShapeDtypeStruct((nb, new, ATT_WIDTH), BF16),
            jax.ShapeDtypeStruct((nb, past, ATT_WIDTH), F32),
            jax.ShapeDtypeStruct((nb, past, ATT_WIDTH), F32),
        ],
        scratch_shapes=[pltpu.VMEM((keys, LANES), BF16), pltpu.VMEM((keys, LANES), BF16)],
        compiler_params=_cparams(("parallel", "parallel")),
        name="attn_sample",
    )(q, kn, vn, cache_k, cache_v)


ROUTE_COLS = LANES
META_COLS = SUBLANES


def _moe_route_body(x_ref, g_ref, sc_ref, sh_ref, whi_ref, wlo_ref, b_ref, h_ref, meta_ref):
    tm = h_ref.shape[0]
    h = _norm_mod(x_ref[...], g_ref[...], sc_ref[...], sh_ref[...]).reshape(tm, D_MODEL)
    h_ref[...] = h
    hi = h.astype(BF16)
    lo = (h - hi.astype(F32)).astype(BF16)
    dot = functools.partial(jnp.dot, preferred_element_type=F32)
    logits = dot(hi, whi_ref[...]) + dot(lo, whi_ref[...]) + dot(hi, wlo_ref[...]) + b_ref[...]
    lane = lax.broadcasted_iota(I32, (tm, ROUTE_COLS), 1).astype(F32)
    big = float(ROUTE_COLS)
    neg = -jnp.inf
    gl = jnp.where(lane < MOE_GROUPS, logits, neg)
    gmax = jnp.max(gl, axis=1, keepdims=True)
    g_idx = jnp.min(jnp.where(gl == gmax, lane, big), axis=1, keepdims=True)
    g_w = 1.0 / jnp.sum(jnp.exp(gl - gmax), axis=1, keepdims=True)
    first = MOE_GROUPS + MOE_PER_GROUP * g_idx
    el = jnp.where((lane >= first) & (lane < first + MOE_PER_GROUP), logits, neg)
    v1 = jnp.max(el, axis=1, keepdims=True)
    i1 = jnp.min(jnp.where(el == v1, lane, big), axis=1, keepdims=True)
    el2 = jnp.where(lane == i1, neg, el)
    v2 = jnp.max(el2, axis=1, keepdims=True)
    i2 = jnp.min(jnp.where(el2 == v2, lane, big), axis=1, keepdims=True)
    t = jnp.exp(v2 - v1)
    w1 = g_w / (1.0 + t)
    w2 = g_w * t / (1.0 + t)
    meta = jnp.where(lane == 0, i1 - MOE_GROUPS,
                     jnp.where(lane == 1, i2 - MOE_GROUPS, jnp.where(lane == 2, w1, jnp.where(lane == 3, w2, 0.0))))
    meta_ref[...] = meta[:, 0:META_COLS]


def _moe_route(x, g, sc, sh, w_hi, w_lo, bias):
    nb, sl, _ = x.shape
    bt, lt = _row_tiling(nb, sl, 512)
    nl = sl // lt
    tm = bt * lt
    c2 = lambda i: (0, 0)
    return pl.pallas_call(
        _moe_route_body,
        grid=((nb // bt) * nl,),
        in_specs=[
            pl.BlockSpec((bt, lt, D_MODEL), lambda i: (i // nl, i % nl, 0)),
            pl.BlockSpec((1, 1, D_MODEL), lambda i: (0, 0, 0)),
            pl.BlockSpec((bt, 1, D_MODEL), lambda i: (i // nl, 0, 0)),
            pl.BlockSpec((bt, 1, D_MODEL), lambda i: (i // nl, 0, 0)),
            pl.BlockSpec((D_MODEL, ROUTE_COLS), c2),
            pl.BlockSpec((D_MODEL, ROUTE_COLS), c2),
            pl.BlockSpec((1, ROUTE_COLS), c2),
        ],
        out_specs=[pl.BlockSpec((tm, D_MODEL), lambda i: (i, 0)), pl.BlockSpec((tm, META_COLS), lambda i: (i, 0))],
        out_shape=[jax.ShapeDtypeStruct((nb * sl, D_MODEL), F32), jax.ShapeDtypeStruct((nb * sl, META_COLS), F32)],
        compiler_params=_cparams(("parallel",)),
        name="moe_route",
    )(x, g.reshape(1, 1, D_MODEL), sc, sh, w_hi, w_lo, bias)


FLAG_VALID, FLAG_FIRST, FLAG_LAST = 1, 2, 4


def _moe_experts_body(tok_ref, item_tile_ref, item_exp_ref, item_flag_ref,
                      h_hbm, meta_ref, w1_ref, w3_ref, w2_ref, out_hbm,
                      xbuf, acc, gsem, ssem, *, n_tiles):
    tm = MOE_TILE
    g = pl.program_id(0)
    n_items = pl.num_programs(0)
    tile = item_tile_ref[g]
    e = item_exp_ref[g]
    flag = item_flag_ref[g]
    slot = tile % 2

    def gather_copy(t, s, r):
        tok = tok_ref[t * tm + r]
        return pltpu.make_async_copy(h_hbm.at[pl.ds(tok, 1)], xbuf.at[s, pl.ds(r, 1)], gsem.at[s])

    def scatter_copy(t, s, r):
        tok = tok_ref[t * tm + r]
        return pltpu.make_async_copy(acc.at[s, pl.ds(r, 1)], out_hbm.at[pl.ds(tok, 1)], ssem.at[s])

    def for_rows(fn):
        def body(r, carry):
            fn(r)
            return carry
        lax.fori_loop(0, tm, body, 0, unroll=8)

    @pl.when(g == 0)
    def _():
        for_rows(lambda r: gather_copy(0, 0, r).start())

    @pl.when((flag & FLAG_FIRST) != 0)
    def _():
        @pl.when(tile >= 2)
        def _():
            for_rows(lambda r: scatter_copy(tile - 2, slot, r).wait())

        for_rows(lambda r: gather_copy(tile, slot, r).wait())

        @pl.when(tile + 1 < n_tiles)
        def _():
            for_rows(lambda r: gather_copy(tile + 1, 1 - slot, r).start())

        acc[slot] = jnp.zeros((tm, D_MODEL), F32)

    @pl.when((flag & FLAG_VALID) != 0)
    def _():
        x = xbuf[slot].astype(BF16)
        a = jnp.dot(x, w1_ref[0], preferred_element_type=F32)
        u = jnp.dot(x, w3_ref[0], preferred_element_type=F32)
        ef = e.astype(F32)
        gate = (jnp.where(meta_ref[:, 0:1] == ef, meta_ref[:, 2:3], 0.0)
                + jnp.where(meta_ref[:, 1:2] == ef, meta_ref[:, 3:4], 0.0))
        hm = (_silu(a) * u * gate).astype(BF16)
        acc[slot] += jnp.dot(hm, w2_ref[0], preferred_element_type=F32)

    @pl.when((flag & FLAG_LAST) != 0)
    def _():
        for_rows(lambda r: scatter_copy(tile, slot, r).start())

    @pl.when(g == n_items - 1)
    def _():
        if n_tiles >= 2:
            for_rows(lambda r: scatter_copy(n_tiles - 2, (n_tiles - 2) % 2, r).wait())
        for_rows(lambda r: scatter_copy(n_tiles - 1, (n_tiles - 1) % 2, r).wait())


def _moe_plan(meta, n_tiles, n_items):
    tm = MOE_TILE
    ea = meta[:, 0].astype(I32)
    eb = meta[:, 1].astype(I32)
    key = jnp.minimum(ea, eb) * MOE_EXPERTS + jnp.maximum(ea, eb)
    order = jnp.argsort(key).astype(I32)
    meta_s = meta[order]
    ea_s = ea[order].reshape(n_tiles, tm)
    eb_s = eb[order].reshape(n_tiles, tm)
    experts = jnp.arange(MOE_EXPERTS, dtype=I32)
    present = jnp.any((ea_s[:, :, None] == experts) | (eb_s[:, :, None] == experts), axis=1)
    flat = present.reshape(-1)
    count = jnp.sum(flat.astype(I32))
    (idx,) = jnp.nonzero(flat, size=n_items, fill_value=0)
    idx = idx.astype(I32)
    pos = jnp.arange(n_items, dtype=I32)
    valid = pos < count
    last_idx = idx[jnp.maximum(count - 1, 0)]
    idx = jnp.where(valid, idx, last_idx)
    item_tile = idx // MOE_EXPERTS
    item_exp = idx % MOE_EXPERTS
    prev_tile = jnp.concatenate([jnp.full((1,), -1, I32), item_tile[:-1]])
    next_tile = jnp.concatenate([item_tile[1:], jnp.full((1,), -1, I32)])
    first = valid & (item_tile != prev_tile)
    last = valid & ((item_tile != next_tile) | (pos == count - 1))
    flags = valid.astype(I32) * FLAG_VALID + first.astype(I32) * FLAG_FIRST + last.astype(I32) * FLAG_LAST
    return order, meta_s, item_tile, item_exp, flags


def _moe_experts(h, meta, w1, w3, w2):
    tokens = h.shape[0]
    tm = MOE_TILE
    assert tokens % tm == 0
    n_tiles = tokens // tm
    pair_classes = MOE_GROUPS * (MOE_PER_GROUP * (MOE_PER_GROUP - 1) // 2)
    n_items = min(MOE_EXPERTS * n_tiles, 2 * (n_tiles + pair_classes - 1))
    order, meta_s, item_tile, item_exp, flags = _moe_plan(meta, n_tiles, n_items)
    wspec1 = pl.BlockSpec((1, D_MODEL, MOE_HIDDEN), lambda g, tok, it, ie, fl: (ie[g], 0, 0))
    wspec2 = pl.BlockSpec((1, MOE_HIDDEN, D_MODEL), lambda g, tok, it, ie, fl: (ie[g], 0, 0))
    return pl.pallas_call(
        functools.partial(_moe_experts_body, n_tiles=n_tiles),
        grid_spec=pltpu.PrefetchScalarGridSpec(
            num_scalar_prefetch=4,
            grid=(n_items,),
            in_specs=[
                pl.BlockSpec(memory_space=pl.ANY),
                pl.BlockSpec((tm, META_COLS), lambda g, tok, it, ie, fl: (it[g], 0)),
                wspec1, wspec1, wspec2,
            ],
            out_specs=pl.BlockSpec(memory_space=pl.ANY),
            scratch_shapes=[
                pltpu.VMEM((2, tm, D_MODEL), F32),
                pltpu.VMEM((2, tm, D_MODEL), F32),
                pltpu.SemaphoreType.DMA((2,)),
                pltpu.SemaphoreType.DMA((2,)),
            ],
        ),
        out_shape=jax.ShapeDtypeStruct((tokens, D_MODEL), F32),
        compiler_params=_cparams(("arbitrary",)),
        name="moe_experts",
    )(order, item_tile, item_exp, flags, h, meta_s, w1, w3, w2)


def _moe_combine_body(x_ref, m_ref, gt_ref, fg_ref, o_ref, *, final_norm):
    x = x_ref[...] + gt_ref[...] * m_ref[...].reshape(x_ref.shape)
    if final_norm:
        x = x * lax.rsqrt(jnp.mean(x * x, axis=-1, keepdims=True) + EPS) * fg_ref[...]
    o_ref[...] = x


def _moe_combine(x, moe_out, gt, final_g, final_norm):
    nb, sl, _ = x.shape
    bt, lt = _row_tiling(nb, sl, 512)
    nl = sl // lt
    xmap = lambda i: (i // nl, i % nl, 0)
    return pl.pallas_call(
        functools.partial(_moe_combine_body, final_norm=final_norm),
        grid=((nb // bt) * nl,),
        in_specs=[
            pl.BlockSpec((bt, lt, D_MODEL), xmap),
            pl.BlockSpec((bt * lt, D_MODEL), lambda i: (i, 0)),
            pl.BlockSpec((bt, 1, D_MODEL), lambda i: (i // nl, 0, 0)),
            pl.BlockSpec((1, 1, D_MODEL), lambda i: (0, 0, 0)),
        ],
        out_specs=pl.BlockSpec((bt, lt, D_MODEL), xmap),
        out_shape=jax.ShapeDtypeStruct(x.shape, F32),
        compiler_params=_cparams(("parallel",)),
        name="moe_combine",
    )(x, moe_out, gt, final_g.reshape(1, 1, D_MODEL))


def _moe_layer(x, g, sc, sh, gt, wr_hi, wr_lo, br, w1, w3, w2, final_g, final_norm):
    h, meta = _moe_route(x, g, sc, sh, wr_hi, wr_lo, br)
    mix = _moe_experts(h, meta, w1, w3, w2)
    return _moe_combine(x, mix, gt, final_g, final_norm)


def _rope_tables(pos):
    half = ATT_HEAD_DIM // 2
    inv_freq = ROPE_THETA ** (-jnp.arange(half, dtype=F32) / half)
    ang = pos.astype(F32)[:, None] * inv_freq[None, :]
    cos = jnp.cos(ang)
    sin = jnp.sin(ang)
    cos_h = jnp.concatenate([cos, cos], axis=1)
    sin_h = jnp.concatenate([-sin, sin], axis=1)
    return jnp.tile(cos_h, (1, ATT_HEADS)), jnp.tile(sin_h, (1, ATT_HEADS))


def _router_weights(wg, bg, we, be):
    w = jnp.concatenate([wg, we], axis=1)
    w = jnp.pad(w, ((0, 0), (0, ROUTE_COLS - w.shape[1])))
    b = jnp.pad(jnp.concatenate([bg, be]), (0, ROUTE_COLS - MOE_GROUPS - MOE_EXPERTS)).reshape(1, ROUTE_COLS)
    hi = w.astype(BF16)
    lo = (w - hi.astype(F32)).astype(BF16)
    return hi, lo, b


def _even_w_in_cols(w):
    dt0 = SSD_INNER + SSD_CONV_DIM
    dt1 = dt0 + SSD_HEADS
    zeros = jnp.zeros((w.shape[0], LANES - SSD_HEADS), w.dtype)
    return jnp.concatenate([w[:, :dt0], w[:, dt1:], w[:, dt0:dt1], zeros], axis=1).astype(BF16)


def _run_group(x, mods, pos, states, caches, wts):
    nb, sl, _ = x.shape
    ssd_in, conv_in, ret_in = states
    cos, sin = _rope_tables(pos)

    def mod_parts(i):
        m = mods[i].reshape(nb, 1, N_MOD, D_MODEL)
        return [m[:, :, j] for j in range(N_MOD)]

    sh1, sc1, gt1, sh2, sc2, gt2 = mod_parts(0)
    p = _even_in(x, wts["norm_mix_g"][0], sc1, sh1, wts["even_w_in"]).reshape(nb, sl, P_WIDTH)
    if sl % SCAN_CHUNK == 0:
        valid_len, p_pad, cos_e, sin_e = SCAN_CHUNK, p, cos, sin
    else:
        assert sl < SCAN_CHUNK
        valid_len = sl
        grow = ((0, SCAN_CHUNK - sl), (0, 0))
        p_pad = jnp.pad(p, ((0, 0),) + grow)
        cos_e, sin_e = jnp.pad(cos, grow), jnp.pad(sin, grow)
    y, ssd_new, conv_new, ret_new = _even_core(
        p_pad, cos_e, sin_e, wts["ssd_conv_w"], wts["ssd_conv_b"], wts["ssd_dt_bias"], wts["ssd_a_log"],
        wts["ssd_d"], wts["ssd_norm_g"], wts["ret_norm_g"], ssd_in, conv_in, ret_in, valid_len)
    y = y[:, :sl].reshape(nb * sl, EVEN_OUT)
    x = _proj_res(y, wts["even_w_out"], x, gt1)
    x = _moe_layer(x, wts["norm_ffn_g"][0], sc2, sh2, gt2, *wts["moe"][0], wts["final_norm_g"], False)

    sh1, sc1, gt1, sh2, sc2, gt2 = mod_parts(1)
    if caches is None:
        q, k, v = _qkv_rope(x, wts["norm_mix_g"][1], sc1, sh1, wts["odd_w_qkv"], cos, sin)
        q3, k3, v3 = (a.reshape(nb, sl, ATT_WIDTH) for a in (q, k, v))
        o = _attn_prompt(q3, k3, v3)
        keep = min(ATT_WINDOW, sl)
        new_k, new_v = k3[:, sl - keep:], v3[:, sl - keep:]
    else:
        cos_t, sin_t = jnp.tile(cos, (nb, 1)), jnp.tile(sin, (nb, 1))
        q, k, v = _qkv_rope(x, wts["norm_mix_g"][1], sc1, sh1, wts["odd_w_qkv"], cos_t, sin_t)
        q3, k3, v3 = (a.reshape(nb, sl, ATT_WIDTH) for a in (q, k, v))
        cache_k, cache_v = caches
        past = cache_k.shape[1]
        o, new_k, new_v = _attn_sample(q3, k3, v3, cache_k.reshape(nb, past, ATT_WIDTH),
                                       cache_v.reshape(nb, past, ATT_WIDTH))
    x = _proj_res(o.reshape(nb * sl, ATT_WIDTH), wts["odd_w_out"], x, gt1)
    x = _moe_layer(x, wts["norm_ffn_g"][1], sc2, sh2, gt2, *wts["moe"][1], wts["final_norm_g"], True)
    heads = (ATT_HEADS, ATT_HEAD_DIM)
    return (x, ssd_new[None], conv_new[None], ret_new[None],
            new_k.reshape(1, nb, -1, *heads), new_v.reshape(1, nb, -1, *heads))


def kernel(x_prompt, x_sample, state_ssd, state_conv, state_ret, cache_k, cache_v, c_prompt, c_sample, ada_w, ada_b, norm_mix_g, norm_ffn_g, final_norm_g, even_w_in, even_w_out, ssd_conv_w, ssd_conv_b, ssd_dt_bias, ssd_a_log, ssd_d, ssd_norm_g, ret_norm_g, odd_w_qkv, odd_w_out, moe_wg, moe_bg, moe_we, moe_be, moe_w1, moe_w3, moe_w2):
    depth = ada_w.shape[0]
    assert depth == 2 and even_w_in.shape[0] == 1 and odd_w_qkv.shape[0] == 1
    bp, sp, _ = x_prompt.shape
    bs, ss, _ = x_sample.shape

    wts = {
        "norm_mix_g": norm_mix_g, "norm_ffn_g": norm_ffn_g, "final_norm_g": final_norm_g,
        "even_w_in": _even_w_in_cols(even_w_in[0]), "even_w_out": even_w_out[0].astype(BF16),
        "ssd_conv_w": ssd_conv_w[0], "ssd_conv_b": ssd_conv_b[0], "ssd_dt_bias": ssd_dt_bias[0],
        "ssd_a_log": ssd_a_log[0], "ssd_d": ssd_d[0], "ssd_norm_g": ssd_norm_g[0], "ret_norm_g": ret_norm_g[0],
        "odd_w_qkv": odd_w_qkv[0].astype(BF16), "odd_w_out": odd_w_out[0].astype(BF16),
        "moe": [
            _router_weights(moe_wg[i], moe_bg[i], moe_we[i], moe_be[i])
            + (moe_w1[i].astype(BF16), moe_w3[i].astype(BF16), moe_w2[i].astype(BF16))
            for i in range(depth)
        ],
    }
    mods = _adaln(jnp.concatenate([c_prompt, c_sample], axis=0), ada_w, ada_b)

    zeros_p = (
        jnp.zeros((bp, SSD_HEADS, SSD_HEAD_DIM, SSD_STATE), F32),
        jnp.zeros((bp, SSD_CONV - 1, SSD_CONV_DIM), F32),
        jnp.zeros((bp, RET_HEADS, RET_DV, RET_DK), F32),
    )
    out_p = _run_group(x_prompt, mods[:, :bp], jnp.arange(sp, dtype=I32), zeros_p, None, wts)
    out_s = _run_group(x_sample, mods[:, bp:], PAST_LEN + jnp.arange(ss, dtype=I32),
                       (state_ssd[0], state_conv[0], state_ret[0]), (cache_k[0], cache_v[0]), wts)
    return (out_p[0], out_s[0]) + out_p[1:] + out_s[1:]
```

```python
import functools
import math

import jax
import jax.numpy as jnp
from jax import lax
from jax.experimental import pallas as pl
from jax.experimental.pallas import tpu as pltpu

F32 = jnp.float32
BF16 = jnp.bfloat16
I32 = jnp.int32

D_MODEL = 1024
EPS = 1e-6
N_MOD = 6
SSD_HEADS = 16
SSD_HEAD_DIM = 64
SSD_INNER = SSD_HEADS * SSD_HEAD_DIM
SSD_GROUPS = 4
SSD_STATE = 64
SSD_CONV = 4
SSD_CONV_DIM = SSD_INNER + 2 * SSD_GROUPS * SSD_STATE
RET_HEADS = 8
RET_DK = 64
RET_DV = 128
RET_QK = RET_HEADS * RET_DK
RET_V = RET_HEADS * RET_DV
EVEN_OUT = SSD_INNER + RET_V
ATT_HEADS = 16
ATT_HEAD_DIM = 64
ATT_WIDTH = ATT_HEADS * ATT_HEAD_DIM
DILATED_PATTERNS = ((128, 1), (512, 4), (2048, 16))
ATT_WINDOW = 2048
PAST_LEN = 16384
ROPE_THETA = 10000.0
MOE_GROUPS = 4
MOE_PER_GROUP = 8
MOE_EXPERTS = MOE_GROUPS * MOE_PER_GROUP
MOE_HIDDEN = 256

LANES = 128
SUBLANES = 8
VMEM_LIMIT = 56 * 1024 * 1024

P_Z = 0
P_XBC = P_Z + SSD_INNER
P_Q = P_XBC + SSD_CONV_DIM
P_K = P_Q + RET_QK
P_V = P_K + RET_QK
P_G = P_V + RET_V
P_DT = P_G + RET_V
P_WIDTH = P_DT + LANES

SCAN_CHUNK = 128
ATT_BLOCK = 128
Q_SCALE = ATT_HEAD_DIM ** -0.5 * math.log2(math.e)
MASKED = -1e30
MOE_TILE = 256


def _cparams(sem):
    return pltpu.CompilerParams(dimension_semantics=sem, vmem_limit_bytes=VMEM_LIMIT)


def _row_tiling(nb, sl, target):
    if sl >= target:
        assert sl % target == 0
        return 1, target
    bt = min(nb, target // sl)
    assert nb % bt == 0
    return bt, sl


def _silu(x):
    return x * jax.nn.sigmoid(x)


def _norm_mod(x, g, sc, sh):
    ms = jnp.mean(x * x, axis=-1, keepdims=True)
    return x * lax.rsqrt(ms + EPS) * g * (1.0 + sc) + sh


def _split3(x):
    hi = x.astype(BF16)
    r1 = x - hi.astype(F32)
    mid = r1.astype(BF16)
    lo = (r1 - mid.astype(F32)).astype(BF16)
    return hi, mid, lo


def _dot_exact_rhs(x, m_bf16):
    hi, mid, lo = _split3(x)
    dot = functools.partial(jnp.dot, preferred_element_type=F32)
    return dot(hi, m_bf16) + dot(mid, m_bf16) + dot(lo, m_bf16)


def _dot_exact_lhs(m_bf16, x):
    hi, mid, lo = _split3(x)
    dot = functools.partial(jnp.dot, preferred_element_type=F32)
    return dot(m_bf16, hi) + dot(m_bf16, mid) + dot(m_bf16, lo)


def _dot_nt(a, b):
    return lax.dot_general(a, b, (((1,), (1,)), ((), ())), preferred_element_type=F32)


def _rope(a, cos, sin_signed):
    w = a.shape[-1]
    lane = lax.broadcasted_iota(I32, a.shape, 1)
    first = (lane & (ATT_HEAD_DIM // 2)) == 0
    rot = jnp.where(first, pltpu.roll(a, w - ATT_HEAD_DIM // 2, 1), pltpu.roll(a, ATT_HEAD_DIM // 2, 1))
    return a * cos + rot * sin_signed


def _adaln_body(c_ref, w_ref, b_ref, o_ref):
    a = _silu(c_ref[...]).astype(BF16)
    o_ref[0] = jnp.dot(a, w_ref[0].astype(BF16), preferred_element_type=F32) + b_ref[0]


def _adaln(c_all, ada_w, ada_b):
    nb = c_all.shape[0]
    depth, _, n6 = ada_w.shape
    tn = 1024
    return pl.pallas_call(
        _adaln_body,
        grid=(depth, n6 // tn),
        in_specs=[
            pl.BlockSpec((nb, D_MODEL), lambda i, j: (0, 0)),
            pl.BlockSpec((1, D_MODEL, tn), lambda i, j: (i, 0, j)),
            pl.BlockSpec((1, 1, tn), lambda i, j: (i, 0, j)),
        ],
        out_specs=pl.BlockSpec((1, nb, tn), lambda i, j: (i, 0, j)),
        out_shape=jax.ShapeDtypeStruct((depth, nb, n6), F32),
        compiler_params=_cparams(("parallel", "parallel")),
        name="adaln",
    )(c_all, ada_w, ada_b.reshape(depth, 1, n6))


def _even_in_body(x_ref, g_ref, sc_ref, sh_ref, w_ref, o_ref, h_scr):
    @pl.when(pl.program_id(1) == 0)
    def _():
        h = _norm_mod(x_ref[...], g_ref[...], sc_ref[...], sh_ref[...])
        h_scr[...] = h.reshape(h_scr.shape).astype(BF16)

    o_ref[...] = jnp.dot(h_scr[...], w_ref[...], preferred_element_type=F32)


def _even_in(x, g, sc, sh, w):
    nb, sl, _ = x.shape
    n = w.shape[1]
    bt, lt = _row_tiling(nb, sl, 512)
    nl = sl // lt
    tm = bt * lt
    tn = 1920
    assert n % tn == 0
    return pl.pallas_call(
        _even_in_body,
        grid=((nb // bt) * nl, n // tn),
        in_specs=[
            pl.BlockSpec((bt, lt, D_MODEL), lambda i, j: (i // nl, i % nl, 0)),
            pl.BlockSpec((1, 1, D_MODEL), lambda i, j: (0, 0, 0)),
            pl.BlockSpec((bt, 1, D_MODEL), lambda i, j: (i // nl, 0, 0)),
            pl.BlockSpec((bt, 1, D_MODEL), lambda i, j: (i // nl, 0, 0)),
            pl.BlockSpec((D_MODEL, tn), lambda i, j: (0, j)),
        ],
        out_specs=pl.BlockSpec((tm, tn), lambda i, j: (i, j)),
        out_shape=jax.ShapeDtypeStruct((nb * sl, n), F32),
        scratch_shapes=[pltpu.VMEM((tm, D_MODEL), BF16)],
        compiler_params=_cparams(("parallel", "arbitrary")),
        name="even_in",
    )(x, g.reshape(1, 1, D_MODEL), sc, sh, w)


def _even_core_body(p_ref, cos_ref, sin_ref, cw_ref, cb_ref, dtb_ref, alog_ref, dskip_ref, ssdg_ref, retg_ref,
                    expand_ref, ssd0_ref, conv0_ref, ret0_ref,
                    y_ref, ssd_out_ref, conv_out_ref, ret_out_ref,
                    xpad_scr, ssd_scr, ret_scr, yacc_scr, *, valid_len):
    cs = SCAN_CHUNK
    c = pl.program_id(1)
    nc = pl.num_programs(1)
    hd = SSD_HEAD_DIM

    @pl.when(c == 0)
    def _():
        ssd_scr[...] = ssd0_ref[0]
        ret_scr[...] = ret0_ref[0]
        xpad_scr[0:SUBLANES, :] = conv0_ref[0]

    row = lax.broadcasted_iota(I32, (cs, LANES), 0)
    col = lax.broadcasted_iota(I32, (cs, cs), 1)
    rowc = lax.broadcasted_iota(I32, (cs, cs), 0)
    causal = col <= rowc
    live = row < valid_len

    xpad_scr[SUBLANES:SUBLANES + cs, :] = p_ref[0, :, P_XBC:P_XBC + SSD_CONV_DIM]
    conv = cb_ref[...] + xpad_scr[SUBLANES:SUBLANES + cs, :] * cw_ref[SSD_CONV - 1:SSD_CONV, :]
    for back in range(1, SSD_CONV):
        tap = SSD_CONV - 1 - back
        conv = conv + xpad_scr[SUBLANES - back:SUBLANES - back + cs, :] * cw_ref[tap:tap + 1, :]
    xc = _silu(conv)

    @pl.when(c == nc - 1)
    def _():
        conv_out_ref[0] = xpad_scr[valid_len:valid_len + SUBLANES, :]

    xpad_scr[0:SUBLANES, :] = xpad_scr[cs:cs + SUBLANES, :]

    xh = xc[:, 0:SSD_INNER]
    if valid_len < cs:
        xh = jnp.where(live[:, 0:1], xh, 0.0)

    dt_in = p_ref[0, :, P_DT:P_DT + LANES] + dtb_ref[...]
    dt = jnp.maximum(dt_in, 0.0) + jnp.log1p(jnp.exp(-jnp.abs(dt_in)))
    la = dt * (-jnp.exp(alog_ref[...]))
    if valid_len < cs:
        la = jnp.where(live, la, 0.0)
    tri = causal.astype(BF16)
    acum = _dot_exact_lhs(tri, la)
    acum_last = acum[cs - 1:cs, :]
    to_end = jnp.exp(acum_last - acum)
    eacum = jnp.exp(acum)
    cdecay = jnp.broadcast_to(jnp.exp(acum_last), (2 * SUBLANES, LANES))
    stacked = jnp.concatenate([dt, to_end, eacum, cdecay], axis=0)
    stacked_x = _dot_exact_rhs(stacked, expand_ref[...])
    dt_x = stacked_x[0:cs]
    to_end_x = stacked_x[cs:2 * cs]
    eacum_x = stacked_x[2 * cs:3 * cs]
    cdecay_x = stacked_x[3 * cs:3 * cs + 1]
    acum_t = acum.T

    xs = xh * dt_x
    xs_b = xs.astype(BF16)
    xs_te_t = (xs * to_end_x).T.astype(BF16)

    for grp in range(SSD_GROUPS):
        b_g = xc[:, SSD_INNER + grp * SSD_STATE:SSD_INNER + (grp + 1) * SSD_STATE].astype(BF16)
        c_g = xc[:, SSD_INNER + (SSD_GROUPS + grp) * SSD_STATE:SSD_INNER + (SSD_GROUPS + grp + 1) * SSD_STATE].astype(BF16)
        s_g = _dot_nt(c_g, b_g)
        for hh in range(SSD_HEADS // SSD_GROUPS):
            h = grp * (SSD_HEADS // SSD_GROUPS) + hh
            lo, hi = h * hd, (h + 1) * hd
            diff = acum[:, h:h + 1] - acum_t[h:h + 1, :]
            decay = jnp.exp(jnp.where(causal, diff, -jnp.inf))
            m = (s_g * decay).astype(BF16)
            y_h = jnp.dot(m, xs_b[:, lo:hi], preferred_element_type=F32)
            h_prev = ssd_scr[h]
            y_h = y_h + _dot_nt(c_g, h_prev.astype(BF16)) * eacum_x[:, lo:hi]
            yacc_scr[:, lo:hi] = y_h
            ssd_scr[h] = h_prev * cdecay_x[:, lo:hi] + jnp.dot(xs_te_t[lo:hi, :], b_g, preferred_element_type=F32)

    y = yacc_scr[...] + dskip_ref[...] * xh
    y = y * _silu(p_ref[0, :, P_Z:P_Z + SSD_INNER])
    y = y * lax.rsqrt(jnp.mean(y * y, axis=-1, keepdims=True) + EPS) * ssdg_ref[...]
    y_ref[0, :, 0:SSD_INNER] = y.astype(BF16)

    cos = cos_ref[:, 0:RET_QK]
    sin = sin_ref[:, 0:RET_QK]
    rq = _rope(p_ref[0, :, P_Q:P_Q + RET_QK], cos, sin).astype(BF16)
    rk = (_rope(p_ref[0, :, P_K:P_K + RET_QK], cos, sin) * RET_DK ** -0.5).astype(BF16)
    rv = p_ref[0, :, P_V:P_V + RET_V]
    if valid_len < cs:
        rv = jnp.where(live[:, 0:1], rv, 0.0)
    steps_r = jnp.minimum(rowc + 1, valid_len).astype(F32)
    steps_c = jnp.minimum(col + 1, valid_len).astype(F32)
    steps_l = jnp.minimum(row + 1, valid_len).astype(F32)
    for h in range(RET_HEADS):
        lg = math.log1p(-(2.0 ** (-5.0 - h)))
        q_h = rq[:, h * RET_DK:(h + 1) * RET_DK]
        k_h = rk[:, h * RET_DK:(h + 1) * RET_DK]
        v_h = rv[:, h * RET_DV:(h + 1) * RET_DV]
        decay = jnp.exp(jnp.where(causal, lg * (steps_r - steps_c), -jnp.inf))
        m = (_dot_nt(q_h, k_h) * decay).astype(BF16)
        o_h = jnp.dot(m, v_h.astype(BF16), preferred_element_type=F32)
        s_prev = ret_scr[h]
        o_h = o_h + _dot_nt(q_h, s_prev.astype(BF16)) * jnp.exp(lg * steps_l)
        v_te_t = (v_h * jnp.exp(lg * (float(valid_len) - steps_l))).T.astype(BF16)
        ret_scr[h] = s_prev * math.exp(lg * valid_len) + jnp.dot(v_te_t, k_h, preferred_element_type=F32)
        o_h = o_h * lax.rsqrt(jnp.mean(o_h * o_h, axis=-1, keepdims=True) + EPS) * retg_ref[:, h * RET_DV:(h + 1) * RET_DV]
        o_h = o_h * _silu(p_ref[0, :, P_G + h * RET_DV:P_G + (h + 1) * RET_DV])
        y_ref[0, :, SSD_INNER + h * RET_DV:SSD_INNER + (h + 1) * RET_DV] = o_h.astype(BF16)

    @pl.when(c == nc - 1)
    def _():
        ssd_out_ref[0] = ssd_scr[...]
        ret_out_ref[0] = ret_scr[...]


def _even_core(p, cos, sin, conv_w, conv_b, dt_bias, a_log, d_skip, ssd_norm_g, ret_norm_g, ssd0, conv0, ret0,
               valid_len):
    nb, sl, _ = p.shape
    cs = SCAN_CHUNK
    nc = sl // cs
    assert sl % cs == 0 and (valid_len == cs or nc == 1) and valid_len % SUBLANES == 0
    pad = LANES - SSD_HEADS
    dtb = jnp.pad(dt_bias, (0, pad)).reshape(1, LANES)
    alog = jnp.pad(a_log, (0, pad)).reshape(1, LANES)
    dskip_x = jnp.repeat(d_skip, SSD_HEAD_DIM).reshape(1, SSD_INNER)
    expand = (jnp.arange(LANES)[:, None] == (jnp.arange(SSD_INNER)[None, :] // SSD_HEAD_DIM)).astype(BF16)
    conv0p = jnp.pad(conv0, ((0, 0), (SUBLANES - (SSD_CONV - 1), 0), (0, 0)))
    const2 = lambda b, c: (0, 0)
    st4 = lambda b, c: (b, 0, 0, 0)
    y, ssd_new, conv_new, ret_new = pl.pallas_call(
        functools.partial(_even_core_body, valid_len=valid_len),
        grid=(nb, nc),
        in_specs=[
            pl.BlockSpec((1, cs, P_WIDTH), lambda b, c: (b, c, 0)),
            pl.BlockSpec((cs, ATT_WIDTH), lambda b, c: (c, 0)),
            pl.BlockSpec((cs, ATT_WIDTH), lambda b, c: (c, 0)),
            pl.BlockSpec((SSD_CONV, SSD_CONV_DIM), const2),
            pl.BlockSpec((1, SSD_CONV_DIM), const2),
            pl.BlockSpec((1, LANES), const2),
            pl.BlockSpec((1, LANES), const2),
            pl.BlockSpec((1, SSD_INNER), const2),
            pl.BlockSpec((1, SSD_INNER), const2),
            pl.BlockSpec((1, RET_V), const2),
            pl.BlockSpec((LANES, SSD_INNER), const2),
            pl.BlockSpec((1, SSD_HEADS, SSD_HEAD_DIM, SSD_STATE), st4),
            pl.BlockSpec((1, SUBLANES, SSD_CONV_DIM), lambda b, c: (b, 0, 0)),
            pl.BlockSpec((1, RET_HEADS, RET_DV, RET_DK), st4),
        ],
        out_specs=[
            pl.BlockSpec((1, cs, EVEN_OUT), lambda b, c: (b, c, 0)),
            pl.BlockSpec((1, SSD_HEADS, SSD_HEAD_DIM, SSD_STATE), st4),
            pl.BlockSpec((1, SUBLANES, SSD_CONV_DIM), lambda b, c: (b, 0, 0)),
            pl.BlockSpec((1, RET_HEADS, RET_DV, RET_DK), st4),
        ],
        out_shape=[
            jax.ShapeDtypeStruct((nb, sl, EVEN_OUT), BF16),
            jax.ShapeDtypeStruct((nb, SSD_HEADS, SSD_HEAD_DIM, SSD_STATE), F32),
            jax.ShapeDtypeStruct((nb, SUBLANES, SSD_CONV_DIM), F32),
            jax.ShapeDtypeStruct((nb, RET_HEADS, RET_DV, RET_DK), F32),
        ],
        scratch_shapes=[
            pltpu.VMEM((cs + SUBLANES, SSD_CONV_DIM), F32),
            pltpu.VMEM((SSD_HEADS, SSD_HEAD_DIM, SSD_STATE), F32),
            pltpu.VMEM((RET_HEADS, RET_DV, RET_DK), F32),
            pltpu.VMEM((cs, SSD_INNER), F32),
        ],
        compiler_params=_cparams(("parallel", "arbitrary")),
        name="even_core",
    )(p, cos, sin, conv_w, conv_b.reshape(1, SSD_CONV_DIM), dtb, alog, dskip_x,
      ssd_norm_g.reshape(1, SSD_INNER), ret_norm_g.reshape(1, RET_V), expand, ssd0, conv0p, ret0)
    return y, ssd_new, conv_new[:, SUBLANES - (SSD_CONV - 1):], ret_new


def _proj_res_body(y_ref, w_ref, x_ref, gt_ref, o_ref):
    m = jnp.dot(y_ref[...], w_ref[...], preferred_element_type=F32)
    o_ref[...] = x_ref[...] + gt_ref[...] * m.reshape(x_ref.shape)


def _proj_res(y, w, x, gt):
    nb, sl, _ = x.shape
    kdim = y.shape[1]
    bt, lt = _row_tiling(nb, sl, 512)
    nl = sl // lt
    xmap = lambda i: (i // nl, i % nl, 0)
    return pl.pallas_call(
        _proj_res_body,
        grid=((nb // bt) * nl,),
        in_specs=[
            pl.BlockSpec((bt * lt, kdim), lambda i: (i, 0)),
            pl.BlockSpec((kdim, D_MODEL), lambda i: (0, 0)),
            pl.BlockSpec((bt, lt, D_MODEL), xmap),
            pl.BlockSpec((bt, 1, D_MODEL), lambda i: (i // nl, 0, 0)),
        ],
        out_specs=pl.BlockSpec((bt, lt, D_MODEL), xmap),
        out_shape=jax.ShapeDtypeStruct(x.shape, F32),
        compiler_params=_cparams(("parallel",)),
        name="proj_res",
    )(y, w, x, gt)


def _qkv_body(x_ref, g_ref, sc_ref, sh_ref, w_ref, cos_ref, sin_ref, q_ref, k_ref, v_ref):
    h = _norm_mod(x_ref[...], g_ref[...], sc_ref[...], sh_ref[...])
    h = h.reshape(q_ref.shape).astype(BF16)
    cos = cos_ref[...]
    sin = sin_ref[...]
    dot = functools.partial(jnp.dot, preferred_element_type=F32)
    q_ref[...] = _rope(dot(h, w_ref[:, 0:ATT_WIDTH]), cos, sin) * Q_SCALE
    k_ref[...] = _rope(dot(h, w_ref[:, ATT_WIDTH:2 * ATT_WIDTH]), cos, sin)
    v_ref[...] = dot(h, w_ref[:, 2 * ATT_WIDTH:3 * ATT_WIDTH])


def _qkv_rope(x, g, sc, sh, w, cos, sin):
    nb, sl, _ = x.shape
    bt, lt = _row_tiling(nb, sl, 256)
    nl = sl // lt
    tm = bt * lt
    assert cos.shape[0] in (sl, nb * sl) and (cos.shape[0] == sl) == (bt == 1)
    ntab = cos.shape[0] // tm
    out = jax.ShapeDtypeStruct((nb * sl, ATT_WIDTH), F32)
    ospec = pl.BlockSpec((tm, ATT_WIDTH), lambda i: (i, 0))
    tspec = pl.BlockSpec((tm, ATT_WIDTH), lambda i: (i % ntab, 0))
    return pl.pallas_call(
        _qkv_body,
        grid=((nb // bt) * nl,),
        in_specs=[
            pl.BlockSpec((bt, lt, D_MODEL), lambda i: (i // nl, i % nl, 0)),
            pl.BlockSpec((1, 1, D_MODEL), lambda i: (0, 0, 0)),
            pl.BlockSpec((bt, 1, D_MODEL), lambda i: (i // nl, 0, 0)),
            pl.BlockSpec((bt, 1, D_MODEL), lambda i: (i // nl, 0, 0)),
            pl.BlockSpec((D_MODEL, 3 * ATT_WIDTH), lambda i: (0, 0)),
            tspec, tspec,
        ],
        out_specs=[ospec, ospec, ospec],
        out_shape=[out, out, out],
        compiler_params=_cparams(("parallel",)),
        name="qkv_rope",
    )(x, g.reshape(1, 1, D_MODEL), sc, sh, w, cos, sin)


def _attn_prompt_body(q_ref, k_ref, v_ref, o_ref, kt_ref, vt_ref, ob_scr, lse_scr, *, seq):
    blk = ATT_BLOCK
    half = ATT_HEAD_DIM
    lane = lax.broadcasted_iota(I32, (blk, LANES), 1)
    head_a = lane < half
    rowi = lax.broadcasted_iota(I32, (2 * blk, blk), 0) & (blk - 1)
    coli = lax.broadcasted_iota(I32, (2 * blk, blk), 1)
    mask_cur = coli <= rowi
    mask_prev = coli >= rowi
    ones = jnp.ones((blk, LANES), BF16)

    for i in range(seq // blk):
        rows = slice(i * blk, (i + 1) * blk)
        kt_ref[0, :, :, rows] = k_ref[0, rows, :].T.reshape(2, half, blk)
        vt_ref[0, :, :, rows] = v_ref[0, rows, :].T.reshape(2, half, blk)

    def block(gi, dil, start, pstart):
        def ld(ref, s0):
            if dil == 1:
                return ref[0, pl.ds(s0, blk), :]
            return ref[0, pl.ds(s0, blk, stride=dil), :]

        def st(ref, val):
            if dil == 1:
                ref[gi, pl.ds(start, blk), :] = val
            else:
                ref[gi, pl.ds(start, blk, stride=dil), :] = val

        q = ld(q_ref, start)
        q2 = jnp.concatenate([jnp.where(head_a, q, 0.0), jnp.where(head_a, 0.0, q)], axis=0).astype(BF16)
        vce = jnp.concatenate([ld(v_ref, start).astype(BF16), ones], axis=1)
        sc = jnp.where(mask_cur, _dot_nt(q2, ld(k_ref, start).astype(BF16)), MASKED)
        if pstart is None:
            m = jnp.max(sc, axis=1, keepdims=True)
            oe = jnp.dot(jnp.exp2(sc - m).astype(BF16), vce, preferred_element_type=F32)
        else:
            vpe = jnp.concatenate([ld(v_ref, pstart).astype(BF16), ones], axis=1)
            sp = jnp.where(mask_prev, _dot_nt(q2, ld(k_ref, pstart).astype(BF16)), MASKED)
            m = jnp.max(jnp.maximum(sc, sp), axis=1, keepdims=True)
            oe = jnp.dot(jnp.exp2(sc - m).astype(BF16), vce, preferred_element_type=F32)
            oe = oe + jnp.dot(jnp.exp2(sp - m).astype(BF16), vpe, preferred_element_type=F32)
        num = jnp.where(head_a, oe[0:blk, 0:LANES], oe[blk:2 * blk, 0:LANES])
        den = jnp.where(head_a, oe[0:blk, LANES:2 * LANES], oe[blk:2 * blk, LANES:2 * LANES])
        mm = jnp.where(head_a, m[0:blk], m[blk:2 * blk])
        st(ob_scr, num / den)
        st(lse_scr, mm + jnp.log2(den))

    for gi, (window, dil) in enumerate(DILATED_PATTERNS):
        assert window // dil == blk and (seq // dil) % blk == 0
        per_residue = seq // dil // blk

        def first(r, carry, gi=gi, dil=dil):
            block(gi, dil, r, None)
            return carry

        def later(t, carry, gi=gi, dil=dil):
            start = t % dil + (1 + t // dil) * (blk * dil)
            block(gi, dil, start, start - blk * dil)
            return carry

        lax.fori_loop(0, dil, first, 0, unroll=min(2, dil))
        if per_residue > 1:
            lax.fori_loop(0, dil * (per_residue - 1), later, 0, unroll=2)

    def merge(i, carry):
        rows = pl.ds(pl.multiple_of(i * blk, blk), blk)
        l0 = lse_scr[0, rows, :]
        l1 = lse_scr[1, rows, :]
        l2 = lse_scr[2, rows, :]
        m = jnp.maximum(jnp.maximum(l0, l1), l2)
        w0 = jnp.exp2(l0 - m)
        w1 = jnp.exp2(l1 - m)
        w2 = jnp.exp2(l2 - m)
        o = (w0 * ob_scr[0, rows, :] + w1 * ob_scr[1, rows, :] + w2 * ob_scr[2, rows, :]) / (w0 + w1 + w2)
        o_ref[0, rows, :] = o.astype(BF16)
        return carry

    lax.fori_loop(0, seq // blk, merge, 0)


def _attn_prompt(q, k, v):
    nb, seq, _ = q.shape
    npair = ATT_WIDTH // LANES
    spec = pl.BlockSpec((1, seq, LANES), lambda b, h: (b, 0, h))
    tspec = pl.BlockSpec((1, 2, ATT_HEAD_DIM, seq), lambda b, h: (b, h, 0, 0))
    tshape = jax.ShapeDtypeStruct((nb, ATT_HEADS, ATT_HEAD_DIM, seq), F32)
    return pl.pallas_call(
        functools.partial(_attn_prompt_body, seq=seq),
        grid=(nb, npair),
        in_specs=[spec, spec, spec],
        out_specs=[spec, tspec, tspec],
        out_shape=[jax.ShapeDtypeStruct((nb, seq, ATT_WIDTH), BF16), tshape, tshape],
        scratch_shapes=[
            pltpu.VMEM((len(DILATED_PATTERNS), seq, LANES), F32),
            pltpu.VMEM((len(DILATED_PATTERNS), seq, LANES), F32),
        ],
        compiler_params=_cparams(("parallel", "parallel")),
        name="attn_prompt",
    )(q, k, v)


SAMPLE_HEADS_PER_STEP = 4
SAMPLE_ROWS = 2 * SUBLANES


def _attn_sample_body(q_ref, kn_ref, vn_ref, ck_ref, cv_ref, o_ref, nk_ref, nv_ref, *, past, new):
    rows = SAMPLE_ROWS
    keys = past + LANES
    lane = lax.broadcasted_iota(I32, (ATT_HEAD_DIM, LANES), 1)
    is_new = lane >= LANES - new
    t_idx = lax.broadcasted_iota(I32, (rows, keys), 0) & (new - 1)
    k_idx = lax.broadcasted_iota(I32, (rows, keys), 1)
    dist = past + t_idx - k_idx
    oks = [(dist >= 0) & (dist <= window) & ((dist & (dil - 1)) == 0) for window, dil in DILATED_PATTERNS]
    zpad = jnp.zeros((ATT_HEAD_DIM, LANES - new), F32)

    for h in range(SAMPLE_HEADS_PER_STEP):
        alls = []
        for c_ref, n_ref, out_ref in ((ck_ref, kn_ref, nk_ref), (cv_ref, vn_ref, nv_ref)):
            old = c_ref[0, h]
            fresh = n_ref[0, h]
            rolled = pltpu.roll(old, past - new, 1)
            out_ref[0, h] = rolled
            out_ref[0, h, :, past - LANES:past] = jnp.where(
                is_new, jnp.concatenate([zpad, fresh], axis=1), rolled[:, past - LANES:past])
            alls.append(jnp.concatenate([old, fresh, zpad], axis=1).astype(BF16))
        k_all, v_all = alls
        q = jnp.concatenate([q_ref[0, h], jnp.zeros((rows - new, ATT_HEAD_DIM), F32)], axis=0).astype(BF16)
        s = jnp.dot(q, k_all, preferred_element_type=F32)
        ps, dens, lses = [], [], []
        for ok in oks:
            sg = jnp.where(ok, s, MASKED)
            m = jnp.max(sg, axis=1, keepdims=True)
            p = jnp.exp2(sg - m)
            den = jnp.sum(p, axis=1, keepdims=True)
            ps.append(p.astype(BF16))
            dens.append(den)
            lses.append(m + jnp.log2(den))
        o_all = _dot_nt(jnp.concatenate(ps, axis=0), v_all)
        m = jnp.maximum(jnp.maximum(lses[0], lses[1]), lses[2])
        ws = [jnp.exp2(l - m) for l in lses]
        o = sum(w * o_all[i * rows:(i + 1) * rows] / d for i, (w, d) in enumerate(zip(ws, dens)))
        o = o / (ws[0] + ws[1] + ws[2])
        o_ref[0, h] = o[0:new].astype(BF16)


def _attn_sample(q, kn, vn, cache_k, cache_v):
    nb, new, _ = q.shape
    past = cache_k.shape[1]
    hps = SAMPLE_HEADS_PER_STEP
    assert past >= ATT_WINDOW and past % LANES == 0 and new == SUBLANES and ATT_HEADS % hps == 0
    heads = (nb, new, ATT_HEADS, ATT_HEAD_DIM)
    q4 = q.reshape(heads).transpose(0, 2, 1, 3)
    kn4 = kn.reshape(heads).transpose(0, 2, 3, 1)
    vn4 = vn.reshape(heads).transpose(0, 2, 3, 1)
    ck = cache_k.transpose(0, 2, 3, 1)
    cv = cache_v.transpose(0, 2, 3, 1)
    hmap = lambda b, g: (b, g, 0, 0)
    qspec = pl.BlockSpec((1, hps, new, ATT_HEAD_DIM), hmap)
    nspec = pl.BlockSpec((1, hps, ATT_HEAD_DIM, new), hmap)
    cspec = pl.BlockSpec((1, hps, ATT_HEAD_DIM, past), hmap)
    o4, nk, nv = pl.pallas_call(
        functools.partial(_attn_sample_body, past=past, new=new),
        grid=(nb, ATT_HEADS // hps),
        in_specs=[qspec, nspec, nspec, cspec, cspec],
        out_specs=[qspec, cspec, cspec],
        out_shape=[
            jax.ShapeDtypeStruct((nb, ATT_HEADS, new, ATT_HEAD_DIM), BF16),
            jax.ShapeDtypeStruct(ck.shape, F32),
            jax.ShapeDtypeStruct(cv.shape, F32),
        ],
        compiler_params=_cparams(("parallel", "parallel")),
        name="attn_sample",
    )(q4, kn4, vn4, ck, cv)
    o = o4.transpose(0, 2, 1, 3).reshape(nb, new, ATT_WIDTH)
    return o, nk.transpose(0, 3, 1, 2), nv.transpose(0, 3, 1, 2)


ROUTE_COLS = LANES
META_COLS = SUBLANES


def _moe_route_body(x_ref, g_ref, sc_ref, sh_ref, whi_ref, wlo_ref, b_ref, h_ref, meta_ref):
    tm = h_ref.shape[0]
    h = _norm_mod(x_ref[...], g_ref[...], sc_ref[...], sh_ref[...]).reshape(tm, D_MODEL)
    h_ref[...] = h
    hi = h.astype(BF16)
    lo = (h - hi.astype(F32)).astype(BF16)
    dot = functools.partial(jnp.dot, preferred_element_type=F32)
    logits = dot(hi, whi_ref[...]) + dot(lo, whi_ref[...]) + dot(hi, wlo_ref[...]) + b_ref[...]
    lane = lax.broadcasted_iota(I32, (tm, ROUTE_COLS), 1).astype(F32)
    big = float(ROUTE_COLS)
    neg = -jnp.inf
    gl = jnp.where(lane < MOE_GROUPS, logits, neg)
    gmax = jnp.max(gl, axis=1, keepdims=True)
    g_idx = jnp.min(jnp.where(gl == gmax, lane, big), axis=1, keepdims=True)
    g_w = 1.0 / jnp.sum(jnp.exp(gl - gmax), axis=1, keepdims=True)
    first = MOE_GROUPS + MOE_PER_GROUP * g_idx
    el = jnp.where((lane >= first) & (lane < first + MOE_PER_GROUP), logits, neg)
    v1 = jnp.max(el, axis=1, keepdims=True)
    i1 = jnp.min(jnp.where(el == v1, lane, big), axis=1, keepdims=True)
    el2 = jnp.where(lane == i1, neg, el)
    v2 = jnp.max(el2, axis=1, keepdims=True)
    i2 = jnp.min(jnp.where(el2 == v2, lane, big), axis=1, keepdims=True)
    t = jnp.exp(v2 - v1)
    w1 = g_w / (1.0 + t)
    w2 = g_w * t / (1.0 + t)
    meta = jnp.where(lane == 0, i1 - MOE_GROUPS,
                     jnp.where(lane == 1, i2 - MOE_GROUPS, jnp.where(lane == 2, w1, jnp.where(lane == 3, w2, 0.0))))
    meta_ref[...] = meta[:, 0:META_COLS]


def _moe_route(x, g, sc, sh, w_hi, w_lo, bias):
    nb, sl, _ = x.shape
    bt, lt = _row_tiling(nb, sl, 512)
    nl = sl // lt
    tm = bt * lt
    c2 = lambda i: (0, 0)
    return pl.pallas_call(
        _moe_route_body,
        grid=((nb // bt) * nl,),
        in_specs=[
            pl.BlockSpec((bt, lt, D_MODEL), lambda i: (i // nl, i % nl, 0)),
            pl.BlockSpec((1, 1, D_MODEL), lambda i: (0, 0, 0)),
            pl.BlockSpec((bt, 1, D_MODEL), lambda i: (i // nl, 0, 0)),
            pl.BlockSpec((bt, 1, D_MODEL), lambda i: (i // nl, 0, 0)),
            pl.BlockSpec((D_MODEL, ROUTE_COLS), c2),
            pl.BlockSpec((D_MODEL, ROUTE_COLS), c2),
            pl.BlockSpec((1, ROUTE_COLS), c2),
        ],
        out_specs=[pl.BlockSpec((tm, D_MODEL), lambda i: (i, 0)), pl.BlockSpec((tm, META_COLS), lambda i: (i, 0))],
        out_shape=[jax.ShapeDtypeStruct((nb * sl, D_MODEL), F32), jax.ShapeDtypeStruct((nb * sl, META_COLS), F32)],
        compiler_params=_cparams(("parallel",)),
        name="moe_route",
    )(x, g.reshape(1, 1, D_MODEL), sc, sh, w_hi, w_lo, bias)


FLAG_VALID, FLAG_FIRST, FLAG_LAST = 1, 2, 4


def _moe_experts_body(tok_ref, item_tile_ref, item_exp_ref, item_flag_ref,
                      h_hbm, meta_ref, w1_ref, w3_ref, w2_ref, out_hbm,
                      xbuf, acc, gsem, ssem, *, n_tiles):
    tm = MOE_TILE
    g = pl.program_id(0)
    n_items = pl.num_programs(0)
    tile = item_tile_ref[g]
    e = item_exp_ref[g]
    flag = item_flag_ref[g]
    slot = tile % 2

    def gather_start(t, s):
        for r in range(tm):
            tok = tok_ref[t * tm + r]
            pltpu.make_async_copy(h_hbm.at[pl.ds(tok, 1)], xbuf.at[s, pl.ds(r, 1)], gsem.at[s]).start()

    def gather_wait(s):
        pltpu.make_async_copy(h_hbm.at[pl.ds(0, tm)], xbuf.at[s], gsem.at[s]).wait()

    def scatter_start(t, s):
        for r in range(tm):
            tok = tok_ref[t * tm + r]
            pltpu.make_async_copy(acc.at[s, pl.ds(r, 1)], out_hbm.at[pl.ds(tok, 1)], ssem.at[s]).start()

    def scatter_wait(s):
        pltpu.make_async_copy(acc.at[s], out_hbm.at[pl.ds(0, tm)], ssem.at[s]).wait()

    @pl.when(g == 0)
    def _():
        gather_start(0, 0)

    @pl.when((flag & FLAG_FIRST) != 0)
    def _():
        @pl.when(tile >= 2)
        def _():
            scatter_wait(slot)

        gather_wait(slot)

        @pl.when(tile + 1 < n_tiles)
        def _():
            gather_start(tile + 1, 1 - slot)

        acc[slot] = jnp.zeros((tm, D_MODEL), F32)

    @pl.when((flag & FLAG_VALID) != 0)
    def _():
        x = xbuf[slot].astype(BF16)
        a = jnp.dot(x, w1_ref[0], preferred_element_type=F32)
        u = jnp.dot(x, w3_ref[0], preferred_element_type=F32)
        ef = e.astype(F32)
        gate = (jnp.where(meta_ref[:, 0:1] == ef, meta_ref[:, 2:3], 0.0)
                + jnp.where(meta_ref[:, 1:2] == ef, meta_ref[:, 3:4], 0.0))
        hm = (_silu(a) * u * gate).astype(BF16)
        acc[slot] += jnp.dot(hm, w2_ref[0], preferred_element_type=F32)

    @pl.when((flag & FLAG_LAST) != 0)
    def _():
        scatter_start(tile, slot)

    @pl.when(g == n_items - 1)
    def _():
        if n_tiles >= 2:
            scatter_wait((n_tiles - 2) % 2)
        scatter_wait((n_tiles - 1) % 2)


def _moe_plan(meta, n_tiles, n_items):
    tm = MOE_TILE
    ea = meta[:, 0].astype(I32)
    eb = meta[:, 1].astype(I32)
    key = jnp.minimum(ea, eb) * MOE_EXPERTS + jnp.maximum(ea, eb)
    order = jnp.argsort(key).astype(I32)
    meta_s = meta[order]
    ea_s = ea[order].reshape(n_tiles, tm)
    eb_s = eb[order].reshape(n_tiles, tm)
    experts = jnp.arange(MOE_EXPERTS, dtype=I32)
    present = jnp.any((ea_s[:, :, None] == experts) | (eb_s[:, :, None] == experts), axis=1)
    flat = present.reshape(-1)
    count = jnp.sum(flat.astype(I32))
    (idx,) = jnp.nonzero(flat, size=n_items, fill_value=0)
    idx = idx.astype(I32)
    pos = jnp.arange(n_items, dtype=I32)
    valid = pos < count
    last_idx = idx[jnp.maximum(count - 1, 0)]
    idx = jnp.where(valid, idx, last_idx)
    item_tile = idx // MOE_EXPERTS
    item_exp = idx % MOE_EXPERTS
    prev_tile = jnp.concatenate([jnp.full((1,), -1, I32), item_tile[:-1]])
    next_tile = jnp.concatenate([item_tile[1:], jnp.full((1,), -1, I32)])
    first = valid & (item_tile != prev_tile)
    last = valid & ((item_tile != next_tile) | (pos == count - 1))
    flags = valid.astype(I32) * FLAG_VALID + first.astype(I32) * FLAG_FIRST + last.astype(I32) * FLAG_LAST
    return order, meta_s, item_tile, item_exp, flags


def _moe_experts(h, meta, w1, w3, w2):
    tokens = h.shape[0]
    tm = MOE_TILE
    assert tokens % tm == 0
    n_tiles = tokens // tm
    pair_classes = MOE_GROUPS * (MOE_PER_GROUP * (MOE_PER_GROUP - 1) // 2)
    n_items = min(MOE_EXPERTS * n_tiles, 2 * (n_tiles + pair_classes - 1))
    order, meta_s, item_tile, item_exp, flags = _moe_plan(meta, n_tiles, n_items)
    wspec1 = pl.BlockSpec((1, D_MODEL, MOE_HIDDEN), lambda g, tok, it, ie, fl: (ie[g], 0, 0))
    wspec2 = pl.BlockSpec((1, MOE_HIDDEN, D_MODEL), lambda g, tok, it, ie, fl: (ie[g], 0, 0))
    return pl.pallas_call(
        functools.partial(_moe_experts_body, n_tiles=n_tiles),
        grid_spec=pltpu.PrefetchScalarGridSpec(
            num_scalar_prefetch=4,
            grid=(n_items,),
            in_specs=[
                pl.BlockSpec(memory_space=pl.ANY),
                pl.BlockSpec((tm, META_COLS), lambda g, tok, it, ie, fl: (it[g], 0)),
                wspec1, wspec1, wspec2,
            ],
            out_specs=pl.BlockSpec(memory_space=pl.ANY),
            scratch_shapes=[
                pltpu.VMEM((2, tm, D_MODEL), F32),
                pltpu.VMEM((2, tm, D_MODEL), F32),
                pltpu.SemaphoreType.DMA((2,)),
                pltpu.SemaphoreType.DMA((2,)),
            ],
        ),
        out_shape=jax.ShapeDtypeStruct((tokens, D_MODEL), F32),
        compiler_params=_cparams(("arbitrary",)),
        name="moe_experts",
    )(order, item_tile, item_exp, flags, h, meta_s, w1, w3, w2)


def _moe_combine_body(x_ref, m_ref, gt_ref, fg_ref, o_ref, *, final_norm):
    x = x_ref[...] + gt_ref[...] * m_ref[...].reshape(x_ref.shape)
    if final_norm:
        x = x * lax.rsqrt(jnp.mean(x * x, axis=-1, keepdims=True) + EPS) * fg_ref[...]
    o_ref[...] = x


def _moe_combine(x, moe_out, gt, final_g, final_norm):
    nb, sl, _ = x.shape
    bt, lt = _row_tiling(nb, sl, 512)
    nl = sl // lt
    xmap = lambda i: (i // nl, i % nl, 0)
    return pl.pallas_call(
        functools.partial(_moe_combine_body, final_norm=final_norm),
        grid=((nb // bt) * nl,),
        in_specs=[
            pl.BlockSpec((bt, lt, D_MODEL), xmap),
            pl.BlockSpec((bt * lt, D_MODEL), lambda i: (i, 0)),
            pl.BlockSpec((bt, 1, D_MODEL), lambda i: (i // nl, 0, 0)),
            pl.BlockSpec((1, 1, D_MODEL), lambda i: (0, 0, 0)),
        ],
        out_specs=pl.BlockSpec((bt, lt, D_MODEL), xmap),
        out_shape=jax.ShapeDtypeStruct(x.shape, F32),
        compiler_params=_cparams(("parallel",)),
        name="moe_combine",
    )(x, moe_out, gt, final_g.reshape(1, 1, D_MODEL))


def _moe_layer(x, g, sc, sh, gt, wr_hi, wr_lo, br, w1, w3, w2, final_g, final_norm):
    h, meta = _moe_route(x, g, sc, sh, wr_hi, wr_lo, br)
    mix = _moe_experts(h, meta, w1, w3, w2)
    return _moe_combine(x, mix, gt, final_g, final_norm)


def _rope_tables(pos):
    half = ATT_HEAD_DIM // 2
    inv_freq = ROPE_THETA ** (-jnp.arange(half, dtype=F32) / half)
    ang = pos.astype(F32)[:, None] * inv_freq[None, :]
    cos = jnp.cos(ang)
    sin = jnp.sin(ang)
    cos_h = jnp.concatenate([cos, cos], axis=1)
    sin_h = jnp.concatenate([-sin, sin], axis=1)
    return jnp.tile(cos_h, (1, ATT_HEADS)), jnp.tile(sin_h, (1, ATT_HEADS))


def _router_weights(wg, bg, we, be):
    w = jnp.concatenate([wg, we], axis=1)
    w = jnp.pad(w, ((0, 0), (0, ROUTE_COLS - w.shape[1])))
    b = jnp.pad(jnp.concatenate([bg, be]), (0, ROUTE_COLS - MOE_GROUPS - MOE_EXPERTS)).reshape(1, ROUTE_COLS)
    hi = w.astype(BF16)
    lo = (w - hi.astype(F32)).astype(BF16)
    return hi, lo, b


def _even_w_in_cols(w):
    dt0 = SSD_INNER + SSD_CONV_DIM
    dt1 = dt0 + SSD_HEADS
    zeros = jnp.zeros((w.shape[0], LANES - SSD_HEADS), w.dtype)
    return jnp.concatenate([w[:, :dt0], w[:, dt1:], w[:, dt0:dt1], zeros], axis=1).astype(BF16)


def _run_group(x, mods, pos, states, caches, wts):
    nb, sl, _ = x.shape
    ssd_in, conv_in, ret_in = states
    cos, sin = _rope_tables(pos)

    def mod_parts(i):
        m = mods[i].reshape(nb, 1, N_MOD, D_MODEL)
        return [m[:, :, j] for j in range(N_MOD)]

    sh1, sc1, gt1, sh2, sc2, gt2 = mod_parts(0)
    p = _even_in(x, wts["norm_mix_g"][0], sc1, sh1, wts["even_w_in"]).reshape(nb, sl, P_WIDTH)
    if sl % SCAN_CHUNK == 0:
        valid_len, p_pad, cos_e, sin_e = SCAN_CHUNK, p, cos, sin
    else:
        assert sl < SCAN_CHUNK
        valid_len = sl
        grow = ((0, SCAN_CHUNK - sl), (0, 0))
        p_pad = jnp.pad(p, ((0, 0),) + grow)
        cos_e, sin_e = jnp.pad(cos, grow), jnp.pad(sin, grow)
    y, ssd_new, conv_new, ret_new = _even_core(
        p_pad, cos_e, sin_e, wts["ssd_conv_w"], wts["ssd_conv_b"], wts["ssd_dt_bias"], wts["ssd_a_log"],
        wts["ssd_d"], wts["ssd_norm_g"], wts["ret_norm_g"], ssd_in, conv_in, ret_in, valid_len)
    y = y[:, :sl].reshape(nb * sl, EVEN_OUT)
    x = _proj_res(y, wts["even_w_out"], x, gt1)
    x = _moe_layer(x, wts["norm_ffn_g"][0], sc2, sh2, gt2, *wts["moe"][0], wts["final_norm_g"], False)

    sh1, sc1, gt1, sh2, sc2, gt2 = mod_parts(1)
    if caches is None:
        q, k, v = _qkv_rope(x, wts["norm_mix_g"][1], sc1, sh1, wts["odd_w_qkv"], cos, sin)
        q3, k3, v3 = (a.reshape(nb, sl, ATT_WIDTH) for a in (q, k, v))
        o, k_t, v_t = _attn_prompt(q3, k3, v3)
        keep = min(ATT_WINDOW, sl)
        new_k, new_v = (a.transpose(0, 3, 1, 2)[:, sl - keep:] for a in (k_t, v_t))
    else:
        cos_t, sin_t = jnp.tile(cos, (nb, 1)), jnp.tile(sin, (nb, 1))
        q, k, v = _qkv_rope(x, wts["norm_mix_g"][1], sc1, sh1, wts["odd_w_qkv"], cos_t, sin_t)
        q3, k3, v3 = (a.reshape(nb, sl, ATT_WIDTH) for a in (q, k, v))
        o, new_k, new_v = _attn_sample(q3, k3, v3, *caches)
    x = _proj_res(o.reshape(nb * sl, ATT_WIDTH), wts["odd_w_out"], x, gt1)
    x = _moe_layer(x, wts["norm_ffn_g"][1], sc2, sh2, gt2, *wts["moe"][1], wts["final_norm_g"], True)
    return x, ssd_new[None], conv_new[None], ret_new[None], new_k[None], new_v[None]


def kernel(x_prompt, x_sample, state_ssd, state_conv, state_ret, cache_k, cache_v, c_prompt, c_sample, ada_w, ada_b, norm_mix_g, norm_ffn_g, final_norm_g, even_w_in, even_w_out, ssd_conv_w, ssd_conv_b, ssd_dt_bias, ssd_a_log, ssd_d, ssd_norm_g, ret_norm_g, odd_w_qkv, odd_w_out, moe_wg, moe_bg, moe_we, moe_be, moe_w1, moe_w3, moe_w2):
    depth = ada_w.shape[0]
    assert depth == 2 and even_w_in.shape[0] == 1 and odd_w_qkv.shape[0] == 1
    bp, sp, _ = x_prompt.shape
    bs, ss, _ = x_sample.shape

    wts = {
        "norm_mix_g": norm_mix_g, "norm_ffn_g": norm_ffn_g, "final_norm_g": final_norm_g,
        "even_w_in": _even_w_in_cols(even_w_in[0]), "even_w_out": even_w_out[0].astype(BF16),
        "ssd_conv_w": ssd_conv_w[0], "ssd_conv_b": ssd_conv_b[0], "ssd_dt_bias": ssd_dt_bias[0],
        "ssd_a_log": ssd_a_log[0], "ssd_d": ssd_d[0], "ssd_norm_g": ssd_norm_g[0], "ret_norm_g": ret_norm_g[0],
        "odd_w_qkv": odd_w_qkv[0].astype(BF16), "odd_w_out": odd_w_out[0].astype(BF16),
        "moe": [
            _router_weights(moe_wg[i], moe_bg[i], moe_we[i], moe_be[i])
            + (moe_w1[i].astype(BF16), moe_w3[i].astype(BF16), moe_w2[i].astype(BF16))
            for i in range(depth)
        ],
    }
    mods = _adaln(jnp.concatenate([c_prompt, c_sample], axis=0), ada_w, ada_b)

    zeros_p = (
        jnp.zeros((bp, SSD_HEADS, SSD_HEAD_DIM, SSD_STATE), F32),
        jnp.zeros((bp, SSD_CONV - 1, SSD_CONV_DIM), F32),
        jnp.zeros((bp, RET_HEADS, RET_DV, RET_DK), F32),
    )
    out_p = _run_group(x_prompt, mods[:, :bp], jnp.arange(sp, dtype=I32), zeros_p, None, wts)
    out_s = _run_group(x_sample, mods[:, bp:], PAST_LEN + jnp.arange(ss, dtype=I32),
                       (state_ssd[0], state_conv[0], state_ret[0]), (cache_k[0], cache_v[0]), wts)
    return (out_p[0], out_s[0]) + out_p[1:] + out_s[1:]
```

```python
import functools
import math

import jax
import jax.numpy as jnp
from jax import lax
from jax.experimental import pallas as pl
from jax.experimental.pallas import tpu as pltpu

F32 = jnp.float32
BF16 = jnp.bfloat16
I32 = jnp.int32

D_MODEL = 1024
EPS = 1e-6
N_MOD = 6
SSD_HEADS = 16
SSD_HEAD_DIM = 64
SSD_INNER = SSD_HEADS * SSD_HEAD_DIM
SSD_GROUPS = 4
SSD_STATE = 64
SSD_CONV = 4
SSD_CONV_DIM = SSD_INNER + 2 * SSD_GROUPS * SSD_STATE
RET_HEADS = 8
RET_DK = 64
RET_DV = 128
RET_QK = RET_HEADS * RET_DK
RET_V = RET_HEADS * RET_DV
EVEN_OUT = SSD_INNER + RET_V
ATT_HEADS = 16
ATT_HEAD_DIM = 64
ATT_WIDTH = ATT_HEADS * ATT_HEAD_DIM
DILATED_PATTERNS = ((128, 1), (512, 4), (2048, 16))
ATT_WINDOW = 2048
PAST_LEN = 16384
ROPE_THETA = 10000.0
MOE_GROUPS = 4
MOE_PER_GROUP = 8
MOE_EXPERTS = MOE_GROUPS * MOE_PER_GROUP
MOE_HIDDEN = 256

LANES = 128
SUBLANES = 8
VMEM_LIMIT = 56 * 1024 * 1024

P_Z = 0
P_XBC = P_Z + SSD_INNER
P_Q = P_XBC + SSD_CONV_DIM
P_K = P_Q + RET_QK
P_V = P_K + RET_QK
P_G = P_V + RET_V
P_DT = P_G + RET_V
P_WIDTH = P_DT + LANES

SCAN_CHUNK = 128
ATT_BLOCK = 128
Q_SCALE = ATT_HEAD_DIM ** -0.5 * math.log2(math.e)
MASKED = -1e30
MOE_TILE = 256


def _cparams(sem):
    return pltpu.CompilerParams(dimension_semantics=sem, vmem_limit_bytes=VMEM_LIMIT)


def _row_tiling(nb, sl, target):
    if sl >= target:
        assert sl % target == 0
        return 1, target
    bt = min(nb, target // sl)
    assert nb % bt == 0
    return bt, sl


def _silu(x):
    return x * (0.5 * jnp.tanh(0.5 * x) + 0.5)


def _norm_mod(x, g, sc, sh):
    ms = jnp.mean(x * x, axis=-1, keepdims=True)
    return x * lax.rsqrt(ms + EPS) * g * (1.0 + sc) + sh


def _split3(x):
    hi = x.astype(BF16)
    r1 = x - hi.astype(F32)
    mid = r1.astype(BF16)
    lo = (r1 - mid.astype(F32)).astype(BF16)
    return hi, mid, lo


def _dot_exact_rhs(x, m_bf16):
    hi, mid, lo = _split3(x)
    dot = functools.partial(jnp.dot, preferred_element_type=F32)
    return dot(hi, m_bf16) + dot(mid, m_bf16) + dot(lo, m_bf16)


def _dot_nt(a, b):
    return lax.dot_general(a, b, (((1,), (1,)), ((), ())), preferred_element_type=F32)


def _rope(a, cos, sin_signed):
    w = a.shape[-1]
    lane = lax.broadcasted_iota(I32, a.shape, 1)
    first = (lane & (ATT_HEAD_DIM // 2)) == 0
    rot = jnp.where(first, pltpu.roll(a, w - ATT_HEAD_DIM // 2, 1), pltpu.roll(a, ATT_HEAD_DIM // 2, 1))
    return a * cos + rot * sin_signed


def _adaln_body(c_ref, w_ref, b_ref, o_ref):
    a = _silu(c_ref[...]).astype(BF16)
    o_ref[0] = jnp.dot(a, w_ref[0].astype(BF16), preferred_element_type=F32) + b_ref[0]


def _adaln(c_all, ada_w, ada_b):
    nb = c_all.shape[0]
    depth, _, n6 = ada_w.shape
    tn = 1024
    return pl.pallas_call(
        _adaln_body,
        grid=(depth, n6 // tn),
        in_specs=[
            pl.BlockSpec((nb, D_MODEL), lambda i, j: (0, 0)),
            pl.BlockSpec((1, D_MODEL, tn), lambda i, j: (i, 0, j)),
            pl.BlockSpec((1, 1, tn), lambda i, j: (i, 0, j)),
        ],
        out_specs=pl.BlockSpec((1, nb, tn), lambda i, j: (i, 0, j)),
        out_shape=jax.ShapeDtypeStruct((depth, nb, n6), F32),
        compiler_params=_cparams(("parallel", "parallel")),
        name="adaln",
    )(c_all, ada_w, ada_b.reshape(depth, 1, n6))


def _even_in_body(x_ref, g_ref, sc_ref, sh_ref, w_ref, o_ref, h_scr):
    @pl.when(pl.program_id(1) == 0)
    def _():
        h = _norm_mod(x_ref[...], g_ref[...], sc_ref[...], sh_ref[...])
        h_scr[...] = h.reshape(h_scr.shape).astype(BF16)

    o_ref[...] = jnp.dot(h_scr[...], w_ref[...], preferred_element_type=F32)


def _even_in(x, g, sc, sh, w):
    nb, sl, _ = x.shape
    n = w.shape[1]
    bt, lt = _row_tiling(nb, sl, 512)
    nl = sl // lt
    tm = bt * lt
    tn = 1920
    assert n % tn == 0
    return pl.pallas_call(
        _even_in_body,
        grid=((nb // bt) * nl, n // tn),
        in_specs=[
            pl.BlockSpec((bt, lt, D_MODEL), lambda i, j: (i // nl, i % nl, 0)),
            pl.BlockSpec((1, 1, D_MODEL), lambda i, j: (0, 0, 0)),
            pl.BlockSpec((bt, 1, D_MODEL), lambda i, j: (i // nl, 0, 0)),
            pl.BlockSpec((bt, 1, D_MODEL), lambda i, j: (i // nl, 0, 0)),
            pl.BlockSpec((D_MODEL, tn), lambda i, j: (0, j)),
        ],
        out_specs=pl.BlockSpec((tm, tn), lambda i, j: (i, j)),
        out_shape=jax.ShapeDtypeStruct((nb * sl, n), F32),
        scratch_shapes=[pltpu.VMEM((tm, D_MODEL), BF16)],
        compiler_params=_cparams(("parallel", "arbitrary")),
        name="even_in",
    )(x, g.reshape(1, 1, D_MODEL), sc, sh, w)


def _even_core_body(p_ref, cos_ref, sin_ref, cw_ref, cb_ref, dtb_ref, alog_ref, dskip_ref, ssdg_ref, retg_ref,
                    ssd0_ref, conv0_ref, ret0_ref,
                    y_ref, ssd_out_ref, conv_out_ref, ret_out_ref,
                    xpad_scr, ssd_scr, ret_scr, yacc_scr, *, valid_len):
    cs = SCAN_CHUNK
    c = pl.program_id(1)
    nc = pl.num_programs(1)
    hd = SSD_HEAD_DIM

    @pl.when(c == 0)
    def _():
        ssd_scr[...] = ssd0_ref[0]
        ret_scr[...] = ret0_ref[0]
        xpad_scr[0:SUBLANES, :] = conv0_ref[0]

    row = lax.broadcasted_iota(I32, (cs, LANES), 0)
    col = lax.broadcasted_iota(I32, (cs, cs), 1)
    rowc = lax.broadcasted_iota(I32, (cs, cs), 0)
    causal = col <= rowc
    live = row < valid_len

    xpad_scr[SUBLANES:SUBLANES + cs, :] = p_ref[0, :, P_XBC:P_XBC + SSD_CONV_DIM]
    conv = cb_ref[...] + xpad_scr[SUBLANES:SUBLANES + cs, :] * cw_ref[SSD_CONV - 1:SSD_CONV, :]
    for back in range(1, SSD_CONV):
        tap = SSD_CONV - 1 - back
        conv = conv + xpad_scr[SUBLANES - back:SUBLANES - back + cs, :] * cw_ref[tap:tap + 1, :]
    xc = _silu(conv)

    @pl.when(c == nc - 1)
    def _():
        conv_out_ref[0] = xpad_scr[valid_len:valid_len + SUBLANES, :]

    xpad_scr[0:SUBLANES, :] = xpad_scr[cs:cs + SUBLANES, :]

    xh = xc[:, 0:SSD_INNER]
    if valid_len < cs:
        xh = jnp.where(live[:, 0:1], xh, 0.0)

    dt_in = p_ref[0, :, P_DT:P_DT + LANES].T[0:SSD_HEADS, :] + dtb_ref[...]
    dt = jnp.maximum(dt_in, 0.0) + jnp.log1p(jnp.exp(-jnp.abs(dt_in)))
    la = dt * (-jnp.exp(alog_ref[...]))
    if valid_len < cs:
        la = jnp.where(lax.broadcasted_iota(I32, (SSD_HEADS, cs), 1) < valid_len, la, 0.0)
    acum = _dot_exact_rhs(la, (rowc <= col).astype(BF16))
    acum_last = acum[:, cs - 1:cs]
    w_state = dt * jnp.exp(acum_last - acum)
    cdecay = jnp.exp(acum_last)
    acum_c = jnp.concatenate([acum, jnp.zeros((LANES - SSD_HEADS, cs), F32)], axis=0).T
    eacum_c = jnp.exp(acum_c)

    xh_b = xh.astype(BF16)
    xh_t = xh.T

    for grp in range(SSD_GROUPS):
        b_g = xc[:, SSD_INNER + grp * SSD_STATE:SSD_INNER + (grp + 1) * SSD_STATE].astype(BF16)
        c_g = xc[:, SSD_INNER + (SSD_GROUPS + grp) * SSD_STATE:SSD_INNER + (SSD_GROUPS + grp + 1) * SSD_STATE].astype(BF16)
        s_g = _dot_nt(c_g, b_g)
        for hh in range(SSD_HEADS // SSD_GROUPS):
            h = grp * (SSD_HEADS // SSD_GROUPS) + hh
            lo, hi = h * hd, (h + 1) * hd
            decay = jnp.exp(jnp.where(causal, acum_c[:, h:h + 1] - acum[h:h + 1, :], -jnp.inf))
            m = (s_g * decay * dt[h:h + 1, :]).astype(BF16)
            y_h = jnp.dot(m, xh_b[:, lo:hi], preferred_element_type=F32)
            h_prev = ssd_scr[h]
            y_h = y_h + _dot_nt(c_g, h_prev.astype(BF16)) * eacum_c[:, h:h + 1]
            yacc_scr[:, lo:hi] = y_h
            xw_t = (xh_t[lo:hi, :] * w_state[h:h + 1, :]).astype(BF16)
            ssd_scr[h] = h_prev * cdecay[h:h + 1, :] + jnp.dot(xw_t, b_g, preferred_element_type=F32)

    y = yacc_scr[...] + dskip_ref[...] * xh
    y = y * _silu(p_ref[0, :, P_Z:P_Z + SSD_INNER])
    y = y * lax.rsqrt(jnp.mean(y * y, axis=-1, keepdims=True) + EPS) * ssdg_ref[...]
    y_ref[0, :, 0:SSD_INNER] = y.astype(BF16)

    cos = cos_ref[:, 0:RET_QK]
    sin = sin_ref[:, 0:RET_QK]
    rq = _rope(p_ref[0, :, P_Q:P_Q + RET_QK], cos, sin).astype(BF16)
    rk = (_rope(p_ref[0, :, P_K:P_K + RET_QK], cos, sin) * RET_DK ** -0.5).astype(BF16)
    rv = p_ref[0, :, P_V:P_V + RET_V]
    if valid_len < cs:
        rv = jnp.where(live[:, 0:1], rv, 0.0)
    steps_r = jnp.minimum(rowc + 1, valid_len).astype(F32)
    steps_c = jnp.minimum(col + 1, valid_len).astype(F32)
    steps_l = jnp.minimum(row + 1, valid_len).astype(F32)
    for h in range(RET_HEADS):
        lg = math.log1p(-(2.0 ** (-5.0 - h)))
        q_h = rq[:, h * RET_DK:(h + 1) * RET_DK]
        k_h = rk[:, h * RET_DK:(h + 1) * RET_DK]
        v_h = rv[:, h * RET_DV:(h + 1) * RET_DV]
        decay = jnp.exp(jnp.where(causal, lg * (steps_r - steps_c), -jnp.inf))
        m = (_dot_nt(q_h, k_h) * decay).astype(BF16)
        o_h = jnp.dot(m, v_h.astype(BF16), preferred_element_type=F32)
        s_prev = ret_scr[h]
        o_h = o_h + _dot_nt(q_h, s_prev.astype(BF16)) * jnp.exp(lg * steps_l)
        v_te_t = (v_h * jnp.exp(lg * (float(valid_len) - steps_l))).T.astype(BF16)
        ret_scr[h] = s_prev * math.exp(lg * valid_len) + jnp.dot(v_te_t, k_h, preferred_element_type=F32)
        o_h = o_h * lax.rsqrt(jnp.mean(o_h * o_h, axis=-1, keepdims=True) + EPS) * retg_ref[:, h * RET_DV:(h + 1) * RET_DV]
        o_h = o_h * _silu(p_ref[0, :, P_G + h * RET_DV:P_G + (h + 1) * RET_DV])
        y_ref[0, :, SSD_INNER + h * RET_DV:SSD_INNER + (h + 1) * RET_DV] = o_h.astype(BF16)

    @pl.when(c == nc - 1)
    def _():
        ssd_out_ref[0] = ssd_scr[...]
        ret_out_ref[0] = ret_scr[...]


def _even_core(p, cos, sin, conv_w, conv_b, dt_bias, a_log, d_skip, ssd_norm_g, ret_norm_g, ssd0, conv0, ret0,
               valid_len):
    nb, sl, _ = p.shape
    cs = SCAN_CHUNK
    nc = sl // cs
    assert sl % cs == 0 and (valid_len == cs or nc == 1) and valid_len % SUBLANES == 0
    dtb = jnp.broadcast_to(dt_bias[:, None], (SSD_HEADS, cs))
    alog = jnp.broadcast_to(a_log[:, None], (SSD_HEADS, cs))
    dskip_x = jnp.repeat(d_skip, SSD_HEAD_DIM).reshape(1, SSD_INNER)
    conv0p = jnp.pad(conv0, ((0, 0), (SUBLANES - (SSD_CONV - 1), 0), (0, 0)))
    const2 = lambda b, c: (0, 0)
    st4 = lambda b, c: (b, 0, 0, 0)
    y, ssd_new, conv_new, ret_new = pl.pallas_call(
        functools.partial(_even_core_body, valid_len=valid_len),
        grid=(nb, nc),
        in_specs=[
            pl.BlockSpec((1, cs, P_WIDTH), lambda b, c: (b, c, 0)),
            pl.BlockSpec((cs, ATT_WIDTH), lambda b, c: (c, 0)),
            pl.BlockSpec((cs, ATT_WIDTH), lambda b, c: (c, 0)),
            pl.BlockSpec((SSD_CONV, SSD_CONV_DIM), const2),
            pl.BlockSpec((1, SSD_CONV_DIM), const2),
            pl.BlockSpec((SSD_HEADS, cs), const2),
            pl.BlockSpec((SSD_HEADS, cs), const2),
            pl.BlockSpec((1, SSD_INNER), const2),
            pl.BlockSpec((1, SSD_INNER), const2),
            pl.BlockSpec((1, RET_V), const2),
            pl.BlockSpec((1, SSD_HEADS, SSD_HEAD_DIM, SSD_STATE), st4),
            pl.BlockSpec((1, SUBLANES, SSD_CONV_DIM), lambda b, c: (b, 0, 0)),
            pl.BlockSpec((1, RET_HEADS, RET_DV, RET_DK), st4),
        ],
        out_specs=[
            pl.BlockSpec((1, cs, EVEN_OUT), lambda b, c: (b, c, 0)),
            pl.BlockSpec((1, SSD_HEADS, SSD_HEAD_DIM, SSD_STATE), st4),
            pl.BlockSpec((1, SUBLANES, SSD_CONV_DIM), lambda b, c: (b, 0, 0)),
            pl.BlockSpec((1, RET_HEADS, RET_DV, RET_DK), st4),
        ],
        out_shape=[
            jax.ShapeDtypeStruct((nb, sl, EVEN_OUT), BF16),
            jax.ShapeDtypeStruct((nb, SSD_HEADS, SSD_HEAD_DIM, SSD_STATE), F32),
            jax.ShapeDtypeStruct((nb, SUBLANES, SSD_CONV_DIM), F32),
            jax.ShapeDtypeStruct((nb, RET_HEADS, RET_DV, RET_DK), F32),
        ],
        scratch_shapes=[
            pltpu.VMEM((cs + SUBLANES, SSD_CONV_DIM), F32),
            pltpu.VMEM((SSD_HEADS, SSD_HEAD_DIM, SSD_STATE), F32),
            pltpu.VMEM((RET_HEADS, RET_DV, RET_DK), F32),
            pltpu.VMEM((cs, SSD_INNER), F32),
        ],
        compiler_params=_cparams(("parallel", "arbitrary")),
        name="even_core",
    )(p, cos, sin, conv_w, conv_b.reshape(1, SSD_CONV_DIM), dtb, alog, dskip_x,
      ssd_norm_g.reshape(1, SSD_INNER), ret_norm_g.reshape(1, RET_V), ssd0, conv0p, ret0)
    return y, ssd_new, conv_new[:, SUBLANES - (SSD_CONV - 1):], ret_new


def _proj_res_body(y_ref, w_ref, x_ref, gt_ref, o_ref):
    m = jnp.dot(y_ref[...], w_ref[...], preferred_element_type=F32)
    o_ref[...] = x_ref[...] + gt_ref[...] * m.reshape(x_ref.shape)


def _proj_res(y, w, x, gt):
    nb, sl, _ = x.shape
    kdim = y.shape[1]
    bt, lt = _row_tiling(nb, sl, 512)
    nl = sl // lt
    xmap = lambda i: (i // nl, i % nl, 0)
    return pl.pallas_call(
        _proj_res_body,
        grid=((nb // bt) * nl,),
        in_specs=[
            pl.BlockSpec((bt * lt, kdim), lambda i: (i, 0)),
            pl.BlockSpec((kdim, D_MODEL), lambda i: (0, 0)),
            pl.BlockSpec((bt, lt, D_MODEL), xmap),
            pl.BlockSpec((bt, 1, D_MODEL), lambda i: (i // nl, 0, 0)),
        ],
        out_specs=pl.BlockSpec((bt, lt, D_MODEL), xmap),
        out_shape=jax.ShapeDtypeStruct(x.shape, F32),
        compiler_params=_cparams(("parallel",)),
        name="proj_res",
    )(y, w, x, gt)


def _qkv_body(x_ref, g_ref, sc_ref, sh_ref, w_ref, cos_ref, sin_ref, q_ref, k_ref, v_ref):
    h = _norm_mod(x_ref[...], g_ref[...], sc_ref[...], sh_ref[...])
    h = h.reshape(q_ref.shape).astype(BF16)
    cos = cos_ref[...]
    sin = sin_ref[...]
    dot = functools.partial(jnp.dot, preferred_element_type=F32)
    q_ref[...] = _rope(dot(h, w_ref[:, 0:ATT_WIDTH]), cos, sin) * Q_SCALE
    k_ref[...] = _rope(dot(h, w_ref[:, ATT_WIDTH:2 * ATT_WIDTH]), cos, sin)
    v_ref[...] = dot(h, w_ref[:, 2 * ATT_WIDTH:3 * ATT_WIDTH])


def _qkv_rope(x, g, sc, sh, w, cos, sin):
    nb, sl, _ = x.shape
    bt, lt = _row_tiling(nb, sl, 256)
    nl = sl // lt
    tm = bt * lt
    assert cos.shape[0] in (sl, nb * sl) and (cos.shape[0] == sl) == (bt == 1)
    ntab = cos.shape[0] // tm
    out = jax.ShapeDtypeStruct((nb * sl, ATT_WIDTH), F32)
    ospec = pl.BlockSpec((tm, ATT_WIDTH), lambda i: (i, 0))
    tspec = pl.BlockSpec((tm, ATT_WIDTH), lambda i: (i % ntab, 0))
    return pl.pallas_call(
        _qkv_body,
        grid=((nb // bt) * nl,),
        in_specs=[
            pl.BlockSpec((bt, lt, D_MODEL), lambda i: (i // nl, i % nl, 0)),
            pl.BlockSpec((1, 1, D_MODEL), lambda i: (0, 0, 0)),
            pl.BlockSpec((bt, 1, D_MODEL), lambda i: (i // nl, 0, 0)),
            pl.BlockSpec((bt, 1, D_MODEL), lambda i: (i // nl, 0, 0)),
            pl.BlockSpec((D_MODEL, 3 * ATT_WIDTH), lambda i: (0, 0)),
            tspec, tspec,
        ],
        out_specs=[ospec, ospec, ospec],
        out_shape=[out, out, out],
        compiler_params=_cparams(("parallel",)),
        name="qkv_rope",
    )(x, g.reshape(1, 1, D_MODEL), sc, sh, w, cos, sin)


def _attn_prompt_body(q_ref, k_ref, v_ref, o_ref, kt_ref, vt_ref, ob_scr, lse_scr, p_scr, m_scr, *, seq):
    blk = ATT_BLOCK
    half = ATT_HEAD_DIM
    lane = lax.broadcasted_iota(I32, (blk, LANES), 1)
    head_a = lane < half
    rowi = lax.broadcasted_iota(I32, (2 * blk, blk), 0) & (blk - 1)
    coli = lax.broadcasted_iota(I32, (2 * blk, blk), 1)
    mask_cur = coli <= rowi
    mask_prev = coli >= rowi
    ones = jnp.ones((blk, LANES), BF16)

    for i in range(seq // blk):
        rows = slice(i * blk, (i + 1) * blk)
        kt_ref[0, :, :, rows] = k_ref[0, rows, :].T.reshape(2, half, blk)
        vt_ref[0, :, :, rows] = v_ref[0, rows, :].T.reshape(2, half, blk)

    def ld(ref, dil, s0):
        if dil == 1:
            return ref[0, pl.ds(s0, blk), :]
        return ref[0, pl.ds(s0, blk, stride=dil), :]

    def probs(dil, start, pstart, slot):
        q = ld(q_ref, dil, start)
        q2 = jnp.concatenate([jnp.where(head_a, q, 0.0), jnp.where(head_a, 0.0, q)], axis=0).astype(BF16)
        sc = jnp.where(mask_cur, _dot_nt(q2, ld(k_ref, dil, start).astype(BF16)), MASKED)
        if pstart is None:
            m = jnp.max(sc, axis=1, keepdims=True)
        else:
            sp = jnp.where(mask_prev, _dot_nt(q2, ld(k_ref, dil, pstart).astype(BF16)), MASKED)
            m = jnp.max(jnp.maximum(sc, sp), axis=1, keepdims=True)
            p_scr[slot, :, blk:2 * blk] = jnp.exp2(sp - m).astype(BF16)
        p_scr[slot, :, 0:blk] = jnp.exp2(sc - m).astype(BF16)
        m_scr[slot] = jnp.where(head_a, m[0:blk], m[blk:2 * blk])

    def weigh(gi, dil, start, pstart, slot):
        ve = jnp.concatenate([ld(v_ref, dil, start).astype(BF16), ones], axis=1)
        if pstart is None:
            oe = jnp.dot(p_scr[slot, :, 0:blk], ve, preferred_element_type=F32)
        else:
            vpe = jnp.concatenate([ld(v_ref, dil, pstart).astype(BF16), ones], axis=1)
            oe = jnp.dot(p_scr[slot], jnp.concatenate([ve, vpe], axis=0), preferred_element_type=F32)
        num = jnp.where(head_a, oe[0:blk, 0:LANES], oe[blk:2 * blk, 0:LANES])
        den = jnp.where(head_a, oe[0:blk, LANES:2 * LANES], oe[blk:2 * blk, LANES:2 * LANES])
        rows = pl.ds(start, blk) if dil == 1 else pl.ds(start, blk, stride=dil)
        ob_scr[gi, rows, :] = num / den
        lse_scr[gi, rows, :] = m_scr[slot] + jnp.log2(den)

    def pipelined(gi, dil, count, coords):
        s1 = lambda t, slot: probs(dil, *coords(t), slot)
        s2 = lambda t, slot: weigh(gi, dil, *coords(t), slot)
        s1(0, 0)
        if count == 1:
            s2(0, 0)
            return
        s1(1, 1)
        trips = (count - 2) // 2

        def body(i, carry):
            a = 2 * i
            s2(a, 0)
            s2(a + 1, 1)
            s1(a + 2, 0)
            s1(a + 3, 1)
            return carry

        if trips > 0:
            lax.fori_loop(0, trips, body, 0)
        s2(2 * trips, 0)
        if (count - 2) % 2 == 1:
            s1(count - 1, 0)
        s2(2 * trips + 1, 1)
        if (count - 2) % 2 == 1:
            s2(count - 1, 0)

    for gi, (window, dil) in enumerate(DILATED_PATTERNS):
        assert window // dil == blk and (seq // dil) % blk == 0
        per_residue = seq // dil // blk
        span = blk * dil
        pipelined(gi, dil, dil, lambda t: (t, None))
        if per_residue > 1:
            def coords(t, dil=dil, span=span):
                start = t % dil + (1 + t // dil) * span
                return start, start - span
            pipelined(gi, dil, dil * (per_residue - 1), coords)

    def merge(i, carry):
        rows = pl.ds(pl.multiple_of(i * blk, blk), blk)
        l0 = lse_scr[0, rows, :]
        l1 = lse_scr[1, rows, :]
        l2 = lse_scr[2, rows, :]
        m = jnp.maximum(jnp.maximum(l0, l1), l2)
        w0 = jnp.exp2(l0 - m)
        w1 = jnp.exp2(l1 - m)
        w2 = jnp.exp2(l2 - m)
        o = (w0 * ob_scr[0, rows, :] + w1 * ob_scr[1, rows, :] + w2 * ob_scr[2, rows, :]) / (w0 + w1 + w2)
        o_ref[0, rows, :] = o.astype(BF16)
        return carry

    lax.fori_loop(0, seq // blk, merge, 0)


def _attn_prompt(q, k, v):
    nb, seq, _ = q.shape
    npair = ATT_WIDTH // LANES
    spec = pl.BlockSpec((1, seq, LANES), lambda b, h: (b, 0, h))
    tspec = pl.BlockSpec((1, 2, ATT_HEAD_DIM, seq), lambda b, h: (b, h, 0, 0))
    tshape = jax.ShapeDtypeStruct((nb, ATT_HEADS, ATT_HEAD_DIM, seq), F32)
    return pl.pallas_call(
        functools.partial(_attn_prompt_body, seq=seq),
        grid=(nb, npair),
        in_specs=[spec, spec, spec],
        out_specs=[spec, tspec, tspec],
        out_shape=[jax.ShapeDtypeStruct((nb, seq, ATT_WIDTH), BF16), tshape, tshape],
        scratch_shapes=[
            pltpu.VMEM((len(DILATED_PATTERNS), seq, LANES), F32),
            pltpu.VMEM((len(DILATED_PATTERNS), seq, LANES), F32),
            pltpu.VMEM((2, 2 * ATT_BLOCK, 2 * ATT_BLOCK), BF16),
            pltpu.VMEM((2, ATT_BLOCK, LANES), F32),
        ],
        compiler_params=_cparams(("parallel", "parallel")),
        name="attn_prompt",
    )(q, k, v)


SAMPLE_HEADS_PER_STEP = 4
SAMPLE_ROWS = 2 * SUBLANES


def _attn_sample_body(q_ref, kn_ref, vn_ref, ck_ref, cv_ref, o_ref, nk_ref, nv_ref, *, past, new):
    rows = SAMPLE_ROWS
    keys = past + LANES
    lane = lax.broadcasted_iota(I32, (ATT_HEAD_DIM, LANES), 1)
    is_new = lane >= LANES - new
    t_idx = lax.broadcasted_iota(I32, (rows, keys), 0) & (new - 1)
    k_idx = lax.broadcasted_iota(I32, (rows, keys), 1)
    dist = past + t_idx - k_idx
    oks = [(dist >= 0) & (dist <= window) & ((dist & (dil - 1)) == 0) for window, dil in DILATED_PATTERNS]
    zpad = jnp.zeros((ATT_HEAD_DIM, LANES - new), F32)

    for h in range(SAMPLE_HEADS_PER_STEP):
        alls = []
        for c_ref, n_ref, out_ref in ((ck_ref, kn_ref, nk_ref), (cv_ref, vn_ref, nv_ref)):
            old = c_ref[0, h]
            fresh = n_ref[0, h]
            rolled = pltpu.roll(old, past - new, 1)
            out_ref[0, h] = rolled
            out_ref[0, h, :, past - LANES:past] = jnp.where(
                is_new, jnp.concatenate([zpad, fresh], axis=1), rolled[:, past - LANES:past])
            alls.append(jnp.concatenate([old, fresh, zpad], axis=1).astype(BF16))
        k_all, v_all = alls
        q = jnp.concatenate([q_ref[0, h], jnp.zeros((rows - new, ATT_HEAD_DIM), F32)], axis=0).astype(BF16)
        s = jnp.dot(q, k_all, preferred_element_type=F32)
        ps, dens, lses = [], [], []
        for ok in oks:
            sg = jnp.where(ok, s, MASKED)
            m = jnp.max(sg, axis=1, keepdims=True)
            p = jnp.exp2(sg - m)
            den = jnp.sum(p, axis=1, keepdims=True)
            ps.append(p.astype(BF16))
            dens.append(den)
            lses.append(m + jnp.log2(den))
        o_all = _dot_nt(jnp.concatenate(ps, axis=0), v_all)
        m = jnp.maximum(jnp.maximum(lses[0], lses[1]), lses[2])
        ws = [jnp.exp2(l - m) for l in lses]
        o = sum(w * o_all[i * rows:(i + 1) * rows] / d for i, (w, d) in enumerate(zip(ws, dens)))
        o = o / (ws[0] + ws[1] + ws[2])
        o_ref[0, h] = o[0:new].astype(BF16)


def _attn_sample(q, kn, vn, cache_k, cache_v):
    nb, new, _ = q.shape
    past = cache_k.shape[1]
    hps = SAMPLE_HEADS_PER_STEP
    assert past >= ATT_WINDOW and past % LANES == 0 and new == SUBLANES and ATT_HEADS % hps == 0
    heads = (nb, new, ATT_HEADS, ATT_HEAD_DIM)
    q4 = q.reshape(heads).transpose(0, 2, 1, 3)
    kn4 = kn.reshape(heads).transpose(0, 2, 3, 1)
    vn4 = vn.reshape(heads).transpose(0, 2, 3, 1)
    ck = cache_k.transpose(0, 2, 3, 1)
    cv = cache_v.transpose(0, 2, 3, 1)
    hmap = lambda b, g: (b, g, 0, 0)
    qspec = pl.BlockSpec((1, hps, new, ATT_HEAD_DIM), hmap)
    nspec = pl.BlockSpec((1, hps, ATT_HEAD_DIM, new), hmap)
    cspec = pl.BlockSpec((1, hps, ATT_HEAD_DIM, past), hmap)
    o4, nk, nv = pl.pallas_call(
        functools.partial(_attn_sample_body, past=past, new=new),
        grid=(nb, ATT_HEADS // hps),
        in_specs=[qspec, nspec, nspec, cspec, cspec],
        out_specs=[qspec, cspec, cspec],
        out_shape=[
            jax.ShapeDtypeStruct((nb, ATT_HEADS, new, ATT_HEAD_DIM), BF16),
            jax.ShapeDtypeStruct(ck.shape, F32),
            jax.ShapeDtypeStruct(cv.shape, F32),
        ],
        compiler_params=_cparams(("parallel", "parallel")),
        name="attn_sample",
    )(q4, kn4, vn4, ck, cv)
    o = o4.transpose(0, 2, 1, 3).reshape(nb, new, ATT_WIDTH)
    return o, nk.transpose(0, 3, 1, 2), nv.transpose(0, 3, 1, 2)


ROUTE_COLS = LANES
META_COLS = SUBLANES


def _moe_route_body(x_ref, g_ref, sc_ref, sh_ref, whi_ref, wlo_ref, b_ref, h_ref, meta_ref):
    tm = h_ref.shape[0]
    h = _norm_mod(x_ref[...], g_ref[...], sc_ref[...], sh_ref[...]).reshape(tm, D_MODEL)
    h_ref[...] = h
    hi = h.astype(BF16)
    lo = (h - hi.astype(F32)).astype(BF16)
    dot = functools.partial(jnp.dot, preferred_element_type=F32)
    logits = dot(hi, whi_ref[...]) + dot(lo, whi_ref[...]) + dot(hi, wlo_ref[...]) + b_ref[...]
    lane = lax.broadcasted_iota(I32, (tm, ROUTE_COLS), 1).astype(F32)
    big = float(ROUTE_COLS)
    neg = -jnp.inf
    gl = jnp.where(lane < MOE_GROUPS, logits, neg)
    gmax = jnp.max(gl, axis=1, keepdims=True)
    g_idx = jnp.min(jnp.where(gl == gmax, lane, big), axis=1, keepdims=True)
    g_w = 1.0 / jnp.sum(jnp.exp(gl - gmax), axis=1, keepdims=True)
    first = MOE_GROUPS + MOE_PER_GROUP * g_idx
    el = jnp.where((lane >= first) & (lane < first + MOE_PER_GROUP), logits, neg)
    v1 = jnp.max(el, axis=1, keepdims=True)
    i1 = jnp.min(jnp.where(el == v1, lane, big), axis=1, keepdims=True)
    el2 = jnp.where(lane == i1, neg, el)
    v2 = jnp.max(el2, axis=1, keepdims=True)
    i2 = jnp.min(jnp.where(el2 == v2, lane, big), axis=1, keepdims=True)
    t = jnp.exp(v2 - v1)
    w1 = g_w / (1.0 + t)
    w2 = g_w * t / (1.0 + t)
    meta = jnp.where(lane == 0, i1 - MOE_GROUPS,
                     jnp.where(lane == 1, i2 - MOE_GROUPS, jnp.where(lane == 2, w1, jnp.where(lane == 3, w2, 0.0))))
    meta_ref[...] = meta[:, 0:META_COLS]


def _moe_route(x, g, sc, sh, w_hi, w_lo, bias):
    nb, sl, _ = x.shape
    bt, lt = _row_tiling(nb, sl, 512)
    nl = sl // lt
    tm = bt * lt
    c2 = lambda i: (0, 0)
    return pl.pallas_call(
        _moe_route_body,
        grid=((nb // bt) * nl,),
        in_specs=[
            pl.BlockSpec((bt, lt, D_MODEL), lambda i: (i // nl, i % nl, 0)),
            pl.BlockSpec((1, 1, D_MODEL), lambda i: (0, 0, 0)),
            pl.BlockSpec((bt, 1, D_MODEL), lambda i: (i // nl, 0, 0)),
            pl.BlockSpec((bt, 1, D_MODEL), lambda i: (i // nl, 0, 0)),
            pl.BlockSpec((D_MODEL, ROUTE_COLS), c2),
            pl.BlockSpec((D_MODEL, ROUTE_COLS), c2),
            pl.BlockSpec((1, ROUTE_COLS), c2),
        ],
        out_specs=[pl.BlockSpec((tm, D_MODEL), lambda i: (i, 0)), pl.BlockSpec((tm, META_COLS), lambda i: (i, 0))],
        out_shape=[jax.ShapeDtypeStruct((nb * sl, D_MODEL), F32), jax.ShapeDtypeStruct((nb * sl, META_COLS), F32)],
        compiler_params=_cparams(("parallel",)),
        name="moe_route",
    )(x, g.reshape(1, 1, D_MODEL), sc, sh, w_hi, w_lo, bias)


FLAG_VALID, FLAG_FIRST, FLAG_LAST = 1, 2, 4


def _moe_experts_body(tok_ref, item_tile_ref, item_exp_ref, item_flag_ref,
                      h_hbm, meta_ref, w1_ref, w3_ref, w2_ref, out_hbm,
                      xbuf, acc, gsem, ssem, *, n_tiles):
    tm = MOE_TILE
    g = pl.program_id(0)
    n_items = pl.num_programs(0)
    tile = item_tile_ref[g]
    e = item_exp_ref[g]
    flag = item_flag_ref[g]
    slot = tile % 2

    def gather_start(t, s):
        for r in range(tm):
            tok = tok_ref[t * tm + r]
            pltpu.make_async_copy(h_hbm.at[pl.ds(tok, 1)], xbuf.at[s, pl.ds(r, 1)], gsem.at[s]).start()

    def gather_wait(s):
        pltpu.make_async_copy(h_hbm.at[pl.ds(0, tm)], xbuf.at[s], gsem.at[s]).wait()

    def scatter_start(t, s):
        for r in range(tm):
            tok = tok_ref[t * tm + r]
            pltpu.make_async_copy(acc.at[s, pl.ds(r, 1)], out_hbm.at[pl.ds(tok, 1)], ssem.at[s]).start()

    def scatter_wait(s):
        pltpu.make_async_copy(acc.at[s], out_hbm.at[pl.ds(0, tm)], ssem.at[s]).wait()

    def per_slot(fn):
        for s in range(2):
            pl.when(slot == s)(functools.partial(fn, s))

    @pl.when(g == 0)
    def _():
        gather_start(0, 0)

    @pl.when((flag & FLAG_FIRST) != 0)
    def _():
        @pl.when(tile >= 2)
        def _():
            scatter_wait(slot)

        gather_wait(slot)

        @pl.when(tile + 1 < n_tiles)
        def _():
            per_slot(lambda s: gather_start(tile + 1, 1 - s))

        acc[slot] = jnp.zeros((tm, D_MODEL), F32)

    @pl.when((flag & FLAG_VALID) != 0)
    def _():
        x = xbuf[slot].astype(BF16)
        a = jnp.dot(x, w1_ref[0], preferred_element_type=F32)
        u = jnp.dot(x, w3_ref[0], preferred_element_type=F32)
        ef = e.astype(F32)
        gate = (jnp.where(meta_ref[:, 0:1] == ef, meta_ref[:, 2:3], 0.0)
                + jnp.where(meta_ref[:, 1:2] == ef, meta_ref[:, 3:4], 0.0))
        hm = (_silu(a) * u * gate).astype(BF16)
        acc[slot] += jnp.dot(hm, w2_ref[0], preferred_element_type=F32)

    @pl.when((flag & FLAG_LAST) != 0)
    def _():
        per_slot(lambda s: scatter_start(tile, s))

    @pl.when(g == n_items - 1)
    def _():
        if n_tiles >= 2:
            scatter_wait((n_tiles - 2) % 2)
        scatter_wait((n_tiles - 1) % 2)


def _moe_plan(meta, n_tiles, n_items):
    tm = MOE_TILE
    ea = meta[:, 0].astype(I32)
    eb = meta[:, 1].astype(I32)
    key = jnp.minimum(ea, eb) * MOE_EXPERTS + jnp.maximum(ea, eb)
    order = jnp.argsort(key).astype(I32)
    meta_s = meta[order]
    ea_s = ea[order].reshape(n_tiles, tm)
    eb_s = eb[order].reshape(n_tiles, tm)
    experts = jnp.arange(MOE_EXPERTS, dtype=I32)
    present = jnp.any((ea_s[:, :, None] == experts) | (eb_s[:, :, None] == experts), axis=1)
    flat = present.reshape(-1)
    count = jnp.sum(flat.astype(I32))
    (idx,) = jnp.nonzero(flat, size=n_items, fill_value=0)
    idx = idx.astype(I32)
    pos = jnp.arange(n_items, dtype=I32)
    valid = pos < count
    last_idx = idx[jnp.maximum(count - 1, 0)]
    idx = jnp.where(valid, idx, last_idx)
    item_tile = idx // MOE_EXPERTS
    item_exp = idx % MOE_EXPERTS
    prev_tile = jnp.concatenate([jnp.full((1,), -1, I32), item_tile[:-1]])
    next_tile = jnp.concatenate([item_tile[1:], jnp.full((1,), -1, I32)])
    first = valid & (item_tile != prev_tile)
    last = valid & ((item_tile != next_tile) | (pos == count - 1))
    flags = valid.astype(I32) * FLAG_VALID + first.astype(I32) * FLAG_FIRST + last.astype(I32) * FLAG_LAST
    return order, meta_s, item_tile, item_exp, flags


def _moe_experts(h, meta, w1, w3, w2):
    tokens = h.shape[0]
    tm = MOE_TILE
    assert tokens % tm == 0
    n_tiles = tokens // tm
    pair_classes = MOE_GROUPS * (MOE_PER_GROUP * (MOE_PER_GROUP - 1) // 2)
    n_items = min(MOE_EXPERTS * n_tiles, 2 * (n_tiles + pair_classes - 1))
    order, meta_s, item_tile, item_exp, flags = _moe_plan(meta, n_tiles, n_items)
    wspec1 = pl.BlockSpec((1, D_MODEL, MOE_HIDDEN), lambda g, tok, it, ie, fl: (ie[g], 0, 0))
    wspec2 = pl.BlockSpec((1, MOE_HIDDEN, D_MODEL), lambda g, tok, it, ie, fl: (ie[g], 0, 0))
    return pl.pallas_call(
        functools.partial(_moe_experts_body, n_tiles=n_tiles),
        grid_spec=pltpu.PrefetchScalarGridSpec(
            num_scalar_prefetch=4,
            grid=(n_items,),
            in_specs=[
                pl.BlockSpec(memory_space=pl.ANY),
                pl.BlockSpec((tm, META_COLS), lambda g, tok, it, ie, fl: (it[g], 0)),
                wspec1, wspec1, wspec2,
            ],
            out_specs=pl.BlockSpec(memory_space=pl.ANY),
            scratch_shapes=[
                pltpu.VMEM((2, tm, D_MODEL), F32),
                pltpu.VMEM((2, tm, D_MODEL), F32),
                pltpu.SemaphoreType.DMA((2,)),
                pltpu.SemaphoreType.DMA((2,)),
            ],
        ),
        out_shape=jax.ShapeDtypeStruct((tokens, D_MODEL), F32),
        compiler_params=_cparams(("arbitrary",)),
        name="moe_experts",
    )(order, item_tile, item_exp, flags, h, meta_s, w1, w3, w2)


def _moe_combine_body(x_ref, m_ref, gt_ref, fg_ref, o_ref, *, final_norm):
    x = x_ref[...] + gt_ref[...] * m_ref[...].reshape(x_ref.shape)
    if final_norm:
        x = x * lax.rsqrt(jnp.mean(x * x, axis=-1, keepdims=True) + EPS) * fg_ref[...]
    o_ref[...] = x


def _moe_combine(x, moe_out, gt, final_g, final_norm):
    nb, sl, _ = x.shape
    bt, lt = _row_tiling(nb, sl, 512)
    nl = sl // lt
    xmap = lambda i: (i // nl, i % nl, 0)
    return pl.pallas_call(
        functools.partial(_moe_combine_body, final_norm=final_norm),
        grid=((nb // bt) * nl,),
        in_specs=[
            pl.BlockSpec((bt, lt, D_MODEL), xmap),
            pl.BlockSpec((bt * lt, D_MODEL), lambda i: (i, 0)),
            pl.BlockSpec((bt, 1, D_MODEL), lambda i: (i // nl, 0, 0)),
            pl.BlockSpec((1, 1, D_MODEL), lambda i: (0, 0, 0)),
        ],
        out_specs=pl.BlockSpec((bt, lt, D_MODEL), xmap),
        out_shape=jax.ShapeDtypeStruct(x.shape, F32),
        compiler_params=_cparams(("parallel",)),
        name="moe_combine",
    )(x, moe_out, gt, final_g.reshape(1, 1, D_MODEL))


def _moe_layer(x, g, sc, sh, gt, wr_hi, wr_lo, br, w1, w3, w2, final_g, final_norm):
    h, meta = _moe_route(x, g, sc, sh, wr_hi, wr_lo, br)
    mix = _moe_experts(h, meta, w1, w3, w2)
    return _moe_combine(x, mix, gt, final_g, final_norm)


def _rope_tables(pos):
    half = ATT_HEAD_DIM // 2
    inv_freq = ROPE_THETA ** (-jnp.arange(half, dtype=F32) / half)
    ang = pos.astype(F32)[:, None] * inv_freq[None, :]
    cos = jnp.cos(ang)
    sin = jnp.sin(ang)
    cos_h = jnp.concatenate([cos, cos], axis=1)
    sin_h = jnp.concatenate([-sin, sin], axis=1)
    return jnp.tile(cos_h, (1, ATT_HEADS)), jnp.tile(sin_h, (1, ATT_HEADS))


def _router_weights(wg, bg, we, be):
    w = jnp.concatenate([wg, we], axis=1)
    w = jnp.pad(w, ((0, 0), (0, ROUTE_COLS - w.shape[1])))
    b = jnp.pad(jnp.concatenate([bg, be]), (0, ROUTE_COLS - MOE_GROUPS - MOE_EXPERTS)).reshape(1, ROUTE_COLS)
    hi = w.astype(BF16)
    lo = (w - hi.astype(F32)).astype(BF16)
    return hi, lo, b


def _even_w_in_cols(w):
    dt0 = SSD_INNER + SSD_CONV_DIM
    dt1 = dt0 + SSD_HEADS
    zeros = jnp.zeros((w.shape[0], LANES - SSD_HEADS), w.dtype)
    return jnp.concatenate([w[:, :dt0], w[:, dt1:], w[:, dt0:dt1], zeros], axis=1).astype(BF16)


def _run_group(x, mods, pos, states, caches, wts):
    nb, sl, _ = x.shape
    ssd_in, conv_in, ret_in = states
    cos, sin = _rope_tables(pos)

    def mod_parts(i):
        m = mods[i].reshape(nb, 1, N_MOD, D_MODEL)
        return [m[:, :, j] for j in range(N_MOD)]

    sh1, sc1, gt1, sh2, sc2, gt2 = mod_parts(0)
    p = _even_in(x, wts["norm_mix_g"][0], sc1, sh1, wts["even_w_in"]).reshape(nb, sl, P_WIDTH)
    if sl % SCAN_CHUNK == 0:
        valid_len, p_pad, cos_e, sin_e = SCAN_CHUNK, p, cos, sin
    else:
        assert sl < SCAN_CHUNK
        valid_len = sl
        grow = ((0, SCAN_CHUNK - sl), (0, 0))
        p_pad = jnp.pad(p, ((0, 0),) + grow)
        cos_e, sin_e = jnp.pad(cos, grow), jnp.pad(sin, grow)
    y, ssd_new, conv_new, ret_new = _even_core(
        p_pad, cos_e, sin_e, wts["ssd_conv_w"], wts["ssd_conv_b"], wts["ssd_dt_bias"], wts["ssd_a_log"],
        wts["ssd_d"], wts["ssd_norm_g"], wts["ret_norm_g"], ssd_in, conv_in, ret_in, valid_len)
    y = y[:, :sl].reshape(nb * sl, EVEN_OUT)
    x = _proj_res(y, wts["even_w_out"], x, gt1)
    x = _moe_layer(x, wts["norm_ffn_g"][0], sc2, sh2, gt2, *wts["moe"][0], wts["final_norm_g"], False)

    sh1, sc1, gt1, sh2, sc2, gt2 = mod_parts(1)
    if caches is None:
        q, k, v = _qkv_rope(x, wts["norm_mix_g"][1], sc1, sh1, wts["odd_w_qkv"], cos, sin)
        q3, k3, v3 = (a.reshape(nb, sl, ATT_WIDTH) for a in (q, k, v))
        o, k_t, v_t = _attn_prompt(q3, k3, v3)
        keep = min(ATT_WINDOW, sl)
        new_k, new_v = (a.transpose(0, 3, 1, 2)[:, sl - keep:] for a in (k_t, v_t))
    else:
        cos_t, sin_t = jnp.tile(cos, (nb, 1)), jnp.tile(sin, (nb, 1))
        q, k, v = _qkv_rope(x, wts["norm_mix_g"][1], sc1, sh1, wts["odd_w_qkv"], cos_t, sin_t)
        q3, k3, v3 = (a.reshape(nb, sl, ATT_WIDTH) for a in (q, k, v))
        o, new_k, new_v = _attn_sample(q3, k3, v3, *caches)
    x = _proj_res(o.reshape(nb * sl, ATT_WIDTH), wts["odd_w_out"], x, gt1)
    x = _moe_layer(x, wts["norm_ffn_g"][1], sc2, sh2, gt2, *wts["moe"][1], wts["final_norm_g"], True)
    return x, ssd_new[None], conv_new[None], ret_new[None], new_k[None], new_v[None]


def kernel(x_prompt, x_sample, state_ssd, state_conv, state_ret, cache_k, cache_v, c_prompt, c_sample, ada_w, ada_b, norm_mix_g, norm_ffn_g, final_norm_g, even_w_in, even_w_out, ssd_conv_w, ssd_conv_b, ssd_dt_bias, ssd_a_log, ssd_d, ssd_norm_g, ret_norm_g, odd_w_qkv, odd_w_out, moe_wg, moe_bg, moe_we, moe_be, moe_w1, moe_w3, moe_w2):
    depth = ada_w.shape[0]
    assert depth == 2 and even_w_in.shape[0] == 1 and odd_w_qkv.shape[0] == 1
    bp, sp, _ = x_prompt.shape
    bs, ss, _ = x_sample.shape

    wts = {
        "norm_mix_g": norm_mix_g, "norm_ffn_g": norm_ffn_g, "final_norm_g": final_norm_g,
        "even_w_in": _even_w_in_cols(even_w_in[0]), "even_w_out": even_w_out[0].astype(BF16),
        "ssd_conv_w": ssd_conv_w[0], "ssd_conv_b": ssd_conv_b[0], "ssd_dt_bias": ssd_dt_bias[0],
        "ssd_a_log": ssd_a_log[0], "ssd_d": ssd_d[0], "ssd_norm_g": ssd_norm_g[0], "ret_norm_g": ret_norm_g[0],
        "odd_w_qkv": odd_w_qkv[0].astype(BF16), "odd_w_out": odd_w_out[0].astype(BF16),
        "moe": [
            _router_weights(moe_wg[i], moe_bg[i], moe_we[i], moe_be[i])
            + (moe_w1[i].astype(BF16), moe_w3[i].astype(BF16), moe_w2[i].astype(BF16))
            for i in range(depth)
        ],
    }
    mods = _adaln(jnp.concatenate([c_prompt, c_sample], axis=0), ada_w, ada_b)

    zeros_p = (
        jnp.zeros((bp, SSD_HEADS, SSD_HEAD_DIM, SSD_STATE), F32),
        jnp.zeros((bp, SSD_CONV - 1, SSD_CONV_DIM), F32),
        jnp.zeros((bp, RET_HEADS, RET_DV, RET_DK), F32),
    )
    out_p = _run_group(x_prompt, mods[:, :bp], jnp.arange(sp, dtype=I32), zeros_p, None, wts)
    out_s = _run_group(x_sample, mods[:, bp:], PAST_LEN + jnp.arange(ss, dtype=I32),
                       (state_ssd[0], state_conv[0], state_ret[0]), (cache_k[0], cache_v[0]), wts)
    return (out_p[0], out_s[0]) + out_p[1:] + out_s[1:]
```

```python
import functools
import math

import jax
import jax.numpy as jnp
from jax import lax
from jax.experimental import pallas as pl
from jax.experimental.pallas import tpu as pltpu

F32 = jnp.float32
BF16 = jnp.bfloat16
I32 = jnp.int32

D_MODEL = 1024
EPS = 1e-6
N_MOD = 6
SSD_HEADS = 16
SSD_HEAD_DIM = 64
SSD_INNER = SSD_HEADS * SSD_HEAD_DIM
SSD_GROUPS = 4
SSD_STATE = 64
SSD_CONV = 4
SSD_CONV_DIM = SSD_INNER + 2 * SSD_GROUPS * SSD_STATE
RET_HEADS = 8
RET_DK = 64
RET_DV = 128
RET_QK = RET_HEADS * RET_DK
RET_V = RET_HEADS * RET_DV
EVEN_OUT = SSD_INNER + RET_V
ATT_HEADS = 16
ATT_HEAD_DIM = 64
ATT_WIDTH = ATT_HEADS * ATT_HEAD_DIM
DILATED_PATTERNS = ((128, 1), (512, 4), (2048, 16))
ATT_WINDOW = 2048
PAST_LEN = 16384
ROPE_THETA = 10000.0
MOE_GROUPS = 4
MOE_PER_GROUP = 8
MOE_EXPERTS = MOE_GROUPS * MOE_PER_GROUP
MOE_HIDDEN = 256

LANES = 128
SUBLANES = 8
VMEM_LIMIT = 56 * 1024 * 1024

P_Z = 0
P_XBC = P_Z + SSD_INNER
P_Q = P_XBC + SSD_CONV_DIM
P_K = P_Q + RET_QK
P_V = P_K + RET_QK
P_G = P_V + RET_V
P_DT = P_G + RET_V
P_WIDTH = P_DT + LANES

SCAN_CHUNK = 128
ATT_BLOCK = 128
Q_SCALE = ATT_HEAD_DIM ** -0.5 * math.log2(math.e)
MASKED = -1e30
MOE_TILE = 256


def _cparams(sem):
    return pltpu.CompilerParams(dimension_semantics=sem, vmem_limit_bytes=VMEM_LIMIT)


def _row_tiling(nb, sl, target):
    if sl >= target:
        assert sl % target == 0
        return 1, target
    bt = min(nb, target // sl)
    assert nb % bt == 0
    return bt, sl


def _silu(x):
    return x * (0.5 * jnp.tanh(0.5 * x) + 0.5)


def _norm_mod(x, g, sc, sh):
    ms = jnp.mean(x * x, axis=-1, keepdims=True)
    return x * lax.rsqrt(ms + EPS) * g * (1.0 + sc) + sh


def _split3(x):
    hi = x.astype(BF16)
    r1 = x - hi.astype(F32)
    mid = r1.astype(BF16)
    lo = (r1 - mid.astype(F32)).astype(BF16)
    return hi, mid, lo


def _dot_exact_rhs(x, m_bf16):
    hi, mid, lo = _split3(x)
    dot = functools.partial(jnp.dot, preferred_element_type=F32)
    return dot(hi, m_bf16) + dot(mid, m_bf16) + dot(lo, m_bf16)


def _dot_nt(a, b):
    return lax.dot_general(a, b, (((1,), (1,)), ((), ())), preferred_element_type=F32)


def _rope(a, cos, sin_signed):
    w = a.shape[-1]
    lane = lax.broadcasted_iota(I32, a.shape, 1)
    first = (lane & (ATT_HEAD_DIM // 2)) == 0
    rot = jnp.where(first, pltpu.roll(a, w - ATT_HEAD_DIM // 2, 1), pltpu.roll(a, ATT_HEAD_DIM // 2, 1))
    return a * cos + rot * sin_signed


def _adaln_body(c_ref, w_ref, b_ref, o_ref):
    a = _silu(c_ref[...]).astype(BF16)
    o_ref[0] = jnp.dot(a, w_ref[0].astype(BF16), preferred_element_type=F32) + b_ref[0]


def _adaln(c_all, ada_w, ada_b):
    nb = c_all.shape[0]
    depth, _, n6 = ada_w.shape
    tn = 1024
    return pl.pallas_call(
        _adaln_body,
        grid=(depth, n6 // tn),
        in_specs=[
            pl.BlockSpec((nb, D_MODEL), lambda i, j: (0, 0)),
            pl.BlockSpec((1, D_MODEL, tn), lambda i, j: (i, 0, j)),
            pl.BlockSpec((1, 1, tn), lambda i, j: (i, 0, j)),
        ],
        out_specs=pl.BlockSpec((1, nb, tn), lambda i, j: (i, 0, j)),
        out_shape=jax.ShapeDtypeStruct((depth, nb, n6), F32),
        compiler_params=_cparams(("parallel", "parallel")),
        name="adaln",
    )(c_all, ada_w, ada_b.reshape(depth, 1, n6))


def _even_in_body(x_ref, g_ref, sc_ref, sh_ref, w_ref, o_ref, h_scr):
    @pl.when(pl.program_id(1) == 0)
    def _():
        h = _norm_mod(x_ref[...], g_ref[...], sc_ref[...], sh_ref[...])
        h_scr[...] = h.reshape(h_scr.shape).astype(BF16)

    o_ref[...] = jnp.dot(h_scr[...], w_ref[...], preferred_element_type=F32)


def _even_in(x, g, sc, sh, w):
    nb, sl, _ = x.shape
    n = w.shape[1]
    bt, lt = _row_tiling(nb, sl, 512)
    nl = sl // lt
    tm = bt * lt
    tn = 1920
    assert n % tn == 0
    return pl.pallas_call(
        _even_in_body,
        grid=((nb // bt) * nl, n // tn),
        in_specs=[
            pl.BlockSpec((bt, lt, D_MODEL), lambda i, j: (i // nl, i % nl, 0)),
            pl.BlockSpec((1, 1, D_MODEL), lambda i, j: (0, 0, 0)),
            pl.BlockSpec((bt, 1, D_MODEL), lambda i, j: (i // nl, 0, 0)),
            pl.BlockSpec((bt, 1, D_MODEL), lambda i, j: (i // nl, 0, 0)),
            pl.BlockSpec((D_MODEL, tn), lambda i, j: (0, j)),
        ],
        out_specs=pl.BlockSpec((tm, tn), lambda i, j: (i, j)),
        out_shape=jax.ShapeDtypeStruct((nb * sl, n), F32),
        scratch_shapes=[pltpu.VMEM((tm, D_MODEL), BF16)],
        compiler_params=_cparams(("parallel", "arbitrary")),
        name="even_in",
    )(x, g.reshape(1, 1, D_MODEL), sc, sh, w)


def _ret_log_gamma(h):
    return math.log1p(-(2.0 ** (-5.0 - h)))


def _ret_log_gamma_exp(h, steps):
    return math.exp(_ret_log_gamma(h) * steps)


def _ret_tables(valid_len):
    cs = SCAN_CHUNK
    lg = jnp.asarray([_ret_log_gamma(h) for h in range(RET_HEADS)], F32)[:, None, None]
    steps = jnp.minimum(jnp.arange(cs) + 1, valid_len).astype(F32)
    gap = steps[:, None] - steps[None, :]
    causal = jnp.arange(cs)[None, :] <= jnp.arange(cs)[:, None]
    decay = jnp.exp(jnp.where(causal, lg * gap, -jnp.inf))
    wide = jnp.broadcast_to(steps[:, None], (cs, LANES))
    return decay, jnp.exp(lg * wide), jnp.exp(lg * (float(valid_len) - wide))


def _even_core_body(p_ref, cos_ref, sin_ref, cw_ref, cb_ref, dtb_ref, alog_ref, dskip_ref, ssdg_ref, retg_ref,
                    rdec_ref, rgrow_ref, rend_ref, ssd0_ref, conv0_ref, ret0_ref,
                    y_ref, ssd_out_ref, conv_out_ref, ret_out_ref,
                    xpad_scr, ssd_scr, ret_scr, yacc_scr, *, valid_len):
    cs = SCAN_CHUNK
    c = pl.program_id(1)
    nc = pl.num_programs(1)
    hd = SSD_HEAD_DIM

    @pl.when(c == 0)
    def _():
        ssd_scr[...] = ssd0_ref[0]
        ret_scr[...] = ret0_ref[0]
        xpad_scr[0:SUBLANES, :] = conv0_ref[0]

    row = lax.broadcasted_iota(I32, (cs, LANES), 0)
    col = lax.broadcasted_iota(I32, (cs, cs), 1)
    rowc = lax.broadcasted_iota(I32, (cs, cs), 0)
    causal = col <= rowc
    live = row < valid_len

    def proj(lo, hi):
        val = p_ref[0, :, lo:hi]
        if valid_len < cs:
            val = jnp.concatenate([val, jnp.zeros((cs - valid_len, hi - lo), F32)], axis=0)
        return val

    def emit(lo, hi, val):
        y_ref[0, :, lo:hi] = val[0:valid_len].astype(BF16)

    xpad_scr[SUBLANES:SUBLANES + cs, :] = proj(P_XBC, P_XBC + SSD_CONV_DIM)
    conv = cb_ref[...] + xpad_scr[SUBLANES:SUBLANES + cs, :] * cw_ref[SSD_CONV - 1:SSD_CONV, :]
    for back in range(1, SSD_CONV):
        tap = SSD_CONV - 1 - back
        conv = conv + xpad_scr[SUBLANES - back:SUBLANES - back + cs, :] * cw_ref[tap:tap + 1, :]
    xc = _silu(conv)

    @pl.when(c == nc - 1)
    def _():
        conv_out_ref[0] = xpad_scr[valid_len:valid_len + SUBLANES, :]

    xpad_scr[0:SUBLANES, :] = xpad_scr[cs:cs + SUBLANES, :]

    xh = xc[:, 0:SSD_INNER]
    if valid_len < cs:
        xh = jnp.where(live[:, 0:1], xh, 0.0)

    dt_in = proj(P_DT, P_DT + LANES).T[0:SSD_HEADS, :] + dtb_ref[...]
    dt = jnp.maximum(dt_in, 0.0) + jnp.log1p(jnp.exp(-jnp.abs(dt_in)))
    la = dt * (-jnp.exp(alog_ref[...]))
    if valid_len < cs:
        la = jnp.where(lax.broadcasted_iota(I32, (SSD_HEADS, cs), 1) < valid_len, la, 0.0)
    acum = _dot_exact_rhs(la, (rowc <= col).astype(BF16))
    acum_last = acum[:, cs - 1:cs]
    w_state = dt * jnp.exp(acum_last - acum)
    cdecay = jnp.exp(acum_last)
    acum_c = jnp.concatenate([acum, jnp.zeros((LANES - SSD_HEADS, cs), F32)], axis=0).T
    eacum_c = jnp.exp(acum_c)

    xh_b = xh.astype(BF16)
    xh_t = xh.T

    for grp in range(SSD_GROUPS):
        b_g = xc[:, SSD_INNER + grp * SSD_STATE:SSD_INNER + (grp + 1) * SSD_STATE].astype(BF16)
        c_g = xc[:, SSD_INNER + (SSD_GROUPS + grp) * SSD_STATE:SSD_INNER + (SSD_GROUPS + grp + 1) * SSD_STATE].astype(BF16)
        s_g = _dot_nt(c_g, b_g)
        for hh in range(SSD_HEADS // SSD_GROUPS):
            h = grp * (SSD_HEADS // SSD_GROUPS) + hh
            lo, hi = h * hd, (h + 1) * hd
            decay = jnp.exp(jnp.where(causal, acum_c[:, h:h + 1] - acum[h:h + 1, :], -jnp.inf))
            m = (s_g * decay * dt[h:h + 1, :]).astype(BF16)
            y_h = jnp.dot(m, xh_b[:, lo:hi], preferred_element_type=F32)
            h_prev = ssd_scr[h]
            y_h = y_h + _dot_nt(c_g, h_prev.astype(BF16)) * eacum_c[:, h:h + 1]
            yacc_scr[:, lo:hi] = y_h
            xw_t = (xh_t[lo:hi, :] * w_state[h:h + 1, :]).astype(BF16)
            ssd_scr[h] = h_prev * cdecay[h:h + 1, :] + jnp.dot(xw_t, b_g, preferred_element_type=F32)

    y = yacc_scr[...] + dskip_ref[...] * xh
    y = y * _silu(proj(P_Z, P_Z + SSD_INNER))
    y = y * lax.rsqrt(jnp.mean(y * y, axis=-1, keepdims=True) + EPS) * ssdg_ref[...]
    emit(0, SSD_INNER, y)

    cos = cos_ref[:, 0:RET_QK]
    sin = sin_ref[:, 0:RET_QK]
    rq = _rope(proj(P_Q, P_Q + RET_QK), cos, sin).astype(BF16)
    rk = (_rope(proj(P_K, P_K + RET_QK), cos, sin) * RET_DK ** -0.5).astype(BF16)
    rv = proj(P_V, P_V + RET_V)
    if valid_len < cs:
        rv = jnp.where(live[:, 0:1], rv, 0.0)
    for h in range(RET_HEADS):
        q_h = rq[:, h * RET_DK:(h + 1) * RET_DK]
        k_h = rk[:, h * RET_DK:(h + 1) * RET_DK]
        v_h = rv[:, h * RET_DV:(h + 1) * RET_DV]
        m = (_dot_nt(q_h, k_h) * rdec_ref[h]).astype(BF16)
        o_h = jnp.dot(m, v_h.astype(BF16), preferred_element_type=F32)
        s_prev = ret_scr[h]
        o_h = o_h + _dot_nt(q_h, s_prev.astype(BF16)) * rgrow_ref[h]
        v_te_t = (v_h * rend_ref[h]).T.astype(BF16)
        ret_scr[h] = s_prev * _ret_log_gamma_exp(h, valid_len) + jnp.dot(v_te_t, k_h, preferred_element_type=F32)
        o_h = o_h * lax.rsqrt(jnp.mean(o_h * o_h, axis=-1, keepdims=True) + EPS) * retg_ref[:, h * RET_DV:(h + 1) * RET_DV]
        o_h = o_h * _silu(proj(P_G + h * RET_DV, P_G + (h + 1) * RET_DV))
        emit(SSD_INNER + h * RET_DV, SSD_INNER + (h + 1) * RET_DV, o_h)

    @pl.when(c == nc - 1)
    def _():
        ssd_out_ref[0] = ssd_scr[...]
        ret_out_ref[0] = ret_scr[...]


def _even_core(p, cos, sin, conv_w, conv_b, dt_bias, a_log, d_skip, ssd_norm_g, ret_norm_g, ssd0, conv0, ret0):
    nb, sl, _ = p.shape
    cs = SCAN_CHUNK
    if sl % cs == 0:
        nc, valid_len = sl // cs, cs
    else:
        assert sl < cs and sl % SUBLANES == 0
        nc, valid_len = 1, sl
        grow = ((0, cs - sl), (0, 0))
        cos, sin = jnp.pad(cos, grow), jnp.pad(sin, grow)
    dtb = jnp.broadcast_to(dt_bias[:, None], (SSD_HEADS, cs))
    alog = jnp.broadcast_to(a_log[:, None], (SSD_HEADS, cs))
    dskip_x = jnp.repeat(d_skip, SSD_HEAD_DIM).reshape(1, SSD_INNER)
    conv0p = jnp.pad(conv0, ((0, 0), (SUBLANES - (SSD_CONV - 1), 0), (0, 0)))
    assert RET_DV == LANES
    const2 = lambda b, c: (0, 0)
    const3 = lambda b, c: (0, 0, 0)
    st4 = lambda b, c: (b, 0, 0, 0)
    y, ssd_new, conv_new, ret_new = pl.pallas_call(
        functools.partial(_even_core_body, valid_len=valid_len),
        grid=(nb, nc),
        in_specs=[
            pl.BlockSpec((1, valid_len, P_WIDTH), lambda b, c: (b, c, 0)),
            pl.BlockSpec((cs, ATT_WIDTH), lambda b, c: (c, 0)),
            pl.BlockSpec((cs, ATT_WIDTH), lambda b, c: (c, 0)),
            pl.BlockSpec((SSD_CONV, SSD_CONV_DIM), const2),
            pl.BlockSpec((1, SSD_CONV_DIM), const2),
            pl.BlockSpec((SSD_HEADS, cs), const2),
            pl.BlockSpec((SSD_HEADS, cs), const2),
            pl.BlockSpec((1, SSD_INNER), const2),
            pl.BlockSpec((1, SSD_INNER), const2),
            pl.BlockSpec((1, RET_V), const2),
            pl.BlockSpec((RET_HEADS, cs, cs), const3),
            pl.BlockSpec((RET_HEADS, cs, LANES), const3),
            pl.BlockSpec((RET_HEADS, cs, LANES), const3),
            pl.BlockSpec((1, SSD_HEADS, SSD_HEAD_DIM, SSD_STATE), st4),
            pl.BlockSpec((1, SUBLANES, SSD_CONV_DIM), lambda b, c: (b, 0, 0)),
            pl.BlockSpec((1, RET_HEADS, RET_DV, RET_DK), st4),
        ],
        out_specs=[
            pl.BlockSpec((1, valid_len, EVEN_OUT), lambda b, c: (b, c, 0)),
            pl.BlockSpec((1, SSD_HEADS, SSD_HEAD_DIM, SSD_STATE), st4),
            pl.BlockSpec((1, SUBLANES, SSD_CONV_DIM), lambda b, c: (b, 0, 0)),
            pl.BlockSpec((1, RET_HEADS, RET_DV, RET_DK), st4),
        ],
        out_shape=[
            jax.ShapeDtypeStruct((nb, sl, EVEN_OUT), BF16),
            jax.ShapeDtypeStruct((nb, SSD_HEADS, SSD_HEAD_DIM, SSD_STATE), F32),
            jax.ShapeDtypeStruct((nb, SUBLANES, SSD_CONV_DIM), F32),
            jax.ShapeDtypeStruct((nb, RET_HEADS, RET_DV, RET_DK), F32),
        ],
        scratch_shapes=[
            pltpu.VMEM((cs + SUBLANES, SSD_CONV_DIM), F32),
            pltpu.VMEM((SSD_HEADS, SSD_HEAD_DIM, SSD_STATE), F32),
            pltpu.VMEM((RET_HEADS, RET_DV, RET_DK), F32),
            pltpu.VMEM((cs, SSD_INNER), F32),
        ],
        compiler_params=_cparams(("parallel", "arbitrary")),
        name="even_core",
    )(p, cos, sin, conv_w, conv_b.reshape(1, SSD_CONV_DIM), dtb, alog, dskip_x,
      ssd_norm_g.reshape(1, SSD_INNER), ret_norm_g.reshape(1, RET_V), *_ret_tables(valid_len), ssd0, conv0p, ret0)
    return y, ssd_new, conv_new[:, SUBLANES - (SSD_CONV - 1):], ret_new


def _qkv_body(x_ref, m_ref, gt_ref, g_ref, sc_ref, sh_ref, w_ref, cos_ref, sin_ref,
              xo_ref, q_ref, k_ref, v_ref, *t_refs):
    x = x_ref[...] + gt_ref[...] * m_ref[...].reshape(x_ref.shape)
    xo_ref[...] = x
    h = _norm_mod(x, g_ref[...], sc_ref[...], sh_ref[...])
    h = h.reshape(q_ref.shape).astype(BF16)
    cos = cos_ref[...]
    sin = sin_ref[...]
    dot = functools.partial(jnp.dot, preferred_element_type=F32)
    q_ref[...] = _rope(dot(h, w_ref[:, 0:ATT_WIDTH]), cos, sin) * Q_SCALE
    k = _rope(dot(h, w_ref[:, ATT_WIDTH:2 * ATT_WIDTH]), cos, sin)
    v = dot(h, w_ref[:, 2 * ATT_WIDTH:3 * ATT_WIDTH])
    k_ref[...] = k
    v_ref[...] = v
    if t_refs:
        kt_ref, vt_ref = t_refs
        kt_ref[0] = k.T.reshape(kt_ref.shape[1:])
        vt_ref[0] = v.T.reshape(vt_ref.shape[1:])


def _qkv_rope(x, mix, gt, g, sc, sh, w, cos, sin):
    nb, sl, _ = x.shape
    bt, lt = _row_tiling(nb, sl, 256)
    nl = sl // lt
    tm = bt * lt
    assert cos.shape[0] in (sl, nb * sl) and (cos.shape[0] == sl) == (bt == 1)
    ntab = cos.shape[0] // tm
    out = jax.ShapeDtypeStruct((nb * sl, ATT_WIDTH), F32)
    ospec = pl.BlockSpec((tm, ATT_WIDTH), lambda i: (i, 0))
    tspec = pl.BlockSpec((tm, ATT_WIDTH), lambda i: (i % ntab, 0))
    xspec = pl.BlockSpec((bt, lt, D_MODEL), lambda i: (i // nl, i % nl, 0))
    mspec = pl.BlockSpec((bt, 1, D_MODEL), lambda i: (i // nl, 0, 0))
    out_specs = [xspec, ospec, ospec, ospec]
    out_shape = [jax.ShapeDtypeStruct(x.shape, F32), out, out, out]
    if bt == 1 and lt % LANES == 0:
        hspec = pl.BlockSpec((1, ATT_HEADS, ATT_HEAD_DIM, lt), lambda i: (i // nl, 0, 0, i % nl))
        hshape = jax.ShapeDtypeStruct((nb, ATT_HEADS, ATT_HEAD_DIM, sl), F32)
        out_specs += [hspec, hspec]
        out_shape += [hshape, hshape]
    return pl.pallas_call(
        _qkv_body,
        grid=((nb // bt) * nl,),
        in_specs=[
            xspec, ospec, mspec,
            pl.BlockSpec((1, 1, D_MODEL), lambda i: (0, 0, 0)),
            mspec, mspec,
            pl.BlockSpec((D_MODEL, 3 * ATT_WIDTH), lambda i: (0, 0)),
            tspec, tspec,
        ],
        out_specs=out_specs,
        out_shape=out_shape,
        compiler_params=_cparams(("parallel",)),
        name="qkv_rope",
    )(x, mix, gt, g.reshape(1, 1, D_MODEL), sc, sh, w, cos, sin)


def _attn_prompt_body(q_ref, k_ref, v_ref, o_ref, ob_scr, lse_scr, p_scr, m_scr, *, seq):
    blk = ATT_BLOCK
    half = ATT_HEAD_DIM
    lane = lax.broadcasted_iota(I32, (blk, LANES), 1)
    head_a = lane < half
    rowi = lax.broadcasted_iota(I32, (2 * blk, blk), 0) & (blk - 1)
    coli = lax.broadcasted_iota(I32, (2 * blk, blk), 1)
    mask_cur = coli <= rowi
    mask_prev = coli >= rowi
    ones = jnp.ones((blk, LANES), BF16)

    def ld(ref, dil, s0):
        if dil == 1:
            return ref[0, pl.ds(s0, blk), :]
        return ref[0, pl.ds(s0, blk, stride=dil), :]

    def probs(dil, start, pstart, slot):
        q = ld(q_ref, dil, start)
        q2 = jnp.concatenate([jnp.where(head_a, q, 0.0), jnp.where(head_a, 0.0, q)], axis=0).astype(BF16)
        sc = jnp.where(mask_cur, _dot_nt(q2, ld(k_ref, dil, start).astype(BF16)), MASKED)
        if pstart is None:
            m = jnp.max(sc, axis=1, keepdims=True)
        else:
            sp = jnp.where(mask_prev, _dot_nt(q2, ld(k_ref, dil, pstart).astype(BF16)), MASKED)
            m = jnp.max(jnp.maximum(sc, sp), axis=1, keepdims=True)
            p_scr[slot, :, blk:2 * blk] = jnp.exp2(sp - m).astype(BF16)
        p_scr[slot, :, 0:blk] = jnp.exp2(sc - m).astype(BF16)
        m_scr[slot] = jnp.where(head_a, m[0:blk], m[blk:2 * blk])

    def weigh(gi, dil, start, pstart, slot):
        ve = jnp.concatenate([ld(v_ref, dil, start).astype(BF16), ones], axis=1)
        if pstart is None:
            oe = jnp.dot(p_scr[slot, :, 0:blk], ve, preferred_element_type=F32)
        else:
            vpe = jnp.concatenate([ld(v_ref, dil, pstart).astype(BF16), ones], axis=1)
            oe = jnp.dot(p_scr[slot], jnp.concatenate([ve, vpe], axis=0), preferred_element_type=F32)
        num = jnp.where(head_a, oe[0:blk, 0:LANES], oe[blk:2 * blk, 0:LANES])
        den = jnp.where(head_a, oe[0:blk, LANES:2 * LANES], oe[blk:2 * blk, LANES:2 * LANES])
        rows = pl.ds(start, blk) if dil == 1 else pl.ds(start, blk, stride=dil)
        ob_scr[gi, rows, :] = num / den
        lse_scr[gi, rows, :] = m_scr[slot] + jnp.log2(den)

    def pipelined(gi, dil, count, coords):
        s1 = lambda t, slot: probs(dil, *coords(t), slot)
        s2 = lambda t, slot: weigh(gi, dil, *coords(t), slot)
        s1(0, 0)
        if count == 1:
            s2(0, 0)
            return
        s1(1, 1)
        trips = (count - 2) // 2

        def body(i, carry):
            a = 2 * i
            s2(a, 0)
            s2(a + 1, 1)
            s1(a + 2, 0)
            s1(a + 3, 1)
            return carry

        if trips > 0:
            lax.fori_loop(0, trips, body, 0)
        s2(2 * trips, 0)
        if (count - 2) % 2 == 1:
            s1(count - 1, 0)
        s2(2 * trips + 1, 1)
        if (count - 2) % 2 == 1:
            s2(count - 1, 0)

    for gi, (window, dil) in enumerate(DILATED_PATTERNS):
        assert window // dil == blk and (seq // dil) % blk == 0
        per_residue = seq // dil // blk
        span = blk * dil
        pipelined(gi, dil, dil, lambda t: (t, None))
        if per_residue > 1:
            def coords(t, dil=dil, span=span):
                start = t % dil + (1 + t // dil) * span
                return start, start - span
            pipelined(gi, dil, dil * (per_residue - 1), coords)

    def merge(i, carry):
        rows = pl.ds(pl.multiple_of(i * blk, blk), blk)
        l0 = lse_scr[0, rows, :]
        l1 = lse_scr[1, rows, :]
        l2 = lse_scr[2, rows, :]
        m = jnp.maximum(jnp.maximum(l0, l1), l2)
        w0 = jnp.exp2(l0 - m)
        w1 = jnp.exp2(l1 - m)
        w2 = jnp.exp2(l2 - m)
        o = (w0 * ob_scr[0, rows, :] + w1 * ob_scr[1, rows, :] + w2 * ob_scr[2, rows, :]) / (w0 + w1 + w2)
        o_ref[0, rows, :] = o.astype(BF16)
        return carry

    lax.fori_loop(0, seq // blk, merge, 0)


def _attn_prompt(q, k, v):
    nb, seq, _ = q.shape
    npair = ATT_WIDTH // LANES
    spec = pl.BlockSpec((1, seq, LANES), lambda b, h: (b, 0, h))
    return pl.pallas_call(
        functools.partial(_attn_prompt_body, seq=seq),
        grid=(nb, npair),
        in_specs=[spec, spec, spec],
        out_specs=spec,
        out_shape=jax.ShapeDtypeStruct((nb, seq, ATT_WIDTH), BF16),
        scratch_shapes=[
            pltpu.VMEM((len(DILATED_PATTERNS), seq, LANES), F32),
            pltpu.VMEM((len(DILATED_PATTERNS), seq, LANES), F32),
            pltpu.VMEM((2, 2 * ATT_BLOCK, 2 * ATT_BLOCK), BF16),
            pltpu.VMEM((2, ATT_BLOCK, LANES), F32),
        ],
        compiler_params=_cparams(("parallel", "parallel")),
        name="attn_prompt",
    )(q, k, v)


SAMPLE_HEADS_PER_STEP = 4
SAMPLE_ROWS = 2 * SUBLANES


def _attn_sample_body(q_ref, kn_ref, vn_ref, ck_ref, cv_ref, o_ref, nk_ref, nv_ref, *, past, new):
    rows = SAMPLE_ROWS
    keys = past + LANES
    lane = lax.broadcasted_iota(I32, (ATT_HEAD_DIM, LANES), 1)
    is_new = lane >= LANES - new
    t_idx = lax.broadcasted_iota(I32, (rows, keys), 0) & (new - 1)
    k_idx = lax.broadcasted_iota(I32, (rows, keys), 1)
    dist = past + t_idx - k_idx
    oks = [(dist >= 0) & (dist <= window) & ((dist & (dil - 1)) == 0) for window, dil in DILATED_PATTERNS]
    zpad = jnp.zeros((ATT_HEAD_DIM, LANES - new), F32)

    for h in range(SAMPLE_HEADS_PER_STEP):
        alls = []
        for c_ref, n_ref, out_ref in ((ck_ref, kn_ref, nk_ref), (cv_ref, vn_ref, nv_ref)):
            old = c_ref[0, h]
            fresh = n_ref[0, h]
            rolled = pltpu.roll(old, past - new, 1)
            out_ref[0, h] = rolled
            out_ref[0, h, :, past - LANES:past] = jnp.where(
                is_new, jnp.concatenate([zpad, fresh], axis=1), rolled[:, past - LANES:past])
            alls.append(jnp.concatenate([old, fresh, zpad], axis=1).astype(BF16))
        k_all, v_all = alls
        q = jnp.concatenate([q_ref[0, h], jnp.zeros((rows - new, ATT_HEAD_DIM), F32)], axis=0).astype(BF16)
        s = jnp.dot(q, k_all, preferred_element_type=F32)
        ps, dens, lses = [], [], []
        for ok in oks:
            sg = jnp.where(ok, s, MASKED)
            m = jnp.max(sg, axis=1, keepdims=True)
            p = jnp.exp2(sg - m)
            den = jnp.sum(p, axis=1, keepdims=True)
            ps.append(p.astype(BF16))
            dens.append(den)
            lses.append(m + jnp.log2(den))
        o_all = _dot_nt(jnp.concatenate(ps, axis=0), v_all)
        m = jnp.maximum(jnp.maximum(lses[0], lses[1]), lses[2])
        ws = [jnp.exp2(l - m) for l in lses]
        o = sum(w * o_all[i * rows:(i + 1) * rows] / d for i, (w, d) in enumerate(zip(ws, dens)))
        o = o / (ws[0] + ws[1] + ws[2])
        o_ref[0, h] = o[0:new].astype(BF16)


def _attn_sample(q, kn, vn, cache_k, cache_v):
    nb, new, _ = q.shape
    past = cache_k.shape[1]
    hps = SAMPLE_HEADS_PER_STEP
    assert past >= ATT_WINDOW and past % LANES == 0 and new == SUBLANES and ATT_HEADS % hps == 0
    heads = (nb, new, ATT_HEADS, ATT_HEAD_DIM)
    q4 = q.reshape(heads).transpose(0, 2, 1, 3)
    kn4 = kn.reshape(heads).transpose(0, 2, 3, 1)
    vn4 = vn.reshape(heads).transpose(0, 2, 3, 1)
    ck = cache_k.transpose(0, 2, 3, 1)
    cv = cache_v.transpose(0, 2, 3, 1)
    hmap = lambda b, g: (b, g, 0, 0)
    qspec = pl.BlockSpec((1, hps, new, ATT_HEAD_DIM), hmap)
    nspec = pl.BlockSpec((1, hps, ATT_HEAD_DIM, new), hmap)
    cspec = pl.BlockSpec((1, hps, ATT_HEAD_DIM, past), hmap)
    o4, nk, nv = pl.pallas_call(
        functools.partial(_attn_sample_body, past=past, new=new),
        grid=(nb, ATT_HEADS // hps),
        in_specs=[qspec, nspec, nspec, cspec, cspec],
        out_specs=[qspec, cspec, cspec],
        out_shape=[
            jax.ShapeDtypeStruct((nb, ATT_HEADS, new, ATT_HEAD_DIM), BF16),
            jax.ShapeDtypeStruct(ck.shape, F32),
            jax.ShapeDtypeStruct(cv.shape, F32),
        ],
        compiler_params=_cparams(("parallel", "parallel")),
        name="attn_sample",
    )(q4, kn4, vn4, ck, cv)
    o = o4.transpose(0, 2, 1, 3).reshape(nb, new, ATT_WIDTH)
    return o, nk.transpose(0, 3, 1, 2), nv.transpose(0, 3, 1, 2)


ROUTE_COLS = LANES
META_COLS = SUBLANES


def _proj_route_body(y_ref, w_ref, x_ref, gt_ref, g_ref, sc_ref, sh_ref, whi_ref, wlo_ref, b_ref,
                     xo_ref, h_ref, meta_ref):
    tm = h_ref.shape[0]
    mix = jnp.dot(y_ref[...], w_ref[...], preferred_element_type=F32)
    x = x_ref[...] + gt_ref[...] * mix.reshape(x_ref.shape)
    xo_ref[...] = x
    h = _norm_mod(x, g_ref[...], sc_ref[...], sh_ref[...]).reshape(tm, D_MODEL)
    h_ref[...] = h
    hi = h.astype(BF16)
    lo = (h - hi.astype(F32)).astype(BF16)
    dot = functools.partial(jnp.dot, preferred_element_type=F32)
    logits = dot(hi, whi_ref[...]) + dot(lo, whi_ref[...]) + dot(hi, wlo_ref[...]) + b_ref[...]
    lane = lax.broadcasted_iota(I32, (tm, ROUTE_COLS), 1).astype(F32)
    big = float(ROUTE_COLS)
    neg = -jnp.inf
    gl = jnp.where(lane < MOE_GROUPS, logits, neg)
    gmax = jnp.max(gl, axis=1, keepdims=True)
    g_idx = jnp.min(jnp.where(gl == gmax, lane, big), axis=1, keepdims=True)
    g_w = 1.0 / jnp.sum(jnp.exp(gl - gmax), axis=1, keepdims=True)
    first = MOE_GROUPS + MOE_PER_GROUP * g_idx
    el = jnp.where((lane >= first) & (lane < first + MOE_PER_GROUP), logits, neg)
    v1 = jnp.max(el, axis=1, keepdims=True)
    i1 = jnp.min(jnp.where(el == v1, lane, big), axis=1, keepdims=True)
    el2 = jnp.where(lane == i1, neg, el)
    v2 = jnp.max(el2, axis=1, keepdims=True)
    i2 = jnp.min(jnp.where(el2 == v2, lane, big), axis=1, keepdims=True)
    t = jnp.exp(v2 - v1)
    w1 = g_w / (1.0 + t)
    w2 = g_w * t / (1.0 + t)
    meta = jnp.where(lane == 0, i1 - MOE_GROUPS,
                     jnp.where(lane == 1, i2 - MOE_GROUPS, jnp.where(lane == 2, w1, jnp.where(lane == 3, w2, 0.0))))
    meta_ref[...] = meta[:, 0:META_COLS]


def _proj_route(y, w, x, gt, g, sc, sh, w_hi, w_lo, bias):
    nb, sl, _ = x.shape
    kdim = y.shape[1]
    bt, lt = _row_tiling(nb, sl, 512)
    nl = sl // lt
    tm = bt * lt
    c2 = lambda i: (0, 0)
    xspec = pl.BlockSpec((bt, lt, D_MODEL), lambda i: (i // nl, i % nl, 0))
    mspec = pl.BlockSpec((bt, 1, D_MODEL), lambda i: (i // nl, 0, 0))
    return pl.pallas_call(
        _proj_route_body,
        grid=((nb // bt) * nl,),
        in_specs=[
            pl.BlockSpec((tm, kdim), lambda i: (i, 0)),
            pl.BlockSpec((kdim, D_MODEL), c2),
            xspec, mspec,
            pl.BlockSpec((1, 1, D_MODEL), lambda i: (0, 0, 0)),
            mspec, mspec,
            pl.BlockSpec((D_MODEL, ROUTE_COLS), c2),
            pl.BlockSpec((D_MODEL, ROUTE_COLS), c2),
            pl.BlockSpec((1, ROUTE_COLS), c2),
        ],
        out_specs=[xspec, pl.BlockSpec((tm, D_MODEL), lambda i: (i, 0)), pl.BlockSpec((tm, META_COLS), lambda i: (i, 0))],
        out_shape=[jax.ShapeDtypeStruct(x.shape, F32), jax.ShapeDtypeStruct((nb * sl, D_MODEL), F32),
                   jax.ShapeDtypeStruct((nb * sl, META_COLS), F32)],
        compiler_params=_cparams(("parallel",)),
        name="proj_route",
    )(y, w, x, gt, g.reshape(1, 1, D_MODEL), sc, sh, w_hi, w_lo, bias)


FLAG_VALID, FLAG_FIRST, FLAG_LAST = 1, 2, 4


def _moe_experts_body(tok_ref, item_tile_ref, item_exp_ref, item_flag_ref,
                      h_hbm, meta_ref, w1_ref, w3_ref, w2_ref, out_hbm,
                      xbuf, acc, gsem, ssem, *, n_tiles):
    tm = MOE_TILE
    g = pl.program_id(0)
    n_items = pl.num_programs(0)
    tile = item_tile_ref[g]
    e = item_exp_ref[g]
    flag = item_flag_ref[g]
    slot = tile % 2

    def gather_start(t, s):
        for r in range(tm):
            tok = tok_ref[t * tm + r]
            pltpu.make_async_copy(h_hbm.at[pl.ds(tok, 1)], xbuf.at[s, pl.ds(r, 1)], gsem.at[s]).start()

    def gather_wait(s):
        pltpu.make_async_copy(h_hbm.at[pl.ds(0, tm)], xbuf.at[s], gsem.at[s]).wait()

    def scatter_start(t, s):
        for r in range(tm):
            tok = tok_ref[t * tm + r]
            pltpu.make_async_copy(acc.at[s, pl.ds(r, 1)], out_hbm.at[pl.ds(tok, 1)], ssem.at[s]).start()

    def scatter_wait(s):
        pltpu.make_async_copy(acc.at[s], out_hbm.at[pl.ds(0, tm)], ssem.at[s]).wait()

    def per_slot(fn):
        for s in range(2):
            pl.when(slot == s)(functools.partial(fn, s))

    @pl.when(g == 0)
    def _():
        gather_start(0, 0)

    @pl.when((flag & FLAG_FIRST) != 0)
    def _():
        @pl.when(tile >= 2)
        def _():
            scatter_wait(slot)

        gather_wait(slot)

        @pl.when(tile + 1 < n_tiles)
        def _():
            per_slot(lambda s: gather_start(tile + 1, 1 - s))

        acc[slot] = jnp.zeros((tm, D_MODEL), F32)

    @pl.when((flag & FLAG_VALID) != 0)
    def _():
        x = xbuf[slot].astype(BF16)
        a = jnp.dot(x, w1_ref[0], preferred_element_type=F32)
        u = jnp.dot(x, w3_ref[0], preferred_element_type=F32)
        ef = e.astype(F32)
        gate = (jnp.where(meta_ref[:, 0:1] == ef, meta_ref[:, 2:3], 0.0)
                + jnp.where(meta_ref[:, 1:2] == ef, meta_ref[:, 3:4], 0.0))
        hm = (_silu(a) * u * gate).astype(BF16)
        acc[slot] += jnp.dot(hm, w2_ref[0], preferred_element_type=F32)

    @pl.when((flag & FLAG_LAST) != 0)
    def _():
        per_slot(lambda s: scatter_start(tile, s))

    @pl.when(g == n_items - 1)
    def _():
        if n_tiles >= 2:
            scatter_wait((n_tiles - 2) % 2)
        scatter_wait((n_tiles - 1) % 2)


def _moe_plan(meta, n_tiles, n_items):
    tm = MOE_TILE
    ea = meta[:, 0].astype(I32)
    eb = meta[:, 1].astype(I32)
    key = jnp.minimum(ea, eb) * MOE_EXPERTS + jnp.maximum(ea, eb)
    order = jnp.argsort(key).astype(I32)
    meta_s = meta[order]
    ea_s = ea[order].reshape(n_tiles, tm)
    eb_s = eb[order].reshape(n_tiles, tm)
    experts = jnp.arange(MOE_EXPERTS, dtype=I32)
    present = jnp.any((ea_s[:, :, None] == experts) | (eb_s[:, :, None] == experts), axis=1)
    flat = present.reshape(-1)
    count = jnp.sum(flat.astype(I32))
    (idx,) = jnp.nonzero(flat, size=n_items, fill_value=0)
    idx = idx.astype(I32)
    pos = jnp.arange(n_items, dtype=I32)
    valid = pos < count
    last_idx = idx[jnp.maximum(count - 1, 0)]
    idx = jnp.where(valid, idx, last_idx)
    item_tile = idx // MOE_EXPERTS
    item_exp = idx % MOE_EXPERTS
    prev_tile = jnp.concatenate([jnp.full((1,), -1, I32), item_tile[:-1]])
    next_tile = jnp.concatenate([item_tile[1:], jnp.full((1,), -1, I32)])
    first = valid & (item_tile != prev_tile)
    last = valid & ((item_tile != next_tile) | (pos == count - 1))
    flags = valid.astype(I32) * FLAG_VALID + first.astype(I32) * FLAG_FIRST + last.astype(I32) * FLAG_LAST
    return order, meta_s, item_tile, item_exp, flags


def _moe_experts(h, meta, w1, w3, w2, layer):
    tokens = h.shape[0]
    tm = MOE_TILE
    assert tokens % tm == 0
    n_tiles = tokens // tm
    pair_classes = MOE_GROUPS * (MOE_PER_GROUP * (MOE_PER_GROUP - 1) // 2)
    n_items = min(MOE_EXPERTS * n_tiles, 2 * (n_tiles + pair_classes - 1))
    order, meta_s, item_tile, item_exp, flags = _moe_plan(meta, n_tiles, n_items)
    wmap = lambda g, tok, it, ie, fl: (layer, ie[g], 0, 0)
    wspec1 = pl.BlockSpec((None, 1, D_MODEL, MOE_HIDDEN), wmap)
    wspec2 = pl.BlockSpec((None, 1, MOE_HIDDEN, D_MODEL), wmap)
    return pl.pallas_call(
        functools.partial(_moe_experts_body, n_tiles=n_tiles),
        grid_spec=pltpu.PrefetchScalarGridSpec(
            num_scalar_prefetch=4,
            grid=(n_items,),
            in_specs=[
                pl.BlockSpec(memory_space=pl.ANY),
                pl.BlockSpec((tm, META_COLS), lambda g, tok, it, ie, fl: (it[g], 0)),
                wspec1, wspec1, wspec2,
            ],
            out_specs=pl.BlockSpec(memory_space=pl.ANY),
            scratch_shapes=[
                pltpu.VMEM((2, tm, D_MODEL), F32),
                pltpu.VMEM((2, tm, D_MODEL), F32),
                pltpu.SemaphoreType.DMA((2,)),
                pltpu.SemaphoreType.DMA((2,)),
            ],
        ),
        out_shape=jax.ShapeDtypeStruct((tokens, D_MODEL), F32),
        compiler_params=_cparams(("arbitrary",)),
        name="moe_experts",
    )(order, item_tile, item_exp, flags, h, meta_s, w1, w3, w2)


def _final_combine_body(x_ref, m_ref, gt_ref, fg_ref, o_ref):
    x = x_ref[...] + gt_ref[...] * m_ref[...].reshape(x_ref.shape)
    o_ref[...] = x * lax.rsqrt(jnp.mean(x * x, axis=-1, keepdims=True) + EPS) * fg_ref[...]


def _final_combine(x, moe_out, gt, final_g):
    nb, sl, _ = x.shape
    bt, lt = _row_tiling(nb, sl, 512)
    nl = sl // lt
    xmap = lambda i: (i // nl, i % nl, 0)
    return pl.pallas_call(
        _final_combine_body,
        grid=((nb // bt) * nl,),
        in_specs=[
            pl.BlockSpec((bt, lt, D_MODEL), xmap),
            pl.BlockSpec((bt * lt, D_MODEL), lambda i: (i, 0)),
            pl.BlockSpec((bt, 1, D_MODEL), lambda i: (i // nl, 0, 0)),
            pl.BlockSpec((1, 1, D_MODEL), lambda i: (0, 0, 0)),
        ],
        out_specs=pl.BlockSpec((bt, lt, D_MODEL), xmap),
        out_shape=jax.ShapeDtypeStruct(x.shape, F32),
        compiler_params=_cparams(("parallel",)),
        name="final_combine",
    )(x, moe_out, gt, final_g.reshape(1, 1, D_MODEL))


def _rope_tables(pos):
    half = ATT_HEAD_DIM // 2
    inv_freq = ROPE_THETA ** (-jnp.arange(half, dtype=F32) / half)
    ang = pos.astype(F32)[:, None] * inv_freq[None, :]
    cos = jnp.cos(ang)
    sin = jnp.sin(ang)
    cos_h = jnp.concatenate([cos, cos], axis=1)
    sin_h = jnp.concatenate([-sin, sin], axis=1)
    return jnp.tile(cos_h, (1, ATT_HEADS)), jnp.tile(sin_h, (1, ATT_HEADS))


def _router_weights(wg, bg, we, be):
    w = jnp.concatenate([wg, we], axis=1)
    w = jnp.pad(w, ((0, 0), (0, ROUTE_COLS - w.shape[1])))
    b = jnp.pad(jnp.concatenate([bg, be]), (0, ROUTE_COLS - MOE_GROUPS - MOE_EXPERTS)).reshape(1, ROUTE_COLS)
    hi = w.astype(BF16)
    lo = (w - hi.astype(F32)).astype(BF16)
    return hi, lo, b


def _even_w_in_cols(w):
    dt0 = SSD_INNER + SSD_CONV_DIM
    dt1 = dt0 + SSD_HEADS
    zeros = jnp.zeros((w.shape[0], LANES - SSD_HEADS), w.dtype)
    return jnp.concatenate([w[:, :dt0], w[:, dt1:], w[:, dt0:dt1], zeros], axis=1).astype(BF16)


def _run_group(x, mods, pos, states, caches, wts):
    nb, sl, _ = x.shape
    ssd_in, conv_in, ret_in = states
    cos, sin = _rope_tables(pos)

    def mod_parts(i):
        m = mods[i].reshape(nb, 1, N_MOD, D_MODEL)
        return [m[:, :, j] for j in range(N_MOD)]

    sh1, sc1, gt1, sh2, sc2, gt2 = mod_parts(0)
    p = _even_in(x, wts["norm_mix_g"][0], sc1, sh1, wts["even_w_in"]).reshape(nb, sl, P_WIDTH)
    y, ssd_new, conv_new, ret_new = _even_core(
        p, cos, sin, wts["ssd_conv_w"], wts["ssd_conv_b"], wts["ssd_dt_bias"], wts["ssd_a_log"],
        wts["ssd_d"], wts["ssd_norm_g"], wts["ret_norm_g"], ssd_in, conv_in, ret_in)
    y = y.reshape(nb * sl, EVEN_OUT)
    x, h, meta = _proj_route(y, wts["even_w_out"], x, gt1, wts["norm_ffn_g"][0], sc2, sh2, *wts["router"][0])
    mix = _moe_experts(h, meta, *wts["experts"], 0)
    gt_moe = gt2

    sh1, sc1, gt1, sh2, sc2, gt2 = mod_parts(1)
    if caches is None:
        x, q, k, v, k_t, v_t = _qkv_rope(x, mix, gt_moe, wts["norm_mix_g"][1], sc1, sh1, wts["odd_w_qkv"], cos, sin)
        q3, k3, v3 = (a.reshape(nb, sl, ATT_WIDTH) for a in (q, k, v))
        o = _attn_prompt(q3, k3, v3)
        keep = min(ATT_WINDOW, sl)
        new_k, new_v = (a.transpose(0, 3, 1, 2)[:, sl - keep:] for a in (k_t, v_t))
    else:
        cos_t, sin_t = jnp.tile(cos, (nb, 1)), jnp.tile(sin, (nb, 1))
        x, q, k, v = _qkv_rope(x, mix, gt_moe, wts["norm_mix_g"][1], sc1, sh1, wts["odd_w_qkv"], cos_t, sin_t)
        q3, k3, v3 = (a.reshape(nb, sl, ATT_WIDTH) for a in (q, k, v))
        o, new_k, new_v = _attn_sample(q3, k3, v3, *caches)
    x, h, meta = _proj_route(o.reshape(nb * sl, ATT_WIDTH), wts["odd_w_out"], x, gt1,
                             wts["norm_ffn_g"][1], sc2, sh2, *wts["router"][1])
    mix = _moe_experts(h, meta, *wts["experts"], 1)
    x = _final_combine(x, mix, gt2, wts["final_norm_g"])
    return x, ssd_new[None], conv_new[None], ret_new[None], new_k[None], new_v[None]


def kernel(x_prompt, x_sample, state_ssd, state_conv, state_ret, cache_k, cache_v, c_prompt, c_sample, ada_w, ada_b, norm_mix_g, norm_ffn_g, final_norm_g, even_w_in, even_w_out, ssd_conv_w, ssd_conv_b, ssd_dt_bias, ssd_a_log, ssd_d, ssd_norm_g, ret_norm_g, odd_w_qkv, odd_w_out, moe_wg, moe_bg, moe_we, moe_be, moe_w1, moe_w3, moe_w2):
    depth = ada_w.shape[0]
    assert depth == 2 and even_w_in.shape[0] == 1 and odd_w_qkv.shape[0] == 1
    bp, sp, _ = x_prompt.shape
    bs, ss, _ = x_sample.shape

    wts = {
        "norm_mix_g": norm_mix_g, "norm_ffn_g": norm_ffn_g, "final_norm_g": final_norm_g,
        "even_w_in": _even_w_in_cols(even_w_in[0]), "even_w_out": even_w_out[0].astype(BF16),
        "ssd_conv_w": ssd_conv_w[0], "ssd_conv_b": ssd_conv_b[0], "ssd_dt_bias": ssd_dt_bias[0],
        "ssd_a_log": ssd_a_log[0], "ssd_d": ssd_d[0], "ssd_norm_g": ssd_norm_g[0], "ret_norm_g": ret_norm_g[0],
        "odd_w_qkv": odd_w_qkv[0].astype(BF16), "odd_w_out": odd_w_out[0].astype(BF16),
        "router": [_router_weights(moe_wg[i], moe_bg[i], moe_we[i], moe_be[i]) for i in range(depth)],
        "experts": (moe_w1.astype(BF16), moe_w3.astype(BF16), moe_w2.astype(BF16)),
    }
    mods = _adaln(jnp.concatenate([c_prompt, c_sample], axis=0), ada_w, ada_b)

    zeros_p = (
        jnp.zeros((bp, SSD_HEADS, SSD_HEAD_DIM, SSD_STATE), F32),
        jnp.zeros((bp, SSD_CONV - 1, SSD_CONV_DIM), F32),
        jnp.zeros((bp, RET_HEADS, RET_DV, RET_DK), F32),
    )
    out_p = _run_group(x_prompt, mods[:, :bp], jnp.arange(sp, dtype=I32), zeros_p, None, wts)
    out_s = _run_group(x_sample, mods[:, bp:], PAST_LEN + jnp.arange(ss, dtype=I32),
                       (state_ssd[0], state_conv[0], state_ret[0]), (cache_k[0], cache_v[0]), wts)
    return (out_p[0], out_s[0]) + out_p[1:] + out_s[1:]
```

```python
import functools
import math

import jax
import jax.numpy as jnp
from jax import lax
from jax.experimental import pallas as pl
from jax.experimental.pallas import tpu as pltpu

F32 = jnp.float32
BF16 = jnp.bfloat16
I32 = jnp.int32

D_MODEL = 1024
EPS = 1e-6
N_MOD = 6
SSD_HEADS = 16
SSD_HEAD_DIM = 64
SSD_INNER = SSD_HEADS * SSD_HEAD_DIM
SSD_GROUPS = 4
SSD_STATE = 64
SSD_CONV = 4
SSD_CONV_DIM = SSD_INNER + 2 * SSD_GROUPS * SSD_STATE
RET_HEADS = 8
RET_DK = 64
RET_DV = 128
RET_QK = RET_HEADS * RET_DK
RET_V = RET_HEADS * RET_DV
EVEN_OUT = SSD_INNER + RET_V
ATT_HEADS = 16
ATT_HEAD_DIM = 64
ATT_WIDTH = ATT_HEADS * ATT_HEAD_DIM
DILATED_PATTERNS = ((128, 1), (512, 4), (2048, 16))
ATT_WINDOW = 2048
PAST_LEN = 16384
ROPE_THETA = 10000.0
MOE_GROUPS = 4
MOE_PER_GROUP = 8
MOE_EXPERTS = MOE_GROUPS * MOE_PER_GROUP
MOE_HIDDEN = 256

LANES = 128
SUBLANES = 8
VMEM_LIMIT = 56 * 1024 * 1024

P_Z = 0
P_XBC = P_Z + SSD_INNER
P_Q = P_XBC + SSD_CONV_DIM
P_K = P_Q + RET_QK
P_V = P_K + RET_QK
P_G = P_V + RET_V
P_DT = P_G + RET_V
P_WIDTH = P_DT + LANES

SCAN_CHUNK = 128
ATT_BLOCK = 128
Q_SCALE = ATT_HEAD_DIM ** -0.5 * math.log2(math.e)
MASKED = -1e30
MOE_TILE = 256
ROW_TILE = 512
QKV_ROWS = 256
EVEN_IN_ROWS = 1024
EVEN_IN_COLS = 1920
ADALN_COLS = 1024


def _cparams(sem):
    return pltpu.CompilerParams(dimension_semantics=sem, vmem_limit_bytes=VMEM_LIMIT)


def _row_tiling(nb, sl, target):
    if sl >= target:
        assert sl % target == 0
        return 1, target
    bt = min(nb, target // sl)
    assert nb % bt == 0
    return bt, sl


def _silu(x):
    return x * (0.5 * jnp.tanh(0.5 * x) + 0.5)


def _norm_mod(x, g, sc, sh):
    ms = jnp.mean(x * x, axis=-1, keepdims=True)
    return x * lax.rsqrt(ms + EPS) * g * (1.0 + sc) + sh


def _split3(x):
    hi = x.astype(BF16)
    r1 = x - hi.astype(F32)
    mid = r1.astype(BF16)
    lo = (r1 - mid.astype(F32)).astype(BF16)
    return hi, mid, lo


def _dot_exact_rhs(x, m_bf16):
    hi, mid, lo = _split3(x)
    dot = functools.partial(jnp.dot, preferred_element_type=F32)
    return dot(hi, m_bf16) + dot(mid, m_bf16) + dot(lo, m_bf16)


def _dot_nt(a, b):
    return lax.dot_general(a, b, (((1,), (1,)), ((), ())), preferred_element_type=F32)


def _rope(a, cos, sin_signed):
    w = a.shape[-1]
    lane = lax.broadcasted_iota(I32, a.shape, 1)
    first = (lane & (ATT_HEAD_DIM // 2)) == 0
    rot = jnp.where(first, pltpu.roll(a, w - ATT_HEAD_DIM // 2, 1), pltpu.roll(a, ATT_HEAD_DIM // 2, 1))
    return a * cos + rot * sin_signed


def _adaln_body(c_ref, w_ref, b_ref, o_ref):
    a = _silu(c_ref[...]).astype(BF16)
    o_ref[0] = jnp.dot(a, w_ref[0].astype(BF16), preferred_element_type=F32) + b_ref[0]


def _adaln(c_all, ada_w, ada_b):
    nb = c_all.shape[0]
    depth, _, n6 = ada_w.shape
    tn = ADALN_COLS
    return pl.pallas_call(
        _adaln_body,
        grid=(depth, n6 // tn),
        in_specs=[
            pl.BlockSpec((nb, D_MODEL), lambda i, j: (0, 0)),
            pl.BlockSpec((1, D_MODEL, tn), lambda i, j: (i, 0, j)),
            pl.BlockSpec((1, 1, tn), lambda i, j: (i, 0, j)),
        ],
        out_specs=pl.BlockSpec((1, nb, tn), lambda i, j: (i, 0, j)),
        out_shape=jax.ShapeDtypeStruct((depth, nb, n6), F32),
        compiler_params=_cparams(("parallel", "parallel")),
        name="adaln",
    )(c_all, ada_w, ada_b.reshape(depth, 1, n6))


def _even_in_body(x_ref, g_ref, sc_ref, sh_ref, w_ref, o_ref, h_scr):
    @pl.when(pl.program_id(1) == 0)
    def _():
        h = _norm_mod(x_ref[...], g_ref[...], sc_ref[...], sh_ref[...])
        h_scr[...] = h.reshape(h_scr.shape).astype(BF16)

    o_ref[...] = jnp.dot(h_scr[...], w_ref[...], preferred_element_type=F32)


def _even_in(x, g, sc, sh, w):
    nb, sl, _ = x.shape
    n = w.shape[1]
    bt, lt = _row_tiling(nb, sl, EVEN_IN_ROWS)
    nl = sl // lt
    tm = bt * lt
    tn = EVEN_IN_COLS
    assert n % tn == 0
    return pl.pallas_call(
        _even_in_body,
        grid=((nb // bt) * nl, n // tn),
        in_specs=[
            pl.BlockSpec((bt, lt, D_MODEL), lambda i, j: (i // nl, i % nl, 0)),
            pl.BlockSpec((1, 1, D_MODEL), lambda i, j: (0, 0, 0)),
            pl.BlockSpec((bt, 1, D_MODEL), lambda i, j: (i // nl, 0, 0)),
            pl.BlockSpec((bt, 1, D_MODEL), lambda i, j: (i // nl, 0, 0)),
            pl.BlockSpec((D_MODEL, tn), lambda i, j: (0, j)),
        ],
        out_specs=pl.BlockSpec((tm, tn), lambda i, j: (i, j)),
        out_shape=jax.ShapeDtypeStruct((nb * sl, n), F32),
        scratch_shapes=[pltpu.VMEM((tm, D_MODEL), BF16)],
        compiler_params=_cparams(("parallel", "arbitrary")),
        name="even_in",
    )(x, g.reshape(1, 1, D_MODEL), sc, sh, w)


def _ret_log_gamma(h):
    return math.log1p(-(2.0 ** (-5.0 - h)))


def _ret_log_gamma_exp(h, steps):
    return math.exp(_ret_log_gamma(h) * steps)


def _ret_tables(valid_len):
    cs = SCAN_CHUNK
    lg = jnp.asarray([_ret_log_gamma(h) for h in range(RET_HEADS)], F32)[:, None, None]
    steps = jnp.minimum(jnp.arange(cs) + 1, valid_len).astype(F32)
    gap = steps[:, None] - steps[None, :]
    causal = jnp.arange(cs)[None, :] <= jnp.arange(cs)[:, None]
    decay = jnp.exp(jnp.where(causal, lg * gap, -jnp.inf))
    wide = jnp.broadcast_to(steps[:, None], (cs, LANES))
    return decay, jnp.exp(lg * wide), jnp.exp(lg * (float(valid_len) - wide))


def _even_core_body(p_ref, cos_ref, sin_ref, cw_ref, cb_ref, dtb_ref, alog_ref, dskip_ref, ssdg_ref, retg_ref,
                    rdec_ref, rgrow_ref, rend_ref, ssd0_ref, conv0_ref, ret0_ref,
                    y_ref, ssd_out_ref, conv_out_ref, ret_out_ref,
                    xpad_scr, ssd_scr, ret_scr, yacc_scr, *, valid_len):
    cs = SCAN_CHUNK
    c = pl.program_id(1)
    nc = pl.num_programs(1)
    hd = SSD_HEAD_DIM

    @pl.when(c == 0)
    def _():
        ssd_scr[...] = ssd0_ref[0]
        ret_scr[...] = ret0_ref[0]
        xpad_scr[0:SUBLANES, :] = conv0_ref[0]

    row = lax.broadcasted_iota(I32, (cs, LANES), 0)
    col = lax.broadcasted_iota(I32, (cs, cs), 1)
    rowc = lax.broadcasted_iota(I32, (cs, cs), 0)
    causal = col <= rowc
    live = row < valid_len

    def proj(lo, hi):
        val = p_ref[0, :, lo:hi]
        if valid_len < cs:
            val = jnp.concatenate([val, jnp.zeros((cs - valid_len, hi - lo), F32)], axis=0)
        return val

    def emit(lo, hi, val):
        y_ref[0, :, lo:hi] = val[0:valid_len].astype(BF16)

    xpad_scr[SUBLANES:SUBLANES + cs, :] = proj(P_XBC, P_XBC + SSD_CONV_DIM)
    conv = cb_ref[...] + xpad_scr[SUBLANES:SUBLANES + cs, :] * cw_ref[SSD_CONV - 1:SSD_CONV, :]
    for back in range(1, SSD_CONV):
        tap = SSD_CONV - 1 - back
        conv = conv + xpad_scr[SUBLANES - back:SUBLANES - back + cs, :] * cw_ref[tap:tap + 1, :]
    xc = _silu(conv)

    @pl.when(c == nc - 1)
    def _():
        conv_out_ref[0] = xpad_scr[valid_len:valid_len + SUBLANES, :]

    xpad_scr[0:SUBLANES, :] = xpad_scr[cs:cs + SUBLANES, :]

    xh = xc[:, 0:SSD_INNER]
    if valid_len < cs:
        xh = jnp.where(live[:, 0:1], xh, 0.0)

    dt_in = proj(P_DT, P_DT + LANES).T[0:SSD_HEADS, :] + dtb_ref[...]
    dt = jnp.maximum(dt_in, 0.0) + jnp.log1p(jnp.exp(-jnp.abs(dt_in)))
    la = dt * (-jnp.exp(alog_ref[...]))
    if valid_len < cs:
        la = jnp.where(lax.broadcasted_iota(I32, (SSD_HEADS, cs), 1) < valid_len, la, 0.0)
    acum = _dot_exact_rhs(la, (rowc <= col).astype(BF16))
    acum_last = acum[:, cs - 1:cs]
    w_state = dt * jnp.exp(acum_last - acum)
    cdecay = jnp.exp(acum_last)
    acum_c = jnp.concatenate([acum, jnp.zeros((LANES - SSD_HEADS, cs), F32)], axis=0).T
    eacum_c = jnp.exp(acum_c)

    xh_b = xh.astype(BF16)
    xh_t = xh.T

    for grp in range(SSD_GROUPS):
        b_g = xc[:, SSD_INNER + grp * SSD_STATE:SSD_INNER + (grp + 1) * SSD_STATE].astype(BF16)
        c_g = xc[:, SSD_INNER + (SSD_GROUPS + grp) * SSD_STATE:SSD_INNER + (SSD_GROUPS + grp + 1) * SSD_STATE].astype(BF16)
        s_g = _dot_nt(c_g, b_g)
        for hh in range(SSD_HEADS // SSD_GROUPS):
            h = grp * (SSD_HEADS // SSD_GROUPS) + hh
            lo, hi = h * hd, (h + 1) * hd
            decay = jnp.exp(jnp.where(causal, acum_c[:, h:h + 1] - acum[h:h + 1, :], -jnp.inf))
            m = (s_g * decay * dt[h:h + 1, :]).astype(BF16)
            y_h = jnp.dot(m, xh_b[:, lo:hi], preferred_element_type=F32)
            h_prev = ssd_scr[h]
            y_h = y_h + _dot_nt(c_g, h_prev.astype(BF16)) * eacum_c[:, h:h + 1]
            yacc_scr[:, lo:hi] = y_h
            xw_t = (xh_t[lo:hi, :] * w_state[h:h + 1, :]).astype(BF16)
            ssd_scr[h] = h_prev * cdecay[h:h + 1, :] + jnp.dot(xw_t, b_g, preferred_element_type=F32)

    y = yacc_scr[...] + dskip_ref[...] * xh
    y = y * _silu(proj(P_Z, P_Z + SSD_INNER))
    y = y * lax.rsqrt(jnp.mean(y * y, axis=-1, keepdims=True) + EPS) * ssdg_ref[...]
    emit(0, SSD_INNER, y)

    cos = cos_ref[:, 0:RET_QK]
    sin = sin_ref[:, 0:RET_QK]
    rq = _rope(proj(P_Q, P_Q + RET_QK), cos, sin).astype(BF16)
    rk = (_rope(proj(P_K, P_K + RET_QK), cos, sin) * RET_DK ** -0.5).astype(BF16)
    rv = proj(P_V, P_V + RET_V)
    if valid_len < cs:
        rv = jnp.where(live[:, 0:1], rv, 0.0)
    for h in range(RET_HEADS):
        q_h = rq[:, h * RET_DK:(h + 1) * RET_DK]
        k_h = rk[:, h * RET_DK:(h + 1) * RET_DK]
        v_h = rv[:, h * RET_DV:(h + 1) * RET_DV]
        m = (_dot_nt(q_h, k_h) * rdec_ref[h]).astype(BF16)
        o_h = jnp.dot(m, v_h.astype(BF16), preferred_element_type=F32)
        s_prev = ret_scr[h]
        o_h = o_h + _dot_nt(q_h, s_prev.astype(BF16)) * rgrow_ref[h]
        v_te_t = (v_h * rend_ref[h]).T.astype(BF16)
        ret_scr[h] = s_prev * _ret_log_gamma_exp(h, valid_len) + jnp.dot(v_te_t, k_h, preferred_element_type=F32)
        o_h = o_h * lax.rsqrt(jnp.mean(o_h * o_h, axis=-1, keepdims=True) + EPS) * retg_ref[:, h * RET_DV:(h + 1) * RET_DV]
        o_h = o_h * _silu(proj(P_G + h * RET_DV, P_G + (h + 1) * RET_DV))
        emit(SSD_INNER + h * RET_DV, SSD_INNER + (h + 1) * RET_DV, o_h)

    @pl.when(c == nc - 1)
    def _():
        ssd_out_ref[0] = ssd_scr[...]
        ret_out_ref[0] = ret_scr[...]


def _even_core(p, cos, sin, conv_w, conv_b, dt_bias, a_log, d_skip, ssd_norm_g, ret_norm_g, ssd0, conv0, ret0):
    nb, sl, _ = p.shape
    cs = SCAN_CHUNK
    if sl % cs == 0:
        nc, valid_len = sl // cs, cs
    else:
        assert sl < cs and sl % SUBLANES == 0
        nc, valid_len = 1, sl
        grow = ((0, cs - sl), (0, 0))
        cos, sin = jnp.pad(cos, grow), jnp.pad(sin, grow)
    dtb = jnp.broadcast_to(dt_bias[:, None], (SSD_HEADS, cs))
    alog = jnp.broadcast_to(a_log[:, None], (SSD_HEADS, cs))
    dskip_x = jnp.repeat(d_skip, SSD_HEAD_DIM).reshape(1, SSD_INNER)
    conv0p = jnp.pad(conv0, ((0, 0), (SUBLANES - (SSD_CONV - 1), 0), (0, 0)))
    assert RET_DV == LANES
    const2 = lambda b, c: (0, 0)
    const3 = lambda b, c: (0, 0, 0)
    st4 = lambda b, c: (b, 0, 0, 0)
    y, ssd_new, conv_new, ret_new = pl.pallas_call(
        functools.partial(_even_core_body, valid_len=valid_len),
        grid=(nb, nc),
        in_specs=[
            pl.BlockSpec((1, valid_len, P_WIDTH), lambda b, c: (b, c, 0)),
            pl.BlockSpec((cs, ATT_WIDTH), lambda b, c: (c, 0)),
            pl.BlockSpec((cs, ATT_WIDTH), lambda b, c: (c, 0)),
            pl.BlockSpec((SSD_CONV, SSD_CONV_DIM), const2),
            pl.BlockSpec((1, SSD_CONV_DIM), const2),
            pl.BlockSpec((SSD_HEADS, cs), const2),
            pl.BlockSpec((SSD_HEADS, cs), const2),
            pl.BlockSpec((1, SSD_INNER), const2),
            pl.BlockSpec((1, SSD_INNER), const2),
            pl.BlockSpec((1, RET_V), const2),
            pl.BlockSpec((RET_HEADS, cs, cs), const3),
            pl.BlockSpec((RET_HEADS, cs, LANES), const3),
            pl.BlockSpec((RET_HEADS, cs, LANES), const3),
            pl.BlockSpec((1, SSD_HEADS, SSD_HEAD_DIM, SSD_STATE), st4),
            pl.BlockSpec((1, SUBLANES, SSD_CONV_DIM), lambda b, c: (b, 0, 0)),
            pl.BlockSpec((1, RET_HEADS, RET_DV, RET_DK), st4),
        ],
        out_specs=[
            pl.BlockSpec((1, valid_len, EVEN_OUT), lambda b, c: (b, c, 0)),
            pl.BlockSpec((1, SSD_HEADS, SSD_HEAD_DIM, SSD_STATE), st4),
            pl.BlockSpec((1, SUBLANES, SSD_CONV_DIM), lambda b, c: (b, 0, 0)),
            pl.BlockSpec((1, RET_HEADS, RET_DV, RET_DK), st4),
        ],
        out_shape=[
            jax.ShapeDtypeStruct((nb, sl, EVEN_OUT), BF16),
            jax.ShapeDtypeStruct((nb, SSD_HEADS, SSD_HEAD_DIM, SSD_STATE), F32),
            jax.ShapeDtypeStruct((nb, SUBLANES, SSD_CONV_DIM), F32),
            jax.ShapeDtypeStruct((nb, RET_HEADS, RET_DV, RET_DK), F32),
        ],
        scratch_shapes=[
            pltpu.VMEM((cs + SUBLANES, SSD_CONV_DIM), F32),
            pltpu.VMEM((SSD_HEADS, SSD_HEAD_DIM, SSD_STATE), F32),
            pltpu.VMEM((RET_HEADS, RET_DV, RET_DK), F32),
            pltpu.VMEM((cs, SSD_INNER), F32),
        ],
        compiler_params=_cparams(("parallel", "arbitrary")),
        name="even_core",
    )(p, cos, sin, conv_w, conv_b.reshape(1, SSD_CONV_DIM), dtb, alog, dskip_x,
      ssd_norm_g.reshape(1, SSD_INNER), ret_norm_g.reshape(1, RET_V), *_ret_tables(valid_len), ssd0, conv0p, ret0)
    return y, ssd_new, conv_new[:, SUBLANES - (SSD_CONV - 1):], ret_new


def _qkv_body(x_ref, m_ref, gt_ref, g_ref, sc_ref, sh_ref, w_ref, cos_ref, sin_ref,
              xo_ref, q_ref, k_ref, v_ref):
    x = x_ref[...] + gt_ref[...] * m_ref[...].reshape(x_ref.shape)
    xo_ref[...] = x
    h = _norm_mod(x, g_ref[...], sc_ref[...], sh_ref[...])
    h = h.reshape(q_ref.shape).astype(BF16)
    cos = cos_ref[...]
    sin = sin_ref[...]
    dot = functools.partial(jnp.dot, preferred_element_type=F32)
    q_ref[...] = _rope(dot(h, w_ref[:, 0:ATT_WIDTH]), cos, sin) * Q_SCALE
    k_ref[...] = _rope(dot(h, w_ref[:, ATT_WIDTH:2 * ATT_WIDTH]), cos, sin)
    v_ref[...] = dot(h, w_ref[:, 2 * ATT_WIDTH:3 * ATT_WIDTH])


def _qkv_rope(x, mix, gt, g, sc, sh, w, cos, sin):
    nb, sl, _ = x.shape
    bt, lt = _row_tiling(nb, sl, QKV_ROWS)
    nl = sl // lt
    tm = bt * lt
    assert cos.shape[0] in (sl, nb * sl) and (cos.shape[0] == sl) == (bt == 1)
    ntab = cos.shape[0] // tm
    out = jax.ShapeDtypeStruct((nb * sl, ATT_WIDTH), F32)
    ospec = pl.BlockSpec((tm, ATT_WIDTH), lambda i: (i, 0))
    tspec = pl.BlockSpec((tm, ATT_WIDTH), lambda i: (i % ntab, 0))
    xspec = pl.BlockSpec((bt, lt, D_MODEL), lambda i: (i // nl, i % nl, 0))
    mspec = pl.BlockSpec((bt, 1, D_MODEL), lambda i: (i // nl, 0, 0))
    out_specs = [xspec, ospec, ospec, ospec]
    out_shape = [jax.ShapeDtypeStruct(x.shape, F32), out, out, out]
    return pl.pallas_call(
        _qkv_body,
        grid=((nb // bt) * nl,),
        in_specs=[
            xspec, ospec, mspec,
            pl.BlockSpec((1, 1, D_MODEL), lambda i: (0, 0, 0)),
            mspec, mspec,
            pl.BlockSpec((D_MODEL, 3 * ATT_WIDTH), lambda i: (0, 0)),
            tspec, tspec,
        ],
        out_specs=out_specs,
        out_shape=out_shape,
        compiler_params=_cparams(("parallel",)),
        name="qkv_rope",
    )(x, mix, gt, g.reshape(1, 1, D_MODEL), sc, sh, w, cos, sin)


def _attn_prompt_body(q_ref, k_ref, v_ref, o_ref, kt_ref, vt_ref, ob_scr, lse_scr, p_scr, m_scr, *, seq):
    blk = ATT_BLOCK
    half = ATT_HEAD_DIM
    lane = lax.broadcasted_iota(I32, (blk, LANES), 1)
    head_a = lane < half
    rowi = lax.broadcasted_iota(I32, (2 * blk, blk), 0) & (blk - 1)
    coli = lax.broadcasted_iota(I32, (2 * blk, blk), 1)
    mask_cur = coli <= rowi
    mask_prev = coli >= rowi
    ones = jnp.ones((blk, LANES), BF16)

    for i in range(seq // blk):
        rows = slice(i * blk, (i + 1) * blk)
        kt_ref[0, :, :, rows] = k_ref[0, rows, :].T.reshape(2, half, blk)
        vt_ref[0, :, :, rows] = v_ref[0, rows, :].T.reshape(2, half, blk)

    def ld(ref, dil, s0):
        if dil == 1:
            return ref[0, pl.ds(s0, blk), :]
        return ref[0, pl.ds(s0, blk, stride=dil), :]

    def probs(dil, start, pstart, slot):
        q = ld(q_ref, dil, start)
        q2 = jnp.concatenate([jnp.where(head_a, q, 0.0), jnp.where(head_a, 0.0, q)], axis=0).astype(BF16)
        sc = jnp.where(mask_cur, _dot_nt(q2, ld(k_ref, dil, start).astype(BF16)), MASKED)
        if pstart is None:
            m = jnp.max(sc, axis=1, keepdims=True)
        else:
            sp = jnp.where(mask_prev, _dot_nt(q2, ld(k_ref, dil, pstart).astype(BF16)), MASKED)
            m = jnp.max(jnp.maximum(sc, sp), axis=1, keepdims=True)
            p_scr[slot, :, blk:2 * blk] = jnp.exp2(sp - m).astype(BF16)
        p_scr[slot, :, 0:blk] = jnp.exp2(sc - m).astype(BF16)
        m_scr[slot] = jnp.where(head_a, m[0:blk], m[blk:2 * blk])

    def weigh(gi, dil, start, pstart, slot):
        ve = jnp.concatenate([ld(v_ref, dil, start).astype(BF16), ones], axis=1)
        if pstart is None:
            oe = jnp.dot(p_scr[slot, :, 0:blk], ve, preferred_element_type=F32)
        else:
            vpe = jnp.concatenate([ld(v_ref, dil, pstart).astype(BF16), ones], axis=1)
            oe = jnp.dot(p_scr[slot], jnp.concatenate([ve, vpe], axis=0), preferred_element_type=F32)
        num = jnp.where(head_a, oe[0:blk, 0:LANES], oe[blk:2 * blk, 0:LANES])
        den = jnp.where(head_a, oe[0:blk, LANES:2 * LANES], oe[blk:2 * blk, LANES:2 * LANES])
        rows = pl.ds(start, blk) if dil == 1 else pl.ds(start, blk, stride=dil)
        ob_scr[gi, rows, :] = num / den
        lse_scr[gi, rows, :] = m_scr[slot] + jnp.log2(den)

    def pipelined(gi, dil, count, coords):
        s1 = lambda t, slot: probs(dil, *coords(t), slot)
        s2 = lambda t, slot: weigh(gi, dil, *coords(t), slot)
        s1(0, 0)
        if count == 1:
            s2(0, 0)
            return
        s1(1, 1)
        trips = (count - 2) // 2

        def body(i, carry):
            a = 2 * i
            s2(a, 0)
            s2(a + 1, 1)
            s1(a + 2, 0)
            s1(a + 3, 1)
            return carry

        if trips > 0:
            lax.fori_loop(0, trips, body, 0)
        s2(2 * trips, 0)
        if (count - 2) % 2 == 1:
            s1(count - 1, 0)
        s2(2 * trips + 1, 1)
        if (count - 2) % 2 == 1:
            s2(count - 1, 0)

    for gi, (window, dil) in enumerate(DILATED_PATTERNS):
        assert window // dil == blk and (seq // dil) % blk == 0
        per_residue = seq // dil // blk
        span = blk * dil
        pipelined(gi, dil, dil, lambda t: (t, None))
        if per_residue > 1:
            def coords(t, dil=dil, span=span):
                start = t % dil + (1 + t // dil) * span
                return start, start - span
            pipelined(gi, dil, dil * (per_residue - 1), coords)

    def merge(i, carry):
        rows = pl.ds(pl.multiple_of(i * blk, blk), blk)
        l0 = lse_scr[0, rows, :]
        l1 = lse_scr[1, rows, :]
        l2 = lse_scr[2, rows, :]
        m = jnp.maximum(jnp.maximum(l0, l1), l2)
        w0 = jnp.exp2(l0 - m)
        w1 = jnp.exp2(l1 - m)
        w2 = jnp.exp2(l2 - m)
        o = (w0 * ob_scr[0, rows, :] + w1 * ob_scr[1, rows, :] + w2 * ob_scr[2, rows, :]) / (w0 + w1 + w2)
        o_ref[0, rows, :] = o.astype(BF16)
        return carry

    lax.fori_loop(0, seq // blk, merge, 0)


def _attn_prompt(q, k, v):
    nb, seq, _ = q.shape
    npair = ATT_WIDTH // LANES
    spec = pl.BlockSpec((1, seq, LANES), lambda b, h: (b, 0, h))
    tspec = pl.BlockSpec((1, 2, ATT_HEAD_DIM, seq), lambda b, h: (b, h, 0, 0))
    tshape = jax.ShapeDtypeStruct((nb, ATT_HEADS, ATT_HEAD_DIM, seq), F32)
    return pl.pallas_call(
        functools.partial(_attn_prompt_body, seq=seq),
        grid=(nb, npair),
        in_specs=[spec, spec, spec],
        out_specs=[spec, tspec, tspec],
        out_shape=[jax.ShapeDtypeStruct((nb, seq, ATT_WIDTH), BF16), tshape, tshape],
        scratch_shapes=[
            pltpu.VMEM((len(DILATED_PATTERNS), seq, LANES), F32),
            pltpu.VMEM((len(DILATED_PATTERNS), seq, LANES), F32),
            pltpu.VMEM((2, 2 * ATT_BLOCK, 2 * ATT_BLOCK), BF16),
            pltpu.VMEM((2, ATT_BLOCK, LANES), F32),
        ],
        compiler_params=_cparams(("parallel", "parallel")),
        name="attn_prompt",
    )(q, k, v)


SAMPLE_HEADS_PER_STEP = 4
SAMPLE_ROWS = 2 * SUBLANES


def _attn_sample_body(q_ref, kn_ref, vn_ref, ck_ref, cv_ref, o_ref, nk_ref, nv_ref, *, past, new):
    rows = SAMPLE_ROWS
    keys = past + LANES
    lane = lax.broadcasted_iota(I32, (ATT_HEAD_DIM, LANES), 1)
    is_new = lane >= LANES - new
    t_idx = lax.broadcasted_iota(I32, (rows, keys), 0) & (new - 1)
    k_idx = lax.broadcasted_iota(I32, (rows, keys), 1)
    dist = past + t_idx - k_idx
    oks = [(dist >= 0) & (dist <= window) & ((dist & (dil - 1)) == 0) for window, dil in DILATED_PATTERNS]
    zpad = jnp.zeros((ATT_HEAD_DIM, LANES - new), F32)

    for h in range(SAMPLE_HEADS_PER_STEP):
        alls = []
        for c_ref, n_ref, out_ref in ((ck_ref, kn_ref, nk_ref), (cv_ref, vn_ref, nv_ref)):
            old = c_ref[0, h]
            fresh = n_ref[0, h]
            rolled = pltpu.roll(old, past - new, 1)
            out_ref[0, h] = rolled
            out_ref[0, h, :, past - LANES:past] = jnp.where(
                is_new, jnp.concatenate([zpad, fresh], axis=1), rolled[:, past - LANES:past])
            alls.append(jnp.concatenate([old, fresh, zpad], axis=1).astype(BF16))
        k_all, v_all = alls
        q = jnp.concatenate([q_ref[0, h], jnp.zeros((rows - new, ATT_HEAD_DIM), F32)], axis=0).astype(BF16)
        s = jnp.dot(q, k_all, preferred_element_type=F32)
        ps, dens, lses = [], [], []
        for ok in oks:
            sg = jnp.where(ok, s, MASKED)
            m = jnp.max(sg, axis=1, keepdims=True)
            p = jnp.exp2(sg - m)
            den = jnp.sum(p, axis=1, keepdims=True)
            ps.append(p.astype(BF16))
            dens.append(den)
            lses.append(m + jnp.log2(den))
        o_all = _dot_nt(jnp.concatenate(ps, axis=0), v_all)
        m = jnp.maximum(jnp.maximum(lses[0], lses[1]), lses[2])
        ws = [jnp.exp2(l - m) for l in lses]
        o = sum(w * o_all[i * rows:(i + 1) * rows] / d for i, (w, d) in enumerate(zip(ws, dens)))
        o = o / (ws[0] + ws[1] + ws[2])
        o_ref[0, h] = o[0:new].astype(BF16)


def _attn_sample(q, kn, vn, cache_k, cache_v):
    nb, new, _ = q.shape
    past = cache_k.shape[1]
    hps = SAMPLE_HEADS_PER_STEP
    assert past >= ATT_WINDOW and past % LANES == 0 and new == SUBLANES and ATT_HEADS % hps == 0
    heads = (nb, new, ATT_HEADS, ATT_HEAD_DIM)
    q4 = q.reshape(heads).transpose(0, 2, 1, 3)
    kn4 = kn.reshape(heads).transpose(0, 2, 3, 1)
    vn4 = vn.reshape(heads).transpose(0, 2, 3, 1)
    ck = cache_k.transpose(0, 2, 3, 1)
    cv = cache_v.transpose(0, 2, 3, 1)
    hmap = lambda b, g: (b, g, 0, 0)
    qspec = pl.BlockSpec((1, hps, new, ATT_HEAD_DIM), hmap)
    nspec = pl.BlockSpec((1, hps, ATT_HEAD_DIM, new), hmap)
    cspec = pl.BlockSpec((1, hps, ATT_HEAD_DIM, past), hmap)
    o4, nk, nv = pl.pallas_call(
        functools.partial(_attn_sample_body, past=past, new=new),
        grid=(nb, ATT_HEADS // hps),
        in_specs=[qspec, nspec, nspec, cspec, cspec],
        out_specs=[qspec, cspec, cspec],
        out_shape=[
            jax.ShapeDtypeStruct((nb, ATT_HEADS, new, ATT_HEAD_DIM), BF16),
            jax.ShapeDtypeStruct(ck.shape, F32),
            jax.ShapeDtypeStruct(cv.shape, F32),
        ],
        compiler_params=_cparams(("parallel", "parallel")),
        name="attn_sample",
    )(q4, kn4, vn4, ck, cv)
    o = o4.transpose(0, 2, 1, 3).reshape(nb, new, ATT_WIDTH)
    return o, nk.transpose(0, 3, 1, 2), nv.transpose(0, 3, 1, 2)


ROUTE_COLS = LANES
META_COLS = SUBLANES


def _proj_route_body(y_ref, w_ref, x_ref, gt_ref, g_ref, sc_ref, sh_ref, whi_ref, wlo_ref, b_ref,
                     xo_ref, h_ref, meta_ref):
    tm = h_ref.shape[0]
    mix = jnp.dot(y_ref[...], w_ref[...], preferred_element_type=F32)
    x = x_ref[...] + gt_ref[...] * mix.reshape(x_ref.shape)
    xo_ref[...] = x
    h = _norm_mod(x, g_ref[...], sc_ref[...], sh_ref[...]).reshape(tm, D_MODEL)
    h_ref[...] = h
    hi = h.astype(BF16)
    lo = (h - hi.astype(F32)).astype(BF16)
    dot = functools.partial(jnp.dot, preferred_element_type=F32)
    logits = dot(hi, whi_ref[...]) + dot(lo, whi_ref[...]) + dot(hi, wlo_ref[...]) + b_ref[...]
    lane = lax.broadcasted_iota(I32, (tm, ROUTE_COLS), 1).astype(F32)
    big = float(ROUTE_COLS)
    neg = -jnp.inf
    gl = jnp.where(lane < MOE_GROUPS, logits, neg)
    gmax = jnp.max(gl, axis=1, keepdims=True)
    g_idx = jnp.min(jnp.where(gl == gmax, lane, big), axis=1, keepdims=True)
    g_w = 1.0 / jnp.sum(jnp.exp(gl - gmax), axis=1, keepdims=True)
    first = MOE_GROUPS + MOE_PER_GROUP * g_idx
    el = jnp.where((lane >= first) & (lane < first + MOE_PER_GROUP), logits, neg)
    v1 = jnp.max(el, axis=1, keepdims=True)
    i1 = jnp.min(jnp.where(el == v1, lane, big), axis=1, keepdims=True)
    el2 = jnp.where(lane == i1, neg, el)
    v2 = jnp.max(el2, axis=1, keepdims=True)
    i2 = jnp.min(jnp.where(el2 == v2, lane, big), axis=1, keepdims=True)
    t = jnp.exp(v2 - v1)
    w1 = g_w / (1.0 + t)
    w2 = g_w * t / (1.0 + t)
    meta = jnp.where(lane == 0, i1 - MOE_GROUPS,
                     jnp.where(lane == 1, i2 - MOE_GROUPS, jnp.where(lane == 2, w1, jnp.where(lane == 3, w2, 0.0))))
    meta_ref[...] = meta[:, 0:META_COLS]


def _proj_route(y, w, x, gt, g, sc, sh, w_hi, w_lo, bias):
    nb, sl, _ = x.shape
    kdim = y.shape[1]
    bt, lt = _row_tiling(nb, sl, ROW_TILE)
    nl = sl // lt
    tm = bt * lt
    c2 = lambda i: (0, 0)
    xspec = pl.BlockSpec((bt, lt, D_MODEL), lambda i: (i // nl, i % nl, 0))
    mspec = pl.BlockSpec((bt, 1, D_MODEL), lambda i: (i // nl, 0, 0))
    return pl.pallas_call(
        _proj_route_body,
        grid=((nb // bt) * nl,),
        in_specs=[
            pl.BlockSpec((tm, kdim), lambda i: (i, 0)),
            pl.BlockSpec((kdim, D_MODEL), c2),
            xspec, mspec,
            pl.BlockSpec((1, 1, D_MODEL), lambda i: (0, 0, 0)),
            mspec, mspec,
            pl.BlockSpec((D_MODEL, ROUTE_COLS), c2),
            pl.BlockSpec((D_MODEL, ROUTE_COLS), c2),
            pl.BlockSpec((1, ROUTE_COLS), c2),
        ],
        out_specs=[xspec, pl.BlockSpec((tm, D_MODEL), lambda i: (i, 0)), pl.BlockSpec((tm, META_COLS), lambda i: (i, 0))],
        out_shape=[jax.ShapeDtypeStruct(x.shape, F32), jax.ShapeDtypeStruct((nb * sl, D_MODEL), F32),
                   jax.ShapeDtypeStruct((nb * sl, META_COLS), F32)],
        compiler_params=_cparams(("parallel",)),
        name="proj_route",
    )(y, w, x, gt, g.reshape(1, 1, D_MODEL), sc, sh, w_hi, w_lo, bias)


FLAG_VALID, FLAG_FIRST, FLAG_LAST = 1, 2, 4


def _moe_experts_body(tok_ref, item_tile_ref, item_exp_ref, item_flag_ref,
                      h_hbm, meta_ref, w1_ref, w3_ref, w2_ref, out_hbm,
                      xbuf, acc, gsem, ssem, *, n_tiles):
    tm = MOE_TILE
    g = pl.program_id(0)
    n_items = pl.num_programs(0)
    tile = item_tile_ref[g]
    e = item_exp_ref[g]
    flag = item_flag_ref[g]
    slot = tile % 2

    def gather_start(t, s):
        for r in range(tm):
            tok = tok_ref[t * tm + r]
            pltpu.make_async_copy(h_hbm.at[pl.ds(tok, 1)], xbuf.at[s, pl.ds(r, 1)], gsem.at[s]).start()

    def gather_wait(s):
        pltpu.make_async_copy(h_hbm.at[pl.ds(0, tm)], xbuf.at[s], gsem.at[s]).wait()

    def scatter_start(t, s):
        for r in range(tm):
            tok = tok_ref[t * tm + r]
            pltpu.make_async_copy(acc.at[s, pl.ds(r, 1)], out_hbm.at[pl.ds(tok, 1)], ssem.at[s]).start()

    def scatter_wait(s):
        pltpu.make_async_copy(acc.at[s], out_hbm.at[pl.ds(0, tm)], ssem.at[s]).wait()

    def per_slot(fn):
        for s in range(2):
            pl.when(slot == s)(functools.partial(fn, s))

    @pl.when(g == 0)
    def _():
        gather_start(0, 0)

    @pl.when((flag & FLAG_FIRST) != 0)
    def _():
        @pl.when(tile >= 2)
        def _():
            scatter_wait(slot)

        gather_wait(slot)

        @pl.when(tile + 1 < n_tiles)
        def _():
            per_slot(lambda s: gather_start(tile + 1, 1 - s))

    def expert_out():
        x = xbuf[slot].astype(BF16)
        a = jnp.dot(x, w1_ref[0], preferred_element_type=F32)
        u = jnp.dot(x, w3_ref[0], preferred_element_type=F32)
        ef = e.astype(F32)
        gate = (jnp.where(meta_ref[:, 0:1] == ef, meta_ref[:, 2:3], 0.0)
                + jnp.where(meta_ref[:, 1:2] == ef, meta_ref[:, 3:4], 0.0))
        hm = (_silu(a) * u * gate).astype(BF16)
        return jnp.dot(hm, w2_ref[0], preferred_element_type=F32)

    @pl.when((flag & FLAG_FIRST) != 0)
    def _():
        acc[slot] = expert_out()

    @pl.when((flag & (FLAG_VALID | FLAG_FIRST)) == FLAG_VALID)
    def _():
        acc[slot] += expert_out()

    @pl.when((flag & FLAG_LAST) != 0)
    def _():
        per_slot(lambda s: scatter_start(tile, s))

    @pl.when(g == n_items - 1)
    def _():
        if n_tiles >= 2:
            scatter_wait((n_tiles - 2) % 2)
        scatter_wait((n_tiles - 1) % 2)


def _moe_plan(meta, n_tiles, n_items):
    tm = MOE_TILE
    ea = meta[:, 0].astype(I32)
    eb = meta[:, 1].astype(I32)
    key = jnp.minimum(ea, eb) * MOE_EXPERTS + jnp.maximum(ea, eb)
    order = jnp.argsort(key).astype(I32)
    meta_s = meta[order]
    ea_s = ea[order].reshape(n_tiles, tm)
    eb_s = eb[order].reshape(n_tiles, tm)
    experts = jnp.arange(MOE_EXPERTS, dtype=I32)
    present = jnp.any((ea_s[:, :, None] == experts) | (eb_s[:, :, None] == experts), axis=1)
    flat = present.reshape(-1)
    count = jnp.sum(flat.astype(I32))
    (idx,) = jnp.nonzero(flat, size=n_items, fill_value=0)
    idx = idx.astype(I32)
    pos = jnp.arange(n_items, dtype=I32)
    valid = pos < count
    last_idx = idx[jnp.maximum(count - 1, 0)]
    idx = jnp.where(valid, idx, last_idx)
    item_tile = idx // MOE_EXPERTS
    item_exp = idx % MOE_EXPERTS
    prev_tile = jnp.concatenate([jnp.full((1,), -1, I32), item_tile[:-1]])
    next_tile = jnp.concatenate([item_tile[1:], jnp.full((1,), -1, I32)])
    first = valid & (item_tile != prev_tile)
    last = valid & ((item_tile != next_tile) | (pos == count - 1))
    flags = valid.astype(I32) * FLAG_VALID + first.astype(I32) * FLAG_FIRST + last.astype(I32) * FLAG_LAST
    return order, meta_s, item_tile, item_exp, flags


def _moe_experts(h, meta, w1, w3, w2, layer):
    tokens = h.shape[0]
    tm = MOE_TILE
    assert tokens % tm == 0
    n_tiles = tokens // tm
    pair_classes = MOE_GROUPS * (MOE_PER_GROUP * (MOE_PER_GROUP - 1) // 2)
    n_items = min(MOE_EXPERTS * n_tiles, 2 * (n_tiles + pair_classes - 1))
    order, meta_s, item_tile, item_exp, flags = _moe_plan(meta, n_tiles, n_items)
    wmap = lambda g, tok, it, ie, fl: (layer, ie[g], 0, 0)
    wspec1 = pl.BlockSpec((None, 1, D_MODEL, MOE_HIDDEN), wmap)
    wspec2 = pl.BlockSpec((None, 1, MOE_HIDDEN, D_MODEL), wmap)
    return pl.pallas_call(
        functools.partial(_moe_experts_body, n_tiles=n_tiles),
        grid_spec=pltpu.PrefetchScalarGridSpec(
            num_scalar_prefetch=4,
            grid=(n_items,),
            in_specs=[
                pl.BlockSpec(memory_space=pl.ANY),
                pl.BlockSpec((tm, META_COLS), lambda g, tok, it, ie, fl: (it[g], 0)),
                wspec1, wspec1, wspec2,
            ],
            out_specs=pl.BlockSpec(memory_space=pl.ANY),
            scratch_shapes=[
                pltpu.VMEM((2, tm, D_MODEL), F32),
                pltpu.VMEM((2, tm, D_MODEL), F32),
                pltpu.SemaphoreType.DMA((2,)),
                pltpu.SemaphoreType.DMA((2,)),
            ],
        ),
        out_shape=jax.ShapeDtypeStruct((tokens, D_MODEL), F32),
        compiler_params=_cparams(("arbitrary",)),
        name="moe_experts",
    )(order, item_tile, item_exp, flags, h, meta_s, w1, w3, w2)


def _final_combine_body(x_ref, m_ref, gt_ref, fg_ref, o_ref):
    x = x_ref[...] + gt_ref[...] * m_ref[...].reshape(x_ref.shape)
    o_ref[...] = x * lax.rsqrt(jnp.mean(x * x, axis=-1, keepdims=True) + EPS) * fg_ref[...]


def _final_combine(x, moe_out, gt, final_g):
    nb, sl, _ = x.shape
    bt, lt = _row_tiling(nb, sl, ROW_TILE)
    nl = sl // lt
    xmap = lambda i: (i // nl, i % nl, 0)
    return pl.pallas_call(
        _final_combine_body,
        grid=((nb // bt) * nl,),
        in_specs=[
            pl.BlockSpec((bt, lt, D_MODEL), xmap),
            pl.BlockSpec((bt * lt, D_MODEL), lambda i: (i, 0)),
            pl.BlockSpec((bt, 1, D_MODEL), lambda i: (i // nl, 0, 0)),
            pl.BlockSpec((1, 1, D_MODEL), lambda i: (0, 0, 0)),
        ],
        out_specs=pl.BlockSpec((bt, lt, D_MODEL), xmap),
        out_shape=jax.ShapeDtypeStruct(x.shape, F32),
        compiler_params=_cparams(("parallel",)),
        name="final_combine",
    )(x, moe_out, gt, final_g.reshape(1, 1, D_MODEL))


def _rope_tables(pos):
    half = ATT_HEAD_DIM // 2
    inv_freq = ROPE_THETA ** (-jnp.arange(half, dtype=F32) / half)
    ang = pos.astype(F32)[:, None] * inv_freq[None, :]
    cos = jnp.cos(ang)
    sin = jnp.sin(ang)
    cos_h = jnp.concatenate([cos, cos], axis=1)
    sin_h = jnp.concatenate([-sin, sin], axis=1)
    return jnp.tile(cos_h, (1, ATT_HEADS)), jnp.tile(sin_h, (1, ATT_HEADS))


def _router_weights(wg, bg, we, be):
    w = jnp.concatenate([wg, we], axis=1)
    w = jnp.pad(w, ((0, 0), (0, ROUTE_COLS - w.shape[1])))
    b = jnp.pad(jnp.concatenate([bg, be]), (0, ROUTE_COLS - MOE_GROUPS - MOE_EXPERTS)).reshape(1, ROUTE_COLS)
    hi = w.astype(BF16)
    lo = (w - hi.astype(F32)).astype(BF16)
    return hi, lo, b


def _even_w_in_cols(w):
    dt0 = SSD_INNER + SSD_CONV_DIM
    dt1 = dt0 + SSD_HEADS
    zeros = jnp.zeros((w.shape[0], LANES - SSD_HEADS), w.dtype)
    return jnp.concatenate([w[:, :dt0], w[:, dt1:], w[:, dt0:dt1], zeros], axis=1).astype(BF16)


def _run_group(x, mods, pos, states, caches, wts):
    nb, sl, _ = x.shape
    ssd_in, conv_in, ret_in = states
    cos, sin = _rope_tables(pos)

    def mod_parts(i):
        m = mods[i].reshape(nb, 1, N_MOD, D_MODEL)
        return [m[:, :, j] for j in range(N_MOD)]

    sh1, sc1, gt1, sh2, sc2, gt2 = mod_parts(0)
    p = _even_in(x, wts["norm_mix_g"][0], sc1, sh1, wts["even_w_in"]).reshape(nb, sl, P_WIDTH)
    y, ssd_new, conv_new, ret_new = _even_core(
        p, cos, sin, wts["ssd_conv_w"], wts["ssd_conv_b"], wts["ssd_dt_bias"], wts["ssd_a_log"],
        wts["ssd_d"], wts["ssd_norm_g"], wts["ret_norm_g"], ssd_in, conv_in, ret_in)
    y = y.reshape(nb * sl, EVEN_OUT)
    x, h, meta = _proj_route(y, wts["even_w_out"], x, gt1, wts["norm_ffn_g"][0], sc2, sh2, *wts["router"][0])
    mix = _moe_experts(h, meta, *wts["experts"], 0)
    gt_moe = gt2

    sh1, sc1, gt1, sh2, sc2, gt2 = mod_parts(1)
    if caches is None:
        x, q, k, v = _qkv_rope(x, mix, gt_moe, wts["norm_mix_g"][1], sc1, sh1, wts["odd_w_qkv"], cos, sin)
        q3, k3, v3 = (a.reshape(nb, sl, ATT_WIDTH) for a in (q, k, v))
        o, k_t, v_t = _attn_prompt(q3, k3, v3)
        keep = min(ATT_WINDOW, sl)
        new_k, new_v = (a.transpose(0, 3, 1, 2)[:, sl - keep:] for a in (k_t, v_t))
    else:
        cos_t, sin_t = jnp.tile(cos, (nb, 1)), jnp.tile(sin, (nb, 1))
        x, q, k, v = _qkv_rope(x, mix, gt_moe, wts["norm_mix_g"][1], sc1, sh1, wts["odd_w_qkv"], cos_t, sin_t)
        q3, k3, v3 = (a.reshape(nb, sl, ATT_WIDTH) for a in (q, k, v))
        o, new_k, new_v = _attn_sample(q3, k3, v3, *caches)
    x, h, meta = _proj_route(o.reshape(nb * sl, ATT_WIDTH), wts["odd_w_out"], x, gt1,
                             wts["norm_ffn_g"][1], sc2, sh2, *wts["router"][1])
    mix = _moe_experts(h, meta, *wts["experts"], 1)
    x = _final_combine(x, mix, gt2, wts["final_norm_g"])
    return x, ssd_new[None], conv_new[None], ret_new[None], new_k[None], new_v[None]


def kernel(x_prompt, x_sample, state_ssd, state_conv, state_ret, cache_k, cache_v, c_prompt, c_sample, ada_w, ada_b, norm_mix_g, norm_ffn_g, final_norm_g, even_w_in, even_w_out, ssd_conv_w, ssd_conv_b, ssd_dt_bias, ssd_a_log, ssd_d, ssd_norm_g, ret_norm_g, odd_w_qkv, odd_w_out, moe_wg, moe_bg, moe_we, moe_be, moe_w1, moe_w3, moe_w2):
    depth = ada_w.shape[0]
    assert depth == 2 and even_w_in.shape[0] == 1 and odd_w_qkv.shape[0] == 1
    bp, sp, _ = x_prompt.shape
    bs, ss, _ = x_sample.shape

    wts = {
        "norm_mix_g": norm_mix_g, "norm_ffn_g": norm_ffn_g, "final_norm_g": final_norm_g,
        "even_w_in": _even_w_in_cols(even_w_in[0]), "even_w_out": even_w_out[0].astype(BF16),
        "ssd_conv_w": ssd_conv_w[0], "ssd_conv_b": ssd_conv_b[0], "ssd_dt_bias": ssd_dt_bias[0],
        "ssd_a_log": ssd_a_log[0], "ssd_d": ssd_d[0], "ssd_norm_g": ssd_norm_g[0], "ret_norm_g": ret_norm_g[0],
        "odd_w_qkv": odd_w_qkv[0].astype(BF16), "odd_w_out": odd_w_out[0].astype(BF16),
        "router": [_router_weights(moe_wg[i], moe_bg[i], moe_we[i], moe_be[i]) for i in range(depth)],
        "experts": (moe_w1.astype(BF16), moe_w3.astype(BF16), moe_w2.astype(BF16)),
    }
    mods = _adaln(jnp.concatenate([c_prompt, c_sample], axis=0), ada_w, ada_b)

    zeros_p = (
        jnp.zeros((bp, SSD_HEADS, SSD_HEAD_DIM, SSD_STATE), F32),
        jnp.zeros((bp, SSD_CONV - 1, SSD_CONV_DIM), F32),
        jnp.zeros((bp, RET_HEADS, RET_DV, RET_DK), F32),
    )
    out_p = _run_group(x_prompt, mods[:, :bp], jnp.arange(sp, dtype=I32), zeros_p, None, wts)
    out_s = _run_group(x_sample, mods[:, bp:], PAST_LEN + jnp.arange(ss, dtype=I32),
                       (state_ssd[0], state_conv[0], state_ret[0]), (cache_k[0], cache_v[0]), wts)
    return (out_p[0], out_s[0]) + out_p[1:] + out_s[1:]
```

```python
import functools
import math

import jax
import jax.numpy as jnp
from jax import lax
from jax.experimental import pallas as pl
from jax.experimental.pallas import tpu as pltpu

F32 = jnp.float32
BF16 = jnp.bfloat16
I32 = jnp.int32

D_MODEL = 1024
EPS = 1e-6
N_MOD = 6
SSD_HEADS = 16
SSD_HEAD_DIM = 64
SSD_INNER = SSD_HEADS * SSD_HEAD_DIM
SSD_GROUPS = 4
SSD_STATE = 64
SSD_CONV = 4
SSD_CONV_DIM = SSD_INNER + 2 * SSD_GROUPS * SSD_STATE
RET_HEADS = 8
RET_DK = 64
RET_DV = 128
RET_QK = RET_HEADS * RET_DK
RET_V = RET_HEADS * RET_DV
EVEN_OUT = SSD_INNER + RET_V
ATT_HEADS = 16
ATT_HEAD_DIM = 64
ATT_WIDTH = ATT_HEADS * ATT_HEAD_DIM
DILATED_PATTERNS = ((128, 1), (512, 4), (2048, 16))
ATT_WINDOW = 2048
PAST_LEN = 16384
ROPE_THETA = 10000.0
MOE_GROUPS = 4
MOE_PER_GROUP = 8
MOE_EXPERTS = MOE_GROUPS * MOE_PER_GROUP
MOE_HIDDEN = 256

LANES = 128
SUBLANES = 8
VMEM_LIMIT = 56 * 1024 * 1024

P_Z = 0
P_XBC = P_Z + SSD_INNER
P_Q = P_XBC + SSD_CONV_DIM
P_K = P_Q + RET_QK
P_V = P_K + RET_QK
P_G = P_V + RET_V
P_DT = P_G + RET_V
P_WIDTH = P_DT + LANES

SCAN_CHUNK = 128
ATT_BLOCK = 128
Q_SCALE = ATT_HEAD_DIM ** -0.5 * math.log2(math.e)
MASKED = -1e30
MOE_TILE = 256
ROW_TILE = 1024
QKV_ROWS = 512
EVEN_IN_ROWS = 1024
EVEN_IN_COLS = 1920
ADALN_COLS = 1024


def _cparams(sem):
    return pltpu.CompilerParams(dimension_semantics=sem, vmem_limit_bytes=VMEM_LIMIT)


def _row_tiling(nb, sl, target):
    if sl >= target:
        assert sl % target == 0
        return 1, target
    bt = min(nb, target // sl)
    assert nb % bt == 0
    return bt, sl


def _silu(x):
    return x * (0.5 * jnp.tanh(0.5 * x) + 0.5)


def _norm_mod(x, g, sc, sh):
    ms = jnp.mean(x * x, axis=-1, keepdims=True)
    return x * lax.rsqrt(ms + EPS) * g * (1.0 + sc) + sh


def _split3(x):
    hi = x.astype(BF16)
    r1 = x - hi.astype(F32)
    mid = r1.astype(BF16)
    lo = (r1 - mid.astype(F32)).astype(BF16)
    return hi, mid, lo


def _dot_exact_rhs(x, m_bf16):
    hi, mid, lo = _split3(x)
    dot = functools.partial(jnp.dot, preferred_element_type=F32)
    return dot(hi, m_bf16) + dot(mid, m_bf16) + dot(lo, m_bf16)


def _dot_nt(a, b):
    return lax.dot_general(a, b, (((1,), (1,)), ((), ())), preferred_element_type=F32)


def _rope(a, cos, sin_signed):
    w = a.shape[-1]
    lane = lax.broadcasted_iota(I32, a.shape, 1)
    first = (lane & (ATT_HEAD_DIM // 2)) == 0
    rot = jnp.where(first, pltpu.roll(a, w - ATT_HEAD_DIM // 2, 1), pltpu.roll(a, ATT_HEAD_DIM // 2, 1))
    return a * cos + rot * sin_signed


def _adaln_body(c_ref, w_ref, b_ref, o_ref):
    a = _silu(c_ref[...]).astype(BF16)
    o_ref[0] = jnp.dot(a, w_ref[0].astype(BF16), preferred_element_type=F32) + b_ref[0]


def _adaln(c_all, ada_w, ada_b):
    nb = c_all.shape[0]
    depth, _, n6 = ada_w.shape
    tn = ADALN_COLS
    return pl.pallas_call(
        _adaln_body,
        grid=(depth, n6 // tn),
        in_specs=[
            pl.BlockSpec((nb, D_MODEL), lambda i, j: (0, 0)),
            pl.BlockSpec((1, D_MODEL, tn), lambda i, j: (i, 0, j)),
            pl.BlockSpec((1, 1, tn), lambda i, j: (i, 0, j)),
        ],
        out_specs=pl.BlockSpec((1, nb, tn), lambda i, j: (i, 0, j)),
        out_shape=jax.ShapeDtypeStruct((depth, nb, n6), F32),
        compiler_params=_cparams(("parallel", "parallel")),
        name="adaln",
    )(c_all, ada_w, ada_b.reshape(depth, 1, n6))


def _even_in_body(x_ref, g_ref, sc_ref, sh_ref, w_ref, o_ref, h_scr):
    @pl.when(pl.program_id(1) == 0)
    def _():
        h = _norm_mod(x_ref[...], g_ref[...], sc_ref[...], sh_ref[...])
        h_scr[...] = h.reshape(h_scr.shape).astype(BF16)

    o_ref[...] = jnp.dot(h_scr[...], w_ref[...], preferred_element_type=F32)


def _even_in(x, g, sc, sh, w):
    nb, sl, _ = x.shape
    n = w.shape[1]
    bt, lt = _row_tiling(nb, sl, EVEN_IN_ROWS)
    nl = sl // lt
    tm = bt * lt
    tn = EVEN_IN_COLS
    assert n % tn == 0
    return pl.pallas_call(
        _even_in_body,
        grid=((nb // bt) * nl, n // tn),
        in_specs=[
            pl.BlockSpec((bt, lt, D_MODEL), lambda i, j: (i // nl, i % nl, 0)),
            pl.BlockSpec((1, 1, D_MODEL), lambda i, j: (0, 0, 0)),
            pl.BlockSpec((bt, 1, D_MODEL), lambda i, j: (i // nl, 0, 0)),
            pl.BlockSpec((bt, 1, D_MODEL), lambda i, j: (i // nl, 0, 0)),
            pl.BlockSpec((D_MODEL, tn), lambda i, j: (0, j)),
        ],
        out_specs=pl.BlockSpec((tm, tn), lambda i, j: (i, j)),
        out_shape=jax.ShapeDtypeStruct((nb * sl, n), F32),
        scratch_shapes=[pltpu.VMEM((tm, D_MODEL), BF16)],
        compiler_params=_cparams(("parallel", "arbitrary")),
        name="even_in",
    )(x, g.reshape(1, 1, D_MODEL), sc, sh, w)


def _ret_log_gamma(h):
    return math.log1p(-(2.0 ** (-5.0 - h)))


def _ret_log_gamma_exp(h, steps):
    return math.exp(_ret_log_gamma(h) * steps)


def _ret_tables(valid_len):
    cs = SCAN_CHUNK
    lg = jnp.asarray([_ret_log_gamma(h) for h in range(RET_HEADS)], F32)[:, None, None]
    steps = jnp.minimum(jnp.arange(cs) + 1, valid_len).astype(F32)
    gap = steps[:, None] - steps[None, :]
    causal = jnp.arange(cs)[None, :] <= jnp.arange(cs)[:, None]
    decay = jnp.exp(jnp.where(causal, lg * gap, -jnp.inf))
    wide = jnp.broadcast_to(steps[:, None], (cs, LANES))
    return decay, jnp.exp(lg * wide), jnp.exp(lg * (float(valid_len) - wide))


def _even_core_body(p_ref, cos_ref, sin_ref, cw_ref, cb_ref, dtb_ref, alog_ref, dskip_ref, ssdg_ref, retg_ref,
                    rdec_ref, rgrow_ref, rend_ref, ssd0_ref, conv0_ref, ret0_ref,
                    y_ref, ssd_out_ref, conv_out_ref, ret_out_ref,
                    xpad_scr, ssd_scr, ret_scr, yacc_scr, *, valid_len):
    cs = SCAN_CHUNK
    c = pl.program_id(1)
    nc = pl.num_programs(1)
    hd = SSD_HEAD_DIM

    @pl.when(c == 0)
    def _():
        ssd_scr[...] = ssd0_ref[0]
        ret_scr[...] = ret0_ref[0]
        xpad_scr[0:SUBLANES, :] = conv0_ref[0]

    row = lax.broadcasted_iota(I32, (cs, LANES), 0)
    col = lax.broadcasted_iota(I32, (cs, cs), 1)
    rowc = lax.broadcasted_iota(I32, (cs, cs), 0)
    causal = col <= rowc
    live = row < valid_len

    def proj(lo, hi):
        val = p_ref[0, :, lo:hi]
        if valid_len < cs:
            val = jnp.concatenate([val, jnp.zeros((cs - valid_len, hi - lo), F32)], axis=0)
        return val

    def emit(lo, hi, val):
        y_ref[0, :, lo:hi] = val[0:valid_len].astype(BF16)

    xpad_scr[SUBLANES:SUBLANES + cs, :] = proj(P_XBC, P_XBC + SSD_CONV_DIM)
    conv = cb_ref[...] + xpad_scr[SUBLANES:SUBLANES + cs, :] * cw_ref[SSD_CONV - 1:SSD_CONV, :]
    for back in range(1, SSD_CONV):
        tap = SSD_CONV - 1 - back
        conv = conv + xpad_scr[SUBLANES - back:SUBLANES - back + cs, :] * cw_ref[tap:tap + 1, :]
    xc = _silu(conv)

    @pl.when(c == nc - 1)
    def _():
        conv_out_ref[0] = xpad_scr[valid_len:valid_len + SUBLANES, :]

    xpad_scr[0:SUBLANES, :] = xpad_scr[cs:cs + SUBLANES, :]

    xh = xc[:, 0:SSD_INNER]
    if valid_len < cs:
        xh = jnp.where(live[:, 0:1], xh, 0.0)

    dt_in = proj(P_DT, P_DT + LANES).T[0:SSD_HEADS, :] + dtb_ref[...]
    dt = jnp.maximum(dt_in, 0.0) + jnp.log1p(jnp.exp(-jnp.abs(dt_in)))
    la = dt * (-jnp.exp(alog_ref[...]))
    if valid_len < cs:
        la = jnp.where(lax.broadcasted_iota(I32, (SSD_HEADS, cs), 1) < valid_len, la, 0.0)
    acum = _dot_exact_rhs(la, (rowc <= col).astype(BF16))
    acum_last = acum[:, cs - 1:cs]
    w_state = dt * jnp.exp(acum_last - acum)
    cdecay = jnp.exp(acum_last)
    acum_c = jnp.concatenate([acum, jnp.zeros((LANES - SSD_HEADS, cs), F32)], axis=0).T
    eacum_c = jnp.exp(acum_c)

    xh_b = xh.astype(BF16)
    xh_t = xh.T

    for grp in range(SSD_GROUPS):
        b_g = xc[:, SSD_INNER + grp * SSD_STATE:SSD_INNER + (grp + 1) * SSD_STATE].astype(BF16)
        c_g = xc[:, SSD_INNER + (SSD_GROUPS + grp) * SSD_STATE:SSD_INNER + (SSD_GROUPS + grp + 1) * SSD_STATE].astype(BF16)
        s_g = _dot_nt(c_g, b_g)
        for hh in range(SSD_HEADS // SSD_GROUPS):
            h = grp * (SSD_HEADS // SSD_GROUPS) + hh
            lo, hi = h * hd, (h + 1) * hd
            decay = jnp.exp(jnp.where(causal, acum_c[:, h:h + 1] - acum[h:h + 1, :], -jnp.inf))
            m = (s_g * decay * dt[h:h + 1, :]).astype(BF16)
            y_h = jnp.dot(m, xh_b[:, lo:hi], preferred_element_type=F32)
            h_prev = ssd_scr[h]
            y_h = y_h + _dot_nt(c_g, h_prev.astype(BF16)) * eacum_c[:, h:h + 1]
            yacc_scr[:, lo:hi] = y_h
            xw_t = (xh_t[lo:hi, :] * w_state[h:h + 1, :]).astype(BF16)
            ssd_scr[h] = h_prev * cdecay[h:h + 1, :] + jnp.dot(xw_t, b_g, preferred_element_type=F32)

    y = yacc_scr[...] + dskip_ref[...] * xh
    y = y * _silu(proj(P_Z, P_Z + SSD_INNER))
    y = y * lax.rsqrt(jnp.mean(y * y, axis=-1, keepdims=True) + EPS) * ssdg_ref[...]
    emit(0, SSD_INNER, y)

    cos = cos_ref[:, 0:RET_QK]
    sin = sin_ref[:, 0:RET_QK]
    rq = _rope(proj(P_Q, P_Q + RET_QK), cos, sin).astype(BF16)
    rk = (_rope(proj(P_K, P_K + RET_QK), cos, sin) * RET_DK ** -0.5).astype(BF16)
    rv = proj(P_V, P_V + RET_V)
    if valid_len < cs:
        rv = jnp.where(live[:, 0:1], rv, 0.0)
    for h in range(RET_HEADS):
        q_h = rq[:, h * RET_DK:(h + 1) * RET_DK]
        k_h = rk[:, h * RET_DK:(h + 1) * RET_DK]
        v_h = rv[:, h * RET_DV:(h + 1) * RET_DV]
        m = (_dot_nt(q_h, k_h) * rdec_ref[h]).astype(BF16)
        o_h = jnp.dot(m, v_h.astype(BF16), preferred_element_type=F32)
        s_prev = ret_scr[h]
        o_h = o_h + _dot_nt(q_h, s_prev.astype(BF16)) * rgrow_ref[h]
        v_te_t = (v_h * rend_ref[h]).T.astype(BF16)
        ret_scr[h] = s_prev * _ret_log_gamma_exp(h, valid_len) + jnp.dot(v_te_t, k_h, preferred_element_type=F32)
        o_h = o_h * lax.rsqrt(jnp.mean(o_h * o_h, axis=-1, keepdims=True) + EPS) * retg_ref[:, h * RET_DV:(h + 1) * RET_DV]
        o_h = o_h * _silu(proj(P_G + h * RET_DV, P_G + (h + 1) * RET_DV))
        emit(SSD_INNER + h * RET_DV, SSD_INNER + (h + 1) * RET_DV, o_h)

    @pl.when(c == nc - 1)
    def _():
        ssd_out_ref[0] = ssd_scr[...]
        ret_out_ref[0] = ret_scr[...]


def _even_core(p, cos, sin, conv_w, conv_b, dt_bias, a_log, d_skip, ssd_norm_g, ret_norm_g, ssd0, conv0, ret0):
    nb, sl, _ = p.shape
    cs = SCAN_CHUNK
    if sl % cs == 0:
        nc, valid_len = sl // cs, cs
    else:
        assert sl < cs and sl % SUBLANES == 0
        nc, valid_len = 1, sl
        grow = ((0, cs - sl), (0, 0))
        cos, sin = jnp.pad(cos, grow), jnp.pad(sin, grow)
    dtb = jnp.broadcast_to(dt_bias[:, None], (SSD_HEADS, cs))
    alog = jnp.broadcast_to(a_log[:, None], (SSD_HEADS, cs))
    dskip_x = jnp.repeat(d_skip, SSD_HEAD_DIM).reshape(1, SSD_INNER)
    conv0p = jnp.pad(conv0, ((0, 0), (SUBLANES - (SSD_CONV - 1), 0), (0, 0)))
    assert RET_DV == LANES
    const2 = lambda b, c: (0, 0)
    const3 = lambda b, c: (0, 0, 0)
    st4 = lambda b, c: (b, 0, 0, 0)
    y, ssd_new, conv_new, ret_new = pl.pallas_call(
        functools.partial(_even_core_body, valid_len=valid_len),
        grid=(nb, nc),
        in_specs=[
            pl.BlockSpec((1, valid_len, P_WIDTH), lambda b, c: (b, c, 0)),
            pl.BlockSpec((cs, ATT_WIDTH), lambda b, c: (c, 0)),
            pl.BlockSpec((cs, ATT_WIDTH), lambda b, c: (c, 0)),
            pl.BlockSpec((SSD_CONV, SSD_CONV_DIM), const2),
            pl.BlockSpec((1, SSD_CONV_DIM), const2),
            pl.BlockSpec((SSD_HEADS, cs), const2),
            pl.BlockSpec((SSD_HEADS, cs), const2),
            pl.BlockSpec((1, SSD_INNER), const2),
            pl.BlockSpec((1, SSD_INNER), const2),
            pl.BlockSpec((1, RET_V), const2),
            pl.BlockSpec((RET_HEADS, cs, cs), const3),
            pl.BlockSpec((RET_HEADS, cs, LANES), const3),
            pl.BlockSpec((RET_HEADS, cs, LANES), const3),
            pl.BlockSpec((1, SSD_HEADS, SSD_HEAD_DIM, SSD_STATE), st4),
            pl.BlockSpec((1, SUBLANES, SSD_CONV_DIM), lambda b, c: (b, 0, 0)),
            pl.BlockSpec((1, RET_HEADS, RET_DV, RET_DK), st4),
        ],
        out_specs=[
            pl.BlockSpec((1, valid_len, EVEN_OUT), lambda b, c: (b, c, 0)),
            pl.BlockSpec((1, SSD_HEADS, SSD_HEAD_DIM, SSD_STATE), st4),
            pl.BlockSpec((1, SUBLANES, SSD_CONV_DIM), lambda b, c: (b, 0, 0)),
            pl.BlockSpec((1, RET_HEADS, RET_DV, RET_DK), st4),
        ],
        out_shape=[
            jax.ShapeDtypeStruct((nb, sl, EVEN_OUT), BF16),
            jax.ShapeDtypeStruct((nb, SSD_HEADS, SSD_HEAD_DIM, SSD_STATE), F32),
            jax.ShapeDtypeStruct((nb, SUBLANES, SSD_CONV_DIM), F32),
            jax.ShapeDtypeStruct((nb, RET_HEADS, RET_DV, RET_DK), F32),
        ],
        scratch_shapes=[
            pltpu.VMEM((cs + SUBLANES, SSD_CONV_DIM), F32),
            pltpu.VMEM((SSD_HEADS, SSD_HEAD_DIM, SSD_STATE), F32),
            pltpu.VMEM((RET_HEADS, RET_DV, RET_DK), F32),
            pltpu.VMEM((cs, SSD_INNER), F32),
        ],
        compiler_params=_cparams(("parallel", "arbitrary")),
        name="even_core",
    )(p, cos, sin, conv_w, conv_b.reshape(1, SSD_CONV_DIM), dtb, alog, dskip_x,
      ssd_norm_g.reshape(1, SSD_INNER), ret_norm_g.reshape(1, RET_V), *_ret_tables(valid_len), ssd0, conv0p, ret0)
    return y, ssd_new, conv_new[:, SUBLANES - (SSD_CONV - 1):], ret_new


def _qkv_body(x_ref, m_ref, gt_ref, g_ref, sc_ref, sh_ref, w_ref, cos_ref, sin_ref,
              xo_ref, q_ref, k_ref, v_ref):
    x = x_ref[...] + gt_ref[...] * m_ref[...].reshape(x_ref.shape)
    xo_ref[...] = x
    h = _norm_mod(x, g_ref[...], sc_ref[...], sh_ref[...])
    h = h.reshape(q_ref.shape).astype(BF16)
    cos = cos_ref[...]
    sin = sin_ref[...]
    dot = functools.partial(jnp.dot, preferred_element_type=F32)
    q_ref[...] = _rope(dot(h, w_ref[:, 0:ATT_WIDTH]), cos, sin) * Q_SCALE
    k_ref[...] = _rope(dot(h, w_ref[:, ATT_WIDTH:2 * ATT_WIDTH]), cos, sin)
    v_ref[...] = dot(h, w_ref[:, 2 * ATT_WIDTH:3 * ATT_WIDTH])


def _qkv_rope(x, mix, gt, g, sc, sh, w, cos, sin):
    nb, sl, _ = x.shape
    bt, lt = _row_tiling(nb, sl, QKV_ROWS)
    nl = sl // lt
    tm = bt * lt
    assert cos.shape[0] in (sl, nb * sl) and (cos.shape[0] == sl) == (bt == 1)
    ntab = cos.shape[0] // tm
    out = jax.ShapeDtypeStruct((nb * sl, ATT_WIDTH), F32)
    ospec = pl.BlockSpec((tm, ATT_WIDTH), lambda i: (i, 0))
    tspec = pl.BlockSpec((tm, ATT_WIDTH), lambda i: (i % ntab, 0))
    xspec = pl.BlockSpec((bt, lt, D_MODEL), lambda i: (i // nl, i % nl, 0))
    mspec = pl.BlockSpec((bt, 1, D_MODEL), lambda i: (i // nl, 0, 0))
    out_specs = [xspec, ospec, ospec, ospec]
    out_shape = [jax.ShapeDtypeStruct(x.shape, F32), out, out, out]
    return pl.pallas_call(
        _qkv_body,
        grid=((nb // bt) * nl,),
        in_specs=[
            xspec, ospec, mspec,
            pl.BlockSpec((1, 1, D_MODEL), lambda i: (0, 0, 0)),
            mspec, mspec,
            pl.BlockSpec((D_MODEL, 3 * ATT_WIDTH), lambda i: (0, 0)),
            tspec, tspec,
        ],
        out_specs=out_specs,
        out_shape=out_shape,
        compiler_params=_cparams(("parallel",)),
        name="qkv_rope",
    )(x, mix, gt, g.reshape(1, 1, D_MODEL), sc, sh, w, cos, sin)


def _attn_prompt_body(q_ref, k_ref, v_ref, o_ref, kt_ref, vt_ref, ob_scr, lse_scr, p_scr, m_scr, *, seq):
    blk = ATT_BLOCK
    half = ATT_HEAD_DIM
    lane = lax.broadcasted_iota(I32, (blk, LANES), 1)
    head_a = lane < half
    rowi = lax.broadcasted_iota(I32, (2 * blk, blk), 0) & (blk - 1)
    coli = lax.broadcasted_iota(I32, (2 * blk, blk), 1)
    mask_cur = coli <= rowi
    mask_prev = coli >= rowi
    ones = jnp.ones((blk, LANES), BF16)

    for i in range(seq // blk):
        rows = slice(i * blk, (i + 1) * blk)
        kt_ref[0, :, :, rows] = k_ref[0, rows, :].T.reshape(2, half, blk)
        vt_ref[0, :, :, rows] = v_ref[0, rows, :].T.reshape(2, half, blk)

    def ld(ref, dil, s0):
        if dil == 1:
            return ref[0, pl.ds(s0, blk), :]
        return ref[0, pl.ds(s0, blk, stride=dil), :]

    def probs(dil, start, pstart, slot):
        q = ld(q_ref, dil, start)
        q2 = jnp.concatenate([jnp.where(head_a, q, 0.0), jnp.where(head_a, 0.0, q)], axis=0).astype(BF16)
        sc = jnp.where(mask_cur, _dot_nt(q2, ld(k_ref, dil, start).astype(BF16)), MASKED)
        if pstart is None:
            m = jnp.max(sc, axis=1, keepdims=True)
        else:
            sp = jnp.where(mask_prev, _dot_nt(q2, ld(k_ref, dil, pstart).astype(BF16)), MASKED)
            m = jnp.max(jnp.maximum(sc, sp), axis=1, keepdims=True)
            p_scr[slot, :, blk:2 * blk] = jnp.exp2(sp - m).astype(BF16)
        p_scr[slot, :, 0:blk] = jnp.exp2(sc - m).astype(BF16)
        m_scr[slot] = jnp.where(head_a, m[0:blk], m[blk:2 * blk])

    def weigh(gi, dil, start, pstart, slot):
        ve = jnp.concatenate([ld(v_ref, dil, start).astype(BF16), ones], axis=1)
        if pstart is None:
            oe = jnp.dot(p_scr[slot, :, 0:blk], ve, preferred_element_type=F32)
        else:
            vpe = jnp.concatenate([ld(v_ref, dil, pstart).astype(BF16), ones], axis=1)
            oe = jnp.dot(p_scr[slot], jnp.concatenate([ve, vpe], axis=0), preferred_element_type=F32)
        num = jnp.where(head_a, oe[0:blk, 0:LANES], oe[blk:2 * blk, 0:LANES])
        den = jnp.where(head_a, oe[0:blk, LANES:2 * LANES], oe[blk:2 * blk, LANES:2 * LANES])
        rows = pl.ds(start, blk) if dil == 1 else pl.ds(start, blk, stride=dil)
        ob_scr[gi, rows, :] = num / den
        lse_scr[gi, rows, :] = m_scr[slot] + jnp.log2(den)

    def pipelined(gi, dil, count, coords):
        s1 = lambda t, slot: probs(dil, *coords(t), slot)
        s2 = lambda t, slot: weigh(gi, dil, *coords(t), slot)
        s1(0, 0)
        if count == 1:
            s2(0, 0)
            return
        s1(1, 1)
        trips = (count - 2) // 2

        def body(i, carry):
            a = 2 * i
            s2(a, 0)
            s2(a + 1, 1)
            s1(a + 2, 0)
            s1(a + 3, 1)
            return carry

        if trips > 0:
            lax.fori_loop(0, trips, body, 0)
        s2(2 * trips, 0)
        if (count - 2) % 2 == 1:
            s1(count - 1, 0)
        s2(2 * trips + 1, 1)
        if (count - 2) % 2 == 1:
            s2(count - 1, 0)

    for gi, (window, dil) in enumerate(DILATED_PATTERNS):
        assert window // dil == blk and (seq // dil) % blk == 0
        per_residue = seq // dil // blk
        span = blk * dil
        pipelined(gi, dil, dil, lambda t: (t, None))
        if per_residue > 1:
            def coords(t, dil=dil, span=span):
                start = t % dil + (1 + t // dil) * span
                return start, start - span
            pipelined(gi, dil, dil * (per_residue - 1), coords)

    def merge(i, carry):
        rows = pl.ds(pl.multiple_of(i * blk, blk), blk)
        l0 = lse_scr[0, rows, :]
        l1 = lse_scr[1, rows, :]
        l2 = lse_scr[2, rows, :]
        m = jnp.maximum(jnp.maximum(l0, l1), l2)
        w0 = jnp.exp2(l0 - m)
        w1 = jnp.exp2(l1 - m)
        w2 = jnp.exp2(l2 - m)
        o = (w0 * ob_scr[0, rows, :] + w1 * ob_scr[1, rows, :] + w2 * ob_scr[2, rows, :]) / (w0 + w1 + w2)
        o_ref[0, rows, :] = o.astype(BF16)
        return carry

    lax.fori_loop(0, seq // blk, merge, 0)


def _attn_prompt(q, k, v):
    nb, seq, _ = q.shape
    npair = ATT_WIDTH // LANES
    spec = pl.BlockSpec((1, seq, LANES), lambda b, h: (b, 0, h))
    tspec = pl.BlockSpec((1, 2, ATT_HEAD_DIM, seq), lambda b, h: (b, h, 0, 0))
    tshape = jax.ShapeDtypeStruct((nb, ATT_HEADS, ATT_HEAD_DIM, seq), F32)
    return pl.pallas_call(
        functools.partial(_attn_prompt_body, seq=seq),
        grid=(nb, npair),
        in_specs=[spec, spec, spec],
        out_specs=[spec, tspec, tspec],
        out_shape=[jax.ShapeDtypeStruct((nb, seq, ATT_WIDTH), BF16), tshape, tshape],
        scratch_shapes=[
            pltpu.VMEM((len(DILATED_PATTERNS), seq, LANES), F32),
            pltpu.VMEM((len(DILATED_PATTERNS), seq, LANES), F32),
            pltpu.VMEM((2, 2 * ATT_BLOCK, 2 * ATT_BLOCK), BF16),
            pltpu.VMEM((2, ATT_BLOCK, LANES), F32),
        ],
        compiler_params=_cparams(("parallel", "parallel")),
        name="attn_prompt",
    )(q, k, v)


SAMPLE_HEADS_PER_STEP = 8
SAMPLE_ROWS = 2 * SUBLANES


def _attn_sample_body(q_ref, kn_ref, vn_ref, ck_ref, cv_ref, o_ref, nk_ref, nv_ref, *, past, new):
    rows = SAMPLE_ROWS
    keys = past + LANES
    lane = lax.broadcasted_iota(I32, (ATT_HEAD_DIM, LANES), 1)
    is_new = lane >= LANES - new
    t_idx = lax.broadcasted_iota(I32, (rows, keys), 0) & (new - 1)
    k_idx = lax.broadcasted_iota(I32, (rows, keys), 1)
    dist = past + t_idx - k_idx
    oks = [(dist >= 0) & (dist <= window) & ((dist & (dil - 1)) == 0) for window, dil in DILATED_PATTERNS]
    zpad = jnp.zeros((ATT_HEAD_DIM, LANES - new), F32)

    for h in range(SAMPLE_HEADS_PER_STEP):
        alls = []
        for c_ref, n_ref, out_ref in ((ck_ref, kn_ref, nk_ref), (cv_ref, vn_ref, nv_ref)):
            old = c_ref[0, h]
            fresh = n_ref[0, h]
            rolled = pltpu.roll(old, past - new, 1)
            out_ref[0, h] = rolled
            out_ref[0, h, :, past - LANES:past] = jnp.where(
                is_new, jnp.concatenate([zpad, fresh], axis=1), rolled[:, past - LANES:past])
            alls.append(jnp.concatenate([old, fresh, zpad], axis=1).astype(BF16))
        k_all, v_all = alls
        q = jnp.concatenate([q_ref[0, h], jnp.zeros((rows - new, ATT_HEAD_DIM), F32)], axis=0).astype(BF16)
        s = jnp.dot(q, k_all, preferred_element_type=F32)
        ps, dens, lses = [], [], []
        for ok in oks:
            sg = jnp.where(ok, s, MASKED)
            m = jnp.max(sg, axis=1, keepdims=True)
            p = jnp.exp2(sg - m)
            den = jnp.sum(p, axis=1, keepdims=True)
            ps.append(p.astype(BF16))
            dens.append(den)
            lses.append(m + jnp.log2(den))
        o_all = _dot_nt(jnp.concatenate(ps, axis=0), v_all)
        m = jnp.maximum(jnp.maximum(lses[0], lses[1]), lses[2])
        ws = [jnp.exp2(l - m) for l in lses]
        o = sum(w * o_all[i * rows:(i + 1) * rows] / d for i, (w, d) in enumerate(zip(ws, dens)))
        o = o / (ws[0] + ws[1] + ws[2])
        o_ref[0, h] = o[0:new].astype(BF16)


def _attn_sample(q, kn, vn, cache_k, cache_v):
    nb, new, _ = q.shape
    past = cache_k.shape[1]
    hps = SAMPLE_HEADS_PER_STEP
    assert past >= ATT_WINDOW and past % LANES == 0 and new == SUBLANES and ATT_HEADS % hps == 0
    heads = (nb, new, ATT_HEADS, ATT_HEAD_DIM)
    q4 = q.reshape(heads).transpose(0, 2, 1, 3)
    kn4 = kn.reshape(heads).transpose(0, 2, 3, 1)
    vn4 = vn.reshape(heads).transpose(0, 2, 3, 1)
    ck = cache_k.transpose(0, 2, 3, 1)
    cv = cache_v.transpose(0, 2, 3, 1)
    hmap = lambda b, g: (b, g, 0, 0)
    qspec = pl.BlockSpec((1, hps, new, ATT_HEAD_DIM), hmap)
    nspec = pl.BlockSpec((1, hps, ATT_HEAD_DIM, new), hmap)
    cspec = pl.BlockSpec((1, hps, ATT_HEAD_DIM, past), hmap)
    o4, nk, nv = pl.pallas_call(
        functools.partial(_attn_sample_body, past=past, new=new),
        grid=(nb, ATT_HEADS // hps),
        in_specs=[qspec, nspec, nspec, cspec, cspec],
        out_specs=[qspec, cspec, cspec],
        out_shape=[
            jax.ShapeDtypeStruct((nb, ATT_HEADS, new, ATT_HEAD_DIM), BF16),
            jax.ShapeDtypeStruct(ck.shape, F32),
            jax.ShapeDtypeStruct(cv.shape, F32),
        ],
        compiler_params=_cparams(("parallel", "parallel")),
        name="attn_sample",
    )(q4, kn4, vn4, ck, cv)
    o = o4.transpose(0, 2, 1, 3).reshape(nb, new, ATT_WIDTH)
    return o, nk.transpose(0, 3, 1, 2), nv.transpose(0, 3, 1, 2)


ROUTE_COLS = LANES
META_COLS = SUBLANES


def _proj_route_body(y_ref, w_ref, x_ref, gt_ref, g_ref, sc_ref, sh_ref, whi_ref, wlo_ref, b_ref,
                     xo_ref, h_ref, meta_ref):
    tm = h_ref.shape[0]
    mix = jnp.dot(y_ref[...], w_ref[...], preferred_element_type=F32)
    x = x_ref[...] + gt_ref[...] * mix.reshape(x_ref.shape)
    xo_ref[...] = x
    h = _norm_mod(x, g_ref[...], sc_ref[...], sh_ref[...]).reshape(tm, D_MODEL)
    h_ref[...] = h
    hi = h.astype(BF16)
    lo = (h - hi.astype(F32)).astype(BF16)
    dot = functools.partial(jnp.dot, preferred_element_type=F32)
    logits = dot(hi, whi_ref[...]) + dot(lo, whi_ref[...]) + dot(hi, wlo_ref[...]) + b_ref[...]
    lane = lax.broadcasted_iota(I32, (tm, ROUTE_COLS), 1).astype(F32)
    big = float(ROUTE_COLS)
    neg = -jnp.inf
    gl = jnp.where(lane < MOE_GROUPS, logits, neg)
    gmax = jnp.max(gl, axis=1, keepdims=True)
    g_idx = jnp.min(jnp.where(gl == gmax, lane, big), axis=1, keepdims=True)
    g_w = 1.0 / jnp.sum(jnp.exp(gl - gmax), axis=1, keepdims=True)
    first = MOE_GROUPS + MOE_PER_GROUP * g_idx
    el = jnp.where((lane >= first) & (lane < first + MOE_PER_GROUP), logits, neg)
    v1 = jnp.max(el, axis=1, keepdims=True)
    i1 = jnp.min(jnp.where(el == v1, lane, big), axis=1, keepdims=True)
    el2 = jnp.where(lane == i1, neg, el)
    v2 = jnp.max(el2, axis=1, keepdims=True)
    i2 = jnp.min(jnp.where(el2 == v2, lane, big), axis=1, keepdims=True)
    t = jnp.exp(v2 - v1)
    w1 = g_w / (1.0 + t)
    w2 = g_w * t / (1.0 + t)
    meta = jnp.where(lane == 0, i1 - MOE_GROUPS,
                     jnp.where(lane == 1, i2 - MOE_GROUPS, jnp.where(lane == 2, w1, jnp.where(lane == 3, w2, 0.0))))
    meta_ref[...] = meta[:, 0:META_COLS]


def _proj_route(y, w, x, gt, g, sc, sh, w_hi, w_lo, bias):
    nb, sl, _ = x.shape
    kdim = y.shape[1]
    bt, lt = _row_tiling(nb, sl, ROW_TILE)
    nl = sl // lt
    tm = bt * lt
    c2 = lambda i: (0, 0)
    xspec = pl.BlockSpec((bt, lt, D_MODEL), lambda i: (i // nl, i % nl, 0))
    mspec = pl.BlockSpec((bt, 1, D_MODEL), lambda i: (i // nl, 0, 0))
    return pl.pallas_call(
        _proj_route_body,
        grid=((nb // bt) * nl,),
        in_specs=[
            pl.BlockSpec((tm, kdim), lambda i: (i, 0)),
            pl.BlockSpec((kdim, D_MODEL), c2),
            xspec, mspec,
            pl.BlockSpec((1, 1, D_MODEL), lambda i: (0, 0, 0)),
            mspec, mspec,
            pl.BlockSpec((D_MODEL, ROUTE_COLS), c2),
            pl.BlockSpec((D_MODEL, ROUTE_COLS), c2),
            pl.BlockSpec((1, ROUTE_COLS), c2),
        ],
        out_specs=[xspec, pl.BlockSpec((tm, D_MODEL), lambda i: (i, 0)), pl.BlockSpec((tm, META_COLS), lambda i: (i, 0))],
        out_shape=[jax.ShapeDtypeStruct(x.shape, F32), jax.ShapeDtypeStruct((nb * sl, D_MODEL), F32),
                   jax.ShapeDtypeStruct((nb * sl, META_COLS), F32)],
        compiler_params=_cparams(("parallel",)),
        name="proj_route",
    )(y, w, x, gt, g.reshape(1, 1, D_MODEL), sc, sh, w_hi, w_lo, bias)


FLAG_VALID, FLAG_FIRST, FLAG_LAST = 1, 2, 4


def _moe_experts_body(tok_ref, item_tile_ref, item_exp_ref, item_flag_ref,
                      h_hbm, meta_ref, w1_ref, w3_ref, w2_ref, out_hbm,
                      xbuf, acc, gsem, ssem, *, n_tiles):
    tm = MOE_TILE
    g = pl.program_id(0)
    n_items = pl.num_programs(0)
    tile = item_tile_ref[g]
    e = item_exp_ref[g]
    flag = item_flag_ref[g]
    slot = tile % 2

    def gather_start(t, s):
        for r in range(tm):
            tok = tok_ref[t * tm + r]
            pltpu.make_async_copy(h_hbm.at[pl.ds(tok, 1)], xbuf.at[s, pl.ds(r, 1)], gsem.at[s]).start()

    def gather_wait(s):
        pltpu.make_async_copy(h_hbm.at[pl.ds(0, tm)], xbuf.at[s], gsem.at[s]).wait()

    def scatter_start(t, s):
        for r in range(tm):
            tok = tok_ref[t * tm + r]
            pltpu.make_async_copy(acc.at[s, pl.ds(r, 1)], out_hbm.at[pl.ds(tok, 1)], ssem.at[s]).start()

    def scatter_wait(s):
        pltpu.make_async_copy(acc.at[s], out_hbm.at[pl.ds(0, tm)], ssem.at[s]).wait()

    def per_slot(fn):
        for s in range(2):
            pl.when(slot == s)(functools.partial(fn, s))

    @pl.when(g == 0)
    def _():
        gather_start(0, 0)

    @pl.when((flag & FLAG_FIRST) != 0)
    def _():
        @pl.when(tile >= 2)
        def _():
            scatter_wait(slot)

        gather_wait(slot)

        @pl.when(tile + 1 < n_tiles)
        def _():
            per_slot(lambda s: gather_start(tile + 1, 1 - s))

    def expert_out():
        x = xbuf[slot].astype(BF16)
        a = jnp.dot(x, w1_ref[0], preferred_element_type=F32)
        u = jnp.dot(x, w3_ref[0], preferred_element_type=F32)
        ef = e.astype(F32)
        gate = (jnp.where(meta_ref[:, 0:1] == ef, meta_ref[:, 2:3], 0.0)
                + jnp.where(meta_ref[:, 1:2] == ef, meta_ref[:, 3:4], 0.0))
        hm = (_silu(a) * u * gate).astype(BF16)
        return jnp.dot(hm, w2_ref[0], preferred_element_type=F32)

    @pl.when((flag & FLAG_FIRST) != 0)
    def _():
        acc[slot] = expert_out()

    @pl.when((flag & (FLAG_VALID | FLAG_FIRST)) == FLAG_VALID)
    def _():
        acc[slot] += expert_out()

    @pl.when((flag & FLAG_LAST) != 0)
    def _():
        per_slot(lambda s: scatter_start(tile, s))

    @pl.when(g == n_items - 1)
    def _():
        if n_tiles >= 2:
            scatter_wait((n_tiles - 2) % 2)
        scatter_wait((n_tiles - 1) % 2)


def _moe_plan(meta, n_tiles, n_items):
    tm = MOE_TILE
    ea = meta[:, 0].astype(I32)
    eb = meta[:, 1].astype(I32)
    key = jnp.minimum(ea, eb) * MOE_EXPERTS + jnp.maximum(ea, eb)
    order = jnp.argsort(key).astype(I32)
    meta_s = meta[order]
    ea_s = ea[order].reshape(n_tiles, tm)
    eb_s = eb[order].reshape(n_tiles, tm)
    experts = jnp.arange(MOE_EXPERTS, dtype=I32)
    present = jnp.any((ea_s[:, :, None] == experts) | (eb_s[:, :, None] == experts), axis=1)
    flat = present.reshape(-1)
    count = jnp.sum(flat.astype(I32))
    (idx,) = jnp.nonzero(flat, size=n_items, fill_value=0)
    idx = idx.astype(I32)
    pos = jnp.arange(n_items, dtype=I32)
    valid = pos < count
    last_idx = idx[jnp.maximum(count - 1, 0)]
    idx = jnp.where(valid, idx, last_idx)
    item_tile = idx // MOE_EXPERTS
    item_exp = idx % MOE_EXPERTS
    prev_tile = jnp.concatenate([jnp.full((1,), -1, I32), item_tile[:-1]])
    next_tile = jnp.concatenate([item_tile[1:], jnp.full((1,), -1, I32)])
    first = valid & (item_tile != prev_tile)
    last = valid & ((item_tile != next_tile) | (pos == count - 1))
    flags = valid.astype(I32) * FLAG_VALID + first.astype(I32) * FLAG_FIRST + last.astype(I32) * FLAG_LAST
    return order, meta_s, item_tile, item_exp, flags


def _moe_experts(h, meta, w1, w3, w2, layer):
    tokens = h.shape[0]
    tm = MOE_TILE
    assert tokens % tm == 0
    n_tiles = tokens // tm
    pair_classes = MOE_GROUPS * (MOE_PER_GROUP * (MOE_PER_GROUP - 1) // 2)
    n_items = min(MOE_EXPERTS * n_tiles, 2 * (n_tiles + pair_classes - 1))
    order, meta_s, item_tile, item_exp, flags = _moe_plan(meta, n_tiles, n_items)
    wmap = lambda g, tok, it, ie, fl: (layer, ie[g], 0, 0)
    wspec1 = pl.BlockSpec((None, 1, D_MODEL, MOE_HIDDEN), wmap)
    wspec2 = pl.BlockSpec((None, 1, MOE_HIDDEN, D_MODEL), wmap)
    return pl.pallas_call(
        functools.partial(_moe_experts_body, n_tiles=n_tiles),
        grid_spec=pltpu.PrefetchScalarGridSpec(
            num_scalar_prefetch=4,
            grid=(n_items,),
            in_specs=[
                pl.BlockSpec(memory_space=pl.ANY),
                pl.BlockSpec((tm, META_COLS), lambda g, tok, it, ie, fl: (it[g], 0)),
                wspec1, wspec1, wspec2,
            ],
            out_specs=pl.BlockSpec(memory_space=pl.ANY),
            scratch_shapes=[
                pltpu.VMEM((2, tm, D_MODEL), F32),
                pltpu.VMEM((2, tm, D_MODEL), F32),
                pltpu.SemaphoreType.DMA((2,)),
                pltpu.SemaphoreType.DMA((2,)),
            ],
        ),
        out_shape=jax.ShapeDtypeStruct((tokens, D_MODEL), F32),
        compiler_params=_cparams(("arbitrary",)),
        name="moe_experts",
    )(order, item_tile, item_exp, flags, h, meta_s, w1, w3, w2)


def _final_combine_body(x_ref, m_ref, gt_ref, fg_ref, o_ref):
    x = x_ref[...] + gt_ref[...] * m_ref[...].reshape(x_ref.shape)
    o_ref[...] = x * lax.rsqrt(jnp.mean(x * x, axis=-1, keepdims=True) + EPS) * fg_ref[...]


def _final_combine(x, moe_out, gt, final_g):
    nb, sl, _ = x.shape
    bt, lt = _row_tiling(nb, sl, ROW_TILE)
    nl = sl // lt
    xmap = lambda i: (i // nl, i % nl, 0)
    return pl.pallas_call(
        _final_combine_body,
        grid=((nb // bt) * nl,),
        in_specs=[
            pl.BlockSpec((bt, lt, D_MODEL), xmap),
            pl.BlockSpec((bt * lt, D_MODEL), lambda i: (i, 0)),
            pl.BlockSpec((bt, 1, D_MODEL), lambda i: (i // nl, 0, 0)),
            pl.BlockSpec((1, 1, D_MODEL), lambda i: (0, 0, 0)),
        ],
        out_specs=pl.BlockSpec((bt, lt, D_MODEL), xmap),
        out_shape=jax.ShapeDtypeStruct(x.shape, F32),
        compiler_params=_cparams(("parallel",)),
        name="final_combine",
    )(x, moe_out, gt, final_g.reshape(1, 1, D_MODEL))


def _rope_tables(pos):
    half = ATT_HEAD_DIM // 2
    inv_freq = ROPE_THETA ** (-jnp.arange(half, dtype=F32) / half)
    ang = pos.astype(F32)[:, None] * inv_freq[None, :]
    cos = jnp.cos(ang)
    sin = jnp.sin(ang)
    cos_h = jnp.concatenate([cos, cos], axis=1)
    sin_h = jnp.concatenate([-sin, sin], axis=1)
    return jnp.tile(cos_h, (1, ATT_HEADS)), jnp.tile(sin_h, (1, ATT_HEADS))


def _router_weights(wg, bg, we, be):
    w = jnp.concatenate([wg, we], axis=1)
    w = jnp.pad(w, ((0, 0), (0, ROUTE_COLS - w.shape[1])))
    b = jnp.pad(jnp.concatenate([bg, be]), (0, ROUTE_COLS - MOE_GROUPS - MOE_EXPERTS)).reshape(1, ROUTE_COLS)
    hi = w.astype(BF16)
    lo = (w - hi.astype(F32)).astype(BF16)
    return hi, lo, b


def _even_w_in_cols(w):
    dt0 = SSD_INNER + SSD_CONV_DIM
    dt1 = dt0 + SSD_HEADS
    zeros = jnp.zeros((w.shape[0], LANES - SSD_HEADS), w.dtype)
    return jnp.concatenate([w[:, :dt0], w[:, dt1:], w[:, dt0:dt1], zeros], axis=1).astype(BF16)


def _run_group(x, mods, pos, states, caches, wts):
    nb, sl, _ = x.shape
    ssd_in, conv_in, ret_in = states
    cos, sin = _rope_tables(pos)

    def mod_parts(i):
        m = mods[i].reshape(nb, 1, N_MOD, D_MODEL)
        return [m[:, :, j] for j in range(N_MOD)]

    sh1, sc1, gt1, sh2, sc2, gt2 = mod_parts(0)
    p = _even_in(x, wts["norm_mix_g"][0], sc1, sh1, wts["even_w_in"]).reshape(nb, sl, P_WIDTH)
    y, ssd_new, conv_new, ret_new = _even_core(
        p, cos, sin, wts["ssd_conv_w"], wts["ssd_conv_b"], wts["ssd_dt_bias"], wts["ssd_a_log"],
        wts["ssd_d"], wts["ssd_norm_g"], wts["ret_norm_g"], ssd_in, conv_in, ret_in)
    y = y.reshape(nb * sl, EVEN_OUT)
    x, h, meta = _proj_route(y, wts["even_w_out"], x, gt1, wts["norm_ffn_g"][0], sc2, sh2, *wts["router"][0])
    mix = _moe_experts(h, meta, *wts["experts"], 0)
    gt_moe = gt2

    sh1, sc1, gt1, sh2, sc2, gt2 = mod_parts(1)
    if caches is None:
        x, q, k, v = _qkv_rope(x, mix, gt_moe, wts["norm_mix_g"][1], sc1, sh1, wts["odd_w_qkv"], cos, sin)
        q3, k3, v3 = (a.reshape(nb, sl, ATT_WIDTH) for a in (q, k, v))
        o, k_t, v_t = _attn_prompt(q3, k3, v3)
        keep = min(ATT_WINDOW, sl)
        new_k, new_v = (a.transpose(0, 3, 1, 2)[:, sl - keep:] for a in (k_t, v_t))
    else:
        cos_t, sin_t = jnp.tile(cos, (nb, 1)), jnp.tile(sin, (nb, 1))
        x, q, k, v = _qkv_rope(x, mix, gt_moe, wts["norm_mix_g"][1], sc1, sh1, wts["odd_w_qkv"], cos_t, sin_t)
        q3, k3, v3 = (a.reshape(nb, sl, ATT_WIDTH) for a in (q, k, v))
        o, new_k, new_v = _attn_sample(q3, k3, v3, *caches)
    x, h, meta = _proj_route(o.reshape(nb * sl, ATT_WIDTH), wts["odd_w_out"], x, gt1,
                             wts["norm_ffn_g"][1], sc2, sh2, *wts["router"][1])
    mix = _moe_experts(h, meta, *wts["experts"], 1)
    x = _final_combine(x, mix, gt2, wts["final_norm_g"])
    return x, ssd_new[None], conv_new[None], ret_new[None], new_k[None], new_v[None]


def kernel(x_prompt, x_sample, state_ssd, state_conv, state_ret, cache_k, cache_v, c_prompt, c_sample, ada_w, ada_b, norm_mix_g, norm_ffn_g, final_norm_g, even_w_in, even_w_out, ssd_conv_w, ssd_conv_b, ssd_dt_bias, ssd_a_log, ssd_d, ssd_norm_g, ret_norm_g, odd_w_qkv, odd_w_out, moe_wg, moe_bg, moe_we, moe_be, moe_w1, moe_w3, moe_w2):
    depth = ada_w.shape[0]
    assert depth == 2 and even_w_in.shape[0] == 1 and odd_w_qkv.shape[0] == 1
    bp, sp, _ = x_prompt.shape
    bs, ss, _ = x_sample.shape

    wts = {
        "norm_mix_g": norm_mix_g, "norm_ffn_g": norm_ffn_g, "final_norm_g": final_norm_g,
        "even_w_in": _even_w_in_cols(even_w_in[0]), "even_w_out": even_w_out[0].astype(BF16),
        "ssd_conv_w": ssd_conv_w[0], "ssd_conv_b": ssd_conv_b[0], "ssd_dt_bias": ssd_dt_bias[0],
        "ssd_a_log": ssd_a_log[0], "ssd_d": ssd_d[0], "ssd_norm_g": ssd_norm_g[0], "ret_norm_g": ret_norm_g[0],
        "odd_w_qkv": odd_w_qkv[0].astype(BF16), "odd_w_out": odd_w_out[0].astype(BF16),
        "router": [_router_weights(moe_wg[i], moe_bg[i], moe_we[i], moe_be[i]) for i in range(depth)],
        "experts": (moe_w1.astype(BF16), moe_w3.astype(BF16), moe_w2.astype(BF16)),
    }
    mods = _adaln(jnp.concatenate([c_prompt, c_sample], axis=0), ada_w, ada_b)

    zeros_p = (
        jnp.zeros((bp, SSD_HEADS, SSD_HEAD_DIM, SSD_STATE), F32),
        jnp.zeros((bp, SSD_CONV - 1, SSD_CONV_DIM), F32),
        jnp.zeros((bp, RET_HEADS, RET_DV, RET_DK), F32),
    )
    out_p = _run_group(x_prompt, mods[:, :bp], jnp.arange(sp, dtype=I32), zeros_p, None, wts)
    out_s = _run_group(x_sample, mods[:, bp:], PAST_LEN + jnp.arange(ss, dtype=I32),
                       (state_ssd[0], state_conv[0], state_ret[0]), (cache_k[0], cache_v[0]), wts)
    return (out_p[0], out_s[0]) + out_p[1:] + out_s[1:]
```

```python
import functools
import math

import jax
import jax.numpy as jnp
from jax import lax
from jax.experimental import pallas as pl
from jax.experimental.pallas import tpu as pltpu

F32 = jnp.float32
BF16 = jnp.bfloat16
I32 = jnp.int32

D_MODEL = 1024
EPS = 1e-6
N_MOD = 6
SSD_HEADS = 16
SSD_HEAD_DIM = 64
SSD_INNER = SSD_HEADS * SSD_HEAD_DIM
SSD_GROUPS = 4
SSD_STATE = 64
SSD_CONV = 4
SSD_CONV_DIM = SSD_INNER + 2 * SSD_GROUPS * SSD_STATE
RET_HEADS = 8
RET_DK = 64
RET_DV = 128
RET_QK = RET_HEADS * RET_DK
RET_V = RET_HEADS * RET_DV
EVEN_OUT = SSD_INNER + RET_V
ATT_HEADS = 16
ATT_HEAD_DIM = 64
ATT_WIDTH = ATT_HEADS * ATT_HEAD_DIM
DILATED_PATTERNS = ((128, 1), (512, 4), (2048, 16))
ATT_WINDOW = 2048
PAST_LEN = 16384
ROPE_THETA = 10000.0
MOE_GROUPS = 4
MOE_PER_GROUP = 8
MOE_EXPERTS = MOE_GROUPS * MOE_PER_GROUP
MOE_HIDDEN = 256

LANES = 128
SUBLANES = 8
VMEM_LIMIT = 56 * 1024 * 1024

P_Z = 0
P_XBC = P_Z + SSD_INNER
P_Q = P_XBC + SSD_CONV_DIM
P_K = P_Q + RET_QK
P_V = P_K + RET_QK
P_G = P_V + RET_V
P_DT = P_G + RET_V
P_WIDTH = P_DT + LANES

SCAN_CHUNK = 128
ATT_BLOCK = 128
Q_SCALE = ATT_HEAD_DIM ** -0.5 * math.log2(math.e)
MASKED = -1e30
MOE_TILE = 256
ROW_TILE = 1024
QKV_ROWS = 512
EVEN_IN_ROWS = 512
EVEN_IN_COLS = P_WIDTH
ADALN_COLS = 1024


def _cparams(sem):
    return pltpu.CompilerParams(dimension_semantics=sem, vmem_limit_bytes=VMEM_LIMIT)


def _row_tiling(nb, sl, target):
    if sl >= target:
        assert sl % target == 0
        return 1, target
    bt = min(nb, target // sl)
    assert nb % bt == 0
    return bt, sl


def _silu(x):
    return x * (0.5 * jnp.tanh(0.5 * x) + 0.5)


def _norm_mod(x, g, sc, sh):
    ms = jnp.mean(x * x, axis=-1, keepdims=True)
    return x * lax.rsqrt(ms + EPS) * g * (1.0 + sc) + sh


def _split3(x):
    hi = x.astype(BF16)
    r1 = x - hi.astype(F32)
    mid = r1.astype(BF16)
    lo = (r1 - mid.astype(F32)).astype(BF16)
    return hi, mid, lo


def _dot_exact_rhs(x, m_bf16):
    hi, mid, lo = _split3(x)
    dot = functools.partial(jnp.dot, preferred_element_type=F32)
    return dot(hi, m_bf16) + dot(mid, m_bf16) + dot(lo, m_bf16)


def _dot_nt(a, b):
    return lax.dot_general(a, b, (((1,), (1,)), ((), ())), preferred_element_type=F32)


def _rope(a, cos, sin_signed):
    w = a.shape[-1]
    lane = lax.broadcasted_iota(I32, a.shape, 1)
    first = (lane & (ATT_HEAD_DIM // 2)) == 0
    rot = jnp.where(first, pltpu.roll(a, w - ATT_HEAD_DIM // 2, 1), pltpu.roll(a, ATT_HEAD_DIM // 2, 1))
    return a * cos + rot * sin_signed


def _adaln_body(c_ref, w_ref, b_ref, o_ref):
    a = _silu(c_ref[...]).astype(BF16)
    o_ref[0] = jnp.dot(a, w_ref[0].astype(BF16), preferred_element_type=F32) + b_ref[0]


def _adaln(c_all, ada_w, ada_b):
    nb = c_all.shape[0]
    depth, _, n6 = ada_w.shape
    tn = ADALN_COLS
    return pl.pallas_call(
        _adaln_body,
        grid=(depth, n6 // tn),
        in_specs=[
            pl.BlockSpec((nb, D_MODEL), lambda i, j: (0, 0)),
            pl.BlockSpec((1, D_MODEL, tn), lambda i, j: (i, 0, j)),
            pl.BlockSpec((1, 1, tn), lambda i, j: (i, 0, j)),
        ],
        out_specs=pl.BlockSpec((1, nb, tn), lambda i, j: (i, 0, j)),
        out_shape=jax.ShapeDtypeStruct((depth, nb, n6), F32),
        compiler_params=_cparams(("parallel", "parallel")),
        name="adaln",
    )(c_all, ada_w, ada_b.reshape(depth, 1, n6))


def _even_in_body(x_ref, g_ref, sc_ref, sh_ref, w_ref, o_ref, h_scr):
    @pl.when(pl.program_id(1) == 0)
    def _():
        h = _norm_mod(x_ref[...], g_ref[...], sc_ref[...], sh_ref[...])
        h_scr[...] = h.reshape(h_scr.shape).astype(BF16)

    o_ref[...] = jnp.dot(h_scr[...], w_ref[...], preferred_element_type=F32)


def _even_in(x, g, sc, sh, w):
    nb, sl, _ = x.shape
    n = w.shape[1]
    bt, lt = _row_tiling(nb, sl, EVEN_IN_ROWS)
    nl = sl // lt
    tm = bt * lt
    tn = EVEN_IN_COLS
    assert n % tn == 0
    return pl.pallas_call(
        _even_in_body,
        grid=((nb // bt) * nl, n // tn),
        in_specs=[
            pl.BlockSpec((bt, lt, D_MODEL), lambda i, j: (i // nl, i % nl, 0)),
            pl.BlockSpec((1, 1, D_MODEL), lambda i, j: (0, 0, 0)),
            pl.BlockSpec((bt, 1, D_MODEL), lambda i, j: (i // nl, 0, 0)),
            pl.BlockSpec((bt, 1, D_MODEL), lambda i, j: (i // nl, 0, 0)),
            pl.BlockSpec((D_MODEL, tn), lambda i, j: (0, j), pipeline_mode=pl.Buffered(1 if tn == n else 2)),
        ],
        out_specs=pl.BlockSpec((tm, tn), lambda i, j: (i, j)),
        out_shape=jax.ShapeDtypeStruct((nb * sl, n), F32),
        scratch_shapes=[pltpu.VMEM((tm, D_MODEL), BF16)],
        compiler_params=_cparams(("parallel", "arbitrary")),
        name="even_in",
    )(x, g.reshape(1, 1, D_MODEL), sc, sh, w)


def _ret_log_gamma(h):
    return math.log1p(-(2.0 ** (-5.0 - h)))


def _ret_log_gamma_exp(h, steps):
    return math.exp(_ret_log_gamma(h) * steps)


def _ret_tables(valid_len):
    cs = SCAN_CHUNK
    lg = jnp.asarray([_ret_log_gamma(h) for h in range(RET_HEADS)], F32)[:, None, None]
    steps = jnp.minimum(jnp.arange(cs) + 1, valid_len).astype(F32)
    gap = steps[:, None] - steps[None, :]
    causal = jnp.arange(cs)[None, :] <= jnp.arange(cs)[:, None]
    decay = jnp.exp(jnp.where(causal, lg * gap, -jnp.inf))
    wide = jnp.broadcast_to(steps[:, None], (cs, LANES))
    return decay, jnp.exp(lg * wide), jnp.exp(lg * (float(valid_len) - wide))


def _even_core_body(p_ref, cos_ref, sin_ref, cw_ref, cb_ref, dtb_ref, alog_ref, dskip_ref, ssdg_ref, retg_ref,
                    rdec_ref, rgrow_ref, rend_ref, ssd0_ref, conv0_ref, ret0_ref,
                    y_ref, ssd_out_ref, conv_out_ref, ret_out_ref,
                    xpad_scr, ssd_scr, ret_scr, yacc_scr, *, valid_len):
    cs = SCAN_CHUNK
    c = pl.program_id(1)
    nc = pl.num_programs(1)
    hd = SSD_HEAD_DIM

    @pl.when(c == 0)
    def _():
        ssd_scr[...] = ssd0_ref[0]
        ret_scr[...] = ret0_ref[0]
        xpad_scr[0:SUBLANES, :] = conv0_ref[0]

    row = lax.broadcasted_iota(I32, (cs, LANES), 0)
    col = lax.broadcasted_iota(I32, (cs, cs), 1)
    rowc = lax.broadcasted_iota(I32, (cs, cs), 0)
    causal = col <= rowc
    live = row < valid_len

    def proj(lo, hi):
        val = p_ref[0, :, lo:hi]
        if valid_len < cs:
            val = jnp.concatenate([val, jnp.zeros((cs - valid_len, hi - lo), F32)], axis=0)
        return val

    def emit(lo, hi, val):
        y_ref[0, :, lo:hi] = val[0:valid_len].astype(BF16)

    xpad_scr[SUBLANES:SUBLANES + cs, :] = proj(P_XBC, P_XBC + SSD_CONV_DIM)
    conv = cb_ref[...] + xpad_scr[SUBLANES:SUBLANES + cs, :] * cw_ref[SSD_CONV - 1:SSD_CONV, :]
    for back in range(1, SSD_CONV):
        tap = SSD_CONV - 1 - back
        conv = conv + xpad_scr[SUBLANES - back:SUBLANES - back + cs, :] * cw_ref[tap:tap + 1, :]
    xc = _silu(conv)

    @pl.when(c == nc - 1)
    def _():
        conv_out_ref[0] = xpad_scr[valid_len:valid_len + SUBLANES, :]

    xpad_scr[0:SUBLANES, :] = xpad_scr[cs:cs + SUBLANES, :]

    xh = xc[:, 0:SSD_INNER]
    if valid_len < cs:
        xh = jnp.where(live[:, 0:1], xh, 0.0)

    dt_in = proj(P_DT, P_DT + LANES).T[0:SSD_HEADS, :] + dtb_ref[...]
    dt = jnp.maximum(dt_in, 0.0) + jnp.log1p(jnp.exp(-jnp.abs(dt_in)))
    la = dt * (-jnp.exp(alog_ref[...]))
    if valid_len < cs:
        la = jnp.where(lax.broadcasted_iota(I32, (SSD_HEADS, cs), 1) < valid_len, la, 0.0)
    acum = _dot_exact_rhs(la, (rowc <= col).astype(BF16))
    acum_last = acum[:, cs - 1:cs]
    w_state = dt * jnp.exp(acum_last - acum)
    cdecay = jnp.exp(acum_last)
    acum_c = jnp.concatenate([acum, jnp.zeros((LANES - SSD_HEADS, cs), F32)], axis=0).T
    eacum_c = jnp.exp(acum_c)

    xh_b = xh.astype(BF16)
    xh_t = xh.T

    for grp in range(SSD_GROUPS):
        b_g = xc[:, SSD_INNER + grp * SSD_STATE:SSD_INNER + (grp + 1) * SSD_STATE].astype(BF16)
        c_g = xc[:, SSD_INNER + (SSD_GROUPS + grp) * SSD_STATE:SSD_INNER + (SSD_GROUPS + grp + 1) * SSD_STATE].astype(BF16)
        s_g = _dot_nt(c_g, b_g)
        for hh in range(SSD_HEADS // SSD_GROUPS):
            h = grp * (SSD_HEADS // SSD_GROUPS) + hh
            lo, hi = h * hd, (h + 1) * hd
            decay = jnp.exp(jnp.where(causal, acum_c[:, h:h + 1] - acum[h:h + 1, :], -jnp.inf))
            m = (s_g * decay * dt[h:h + 1, :]).astype(BF16)
            y_h = jnp.dot(m, xh_b[:, lo:hi], preferred_element_type=F32)
            h_prev = ssd_scr[h]
            y_h = y_h + _dot_nt(c_g, h_prev.astype(BF16)) * eacum_c[:, h:h + 1]
            yacc_scr[:, lo:hi] = y_h
            xw_t = (xh_t[lo:hi, :] * w_state[h:h + 1, :]).astype(BF16)
            ssd_scr[h] = h_prev * cdecay[h:h + 1, :] + jnp.dot(xw_t, b_g, preferred_element_type=F32)

    y = yacc_scr[...] + dskip_ref[...] * xh
    y = y * _silu(proj(P_Z, P_Z + SSD_INNER))
    y = y * lax.rsqrt(jnp.mean(y * y, axis=-1, keepdims=True) + EPS) * ssdg_ref[...]
    emit(0, SSD_INNER, y)

    cos = cos_ref[:, 0:RET_QK]
    sin = sin_ref[:, 0:RET_QK]
    rq = _rope(proj(P_Q, P_Q + RET_QK), cos, sin).astype(BF16)
    rk = (_rope(proj(P_K, P_K + RET_QK), cos, sin) * RET_DK ** -0.5).astype(BF16)
    rv = proj(P_V, P_V + RET_V)
    if valid_len < cs:
        rv = jnp.where(live[:, 0:1], rv, 0.0)
    for h in range(RET_HEADS):
        q_h = rq[:, h * RET_DK:(h + 1) * RET_DK]
        k_h = rk[:, h * RET_DK:(h + 1) * RET_DK]
        v_h = rv[:, h * RET_DV:(h + 1) * RET_DV]
        m = (_dot_nt(q_h, k_h) * rdec_ref[h]).astype(BF16)
        o_h = jnp.dot(m, v_h.astype(BF16), preferred_element_type=F32)
        s_prev = ret_scr[h]
        o_h = o_h + _dot_nt(q_h, s_prev.astype(BF16)) * rgrow_ref[h]
        v_te_t = (v_h * rend_ref[h]).T.astype(BF16)
        ret_scr[h] = s_prev * _ret_log_gamma_exp(h, valid_len) + jnp.dot(v_te_t, k_h, preferred_element_type=F32)
        o_h = o_h * lax.rsqrt(jnp.mean(o_h * o_h, axis=-1, keepdims=True) + EPS) * retg_ref[:, h * RET_DV:(h + 1) * RET_DV]
        o_h = o_h * _silu(proj(P_G + h * RET_DV, P_G + (h + 1) * RET_DV))
        emit(SSD_INNER + h * RET_DV, SSD_INNER + (h + 1) * RET_DV, o_h)

    @pl.when(c == nc - 1)
    def _():
        ssd_out_ref[0] = ssd_scr[...]
        ret_out_ref[0] = ret_scr[...]


def _even_core(p, cos, sin, conv_w, conv_b, dt_bias, a_log, d_skip, ssd_norm_g, ret_norm_g, ssd0, conv0, ret0):
    nb, sl, _ = p.shape
    cs = SCAN_CHUNK
    if sl % cs == 0:
        nc, valid_len = sl // cs, cs
    else:
        assert sl < cs and sl % SUBLANES == 0
        nc, valid_len = 1, sl
        grow = ((0, cs - sl), (0, 0))
        cos, sin = jnp.pad(cos, grow), jnp.pad(sin, grow)
    dtb = jnp.broadcast_to(dt_bias[:, None], (SSD_HEADS, cs))
    alog = jnp.broadcast_to(a_log[:, None], (SSD_HEADS, cs))
    dskip_x = jnp.repeat(d_skip, SSD_HEAD_DIM).reshape(1, SSD_INNER)
    conv0p = jnp.pad(conv0, ((0, 0), (SUBLANES - (SSD_CONV - 1), 0), (0, 0)))
    assert RET_DV == LANES
    const2 = lambda b, c: (0, 0)
    const3 = lambda b, c: (0, 0, 0)
    st4 = lambda b, c: (b, 0, 0, 0)
    y, ssd_new, conv_new, ret_new = pl.pallas_call(
        functools.partial(_even_core_body, valid_len=valid_len),
        grid=(nb, nc),
        in_specs=[
            pl.BlockSpec((1, valid_len, P_WIDTH), lambda b, c: (b, c, 0)),
            pl.BlockSpec((cs, ATT_WIDTH), lambda b, c: (c, 0)),
            pl.BlockSpec((cs, ATT_WIDTH), lambda b, c: (c, 0)),
            pl.BlockSpec((SSD_CONV, SSD_CONV_DIM), const2),
            pl.BlockSpec((1, SSD_CONV_DIM), const2),
            pl.BlockSpec((SSD_HEADS, cs), const2),
            pl.BlockSpec((SSD_HEADS, cs), const2),
            pl.BlockSpec((1, SSD_INNER), const2),
            pl.BlockSpec((1, SSD_INNER), const2),
            pl.BlockSpec((1, RET_V), const2),
            pl.BlockSpec((RET_HEADS, cs, cs), const3),
            pl.BlockSpec((RET_HEADS, cs, LANES), const3),
            pl.BlockSpec((RET_HEADS, cs, LANES), const3),
            pl.BlockSpec((1, SSD_HEADS, SSD_HEAD_DIM, SSD_STATE), st4),
            pl.BlockSpec((1, SUBLANES, SSD_CONV_DIM), lambda b, c: (b, 0, 0)),
            pl.BlockSpec((1, RET_HEADS, RET_DV, RET_DK), st4),
        ],
        out_specs=[
            pl.BlockSpec((1, valid_len, EVEN_OUT), lambda b, c: (b, c, 0)),
            pl.BlockSpec((1, SSD_HEADS, SSD_HEAD_DIM, SSD_STATE), st4),
            pl.BlockSpec((1, SUBLANES, SSD_CONV_DIM), lambda b, c: (b, 0, 0)),
            pl.BlockSpec((1, RET_HEADS, RET_DV, RET_DK), st4),
        ],
        out_shape=[
            jax.ShapeDtypeStruct((nb, sl, EVEN_OUT), BF16),
            jax.ShapeDtypeStruct((nb, SSD_HEADS, SSD_HEAD_DIM, SSD_STATE), F32),
            jax.ShapeDtypeStruct((nb, SUBLANES, SSD_CONV_DIM), F32),
            jax.ShapeDtypeStruct((nb, RET_HEADS, RET_DV, RET_DK), F32),
        ],
        scratch_shapes=[
            pltpu.VMEM((cs + SUBLANES, SSD_CONV_DIM), F32),
            pltpu.VMEM((SSD_HEADS, SSD_HEAD_DIM, SSD_STATE), F32),
            pltpu.VMEM((RET_HEADS, RET_DV, RET_DK), F32),
            pltpu.VMEM((cs, SSD_INNER), F32),
        ],
        compiler_params=_cparams(("parallel", "arbitrary")),
        name="even_core",
    )(p, cos, sin, conv_w, conv_b.reshape(1, SSD_CONV_DIM), dtb, alog, dskip_x,
      ssd_norm_g.reshape(1, SSD_INNER), ret_norm_g.reshape(1, RET_V), *_ret_tables(valid_len), ssd0, conv0p, ret0)
    return y, ssd_new, conv_new[:, SUBLANES - (SSD_CONV - 1):], ret_new


def _qkv_body(x_ref, m_ref, gt_ref, g_ref, sc_ref, sh_ref, w_ref, cos_ref, sin_ref,
              xo_ref, q_ref, k_ref, v_ref):
    x = x_ref[...] + gt_ref[...] * m_ref[...].reshape(x_ref.shape)
    xo_ref[...] = x
    h = _norm_mod(x, g_ref[...], sc_ref[...], sh_ref[...])
    h = h.reshape(q_ref.shape).astype(BF16)
    cos = cos_ref[...]
    sin = sin_ref[...]
    dot = functools.partial(jnp.dot, preferred_element_type=F32)
    q_ref[...] = _rope(dot(h, w_ref[:, 0:ATT_WIDTH]), cos, sin) * Q_SCALE
    k_ref[...] = _rope(dot(h, w_ref[:, ATT_WIDTH:2 * ATT_WIDTH]), cos, sin)
    v_ref[...] = dot(h, w_ref[:, 2 * ATT_WIDTH:3 * ATT_WIDTH])


def _qkv_rope(x, mix, gt, g, sc, sh, w, cos, sin):
    nb, sl, _ = x.shape
    bt, lt = _row_tiling(nb, sl, QKV_ROWS)
    nl = sl // lt
    tm = bt * lt
    assert cos.shape[0] in (sl, nb * sl) and (cos.shape[0] == sl) == (bt == 1)
    ntab = cos.shape[0] // tm
    out = jax.ShapeDtypeStruct((nb * sl, ATT_WIDTH), F32)
    ospec = pl.BlockSpec((tm, ATT_WIDTH), lambda i: (i, 0))
    tspec = pl.BlockSpec((tm, ATT_WIDTH), lambda i: (i % ntab, 0))
    xspec = pl.BlockSpec((bt, lt, D_MODEL), lambda i: (i // nl, i % nl, 0))
    mspec = pl.BlockSpec((bt, 1, D_MODEL), lambda i: (i // nl, 0, 0))
    out_specs = [xspec, ospec, ospec, ospec]
    out_shape = [jax.ShapeDtypeStruct(x.shape, F32), out, out, out]
    return pl.pallas_call(
        _qkv_body,
        grid=((nb // bt) * nl,),
        in_specs=[
            xspec, ospec, mspec,
            pl.BlockSpec((1, 1, D_MODEL), lambda i: (0, 0, 0)),
            mspec, mspec,
            pl.BlockSpec((D_MODEL, 3 * ATT_WIDTH), lambda i: (0, 0)),
            tspec, tspec,
        ],
        out_specs=out_specs,
        out_shape=out_shape,
        compiler_params=_cparams(("parallel",)),
        name="qkv_rope",
    )(x, mix, gt, g.reshape(1, 1, D_MODEL), sc, sh, w, cos, sin)


def _attn_prompt_body(q_ref, k_ref, v_ref, o_ref, kt_ref, vt_ref, ob_scr, lse_scr, p_scr, m_scr, *, seq):
    blk = ATT_BLOCK
    half = ATT_HEAD_DIM
    lane = lax.broadcasted_iota(I32, (blk, LANES), 1)
    head_a = lane < half
    rowi = lax.broadcasted_iota(I32, (2 * blk, blk), 0) & (blk - 1)
    coli = lax.broadcasted_iota(I32, (2 * blk, blk), 1)
    mask_cur = coli <= rowi
    mask_prev = coli >= rowi
    ones = jnp.ones((blk, LANES), BF16)

    for i in range(seq // blk):
        rows = slice(i * blk, (i + 1) * blk)
        kt_ref[0, :, :, rows] = k_ref[0, rows, :].T.reshape(2, half, blk)
        vt_ref[0, :, :, rows] = v_ref[0, rows, :].T.reshape(2, half, blk)

    def ld(ref, dil, s0):
        if dil == 1:
            return ref[0, pl.ds(s0, blk), :]
        return ref[0, pl.ds(s0, blk, stride=dil), :]

    def probs(dil, start, pstart, slot):
        q = ld(q_ref, dil, start)
        q2 = jnp.concatenate([jnp.where(head_a, q, 0.0), jnp.where(head_a, 0.0, q)], axis=0).astype(BF16)
        sc = jnp.where(mask_cur, _dot_nt(q2, ld(k_ref, dil, start).astype(BF16)), MASKED)
        if pstart is None:
            m = jnp.max(sc, axis=1, keepdims=True)
        else:
            sp = jnp.where(mask_prev, _dot_nt(q2, ld(k_ref, dil, pstart).astype(BF16)), MASKED)
            m = jnp.max(jnp.maximum(sc, sp), axis=1, keepdims=True)
            p_scr[slot, :, blk:2 * blk] = jnp.exp2(sp - m).astype(BF16)
        p_scr[slot, :, 0:blk] = jnp.exp2(sc - m).astype(BF16)
        m_scr[slot] = jnp.where(head_a, m[0:blk], m[blk:2 * blk])

    def weigh(gi, dil, start, pstart, slot):
        ve = jnp.concatenate([ld(v_ref, dil, start).astype(BF16), ones], axis=1)
        if pstart is None:
            oe = jnp.dot(p_scr[slot, :, 0:blk], ve, preferred_element_type=F32)
        else:
            vpe = jnp.concatenate([ld(v_ref, dil, pstart).astype(BF16), ones], axis=1)
            oe = jnp.dot(p_scr[slot], jnp.concatenate([ve, vpe], axis=0), preferred_element_type=F32)
        num = jnp.where(head_a, oe[0:blk, 0:LANES], oe[blk:2 * blk, 0:LANES])
        den = jnp.where(head_a, oe[0:blk, LANES:2 * LANES], oe[blk:2 * blk, LANES:2 * LANES])
        rows = pl.ds(start, blk) if dil == 1 else pl.ds(start, blk, stride=dil)
        ob_scr[gi, rows, :] = num / den
        lse_scr[gi, rows, :] = m_scr[slot] + jnp.log2(den)

    def pipelined(gi, dil, count, coords):
        s1 = lambda t, slot: probs(dil, *coords(t), slot)
        s2 = lambda t, slot: weigh(gi, dil, *coords(t), slot)
        s1(0, 0)
        if count == 1:
            s2(0, 0)
            return
        s1(1, 1)
        trips = (count - 2) // 2

        def body(i, carry):
            a = 2 * i
            s2(a, 0)
            s2(a + 1, 1)
            s1(a + 2, 0)
            s1(a + 3, 1)
            return carry

        if trips > 0:
            lax.fori_loop(0, trips, body, 0)
        s2(2 * trips, 0)
        if (count - 2) % 2 == 1:
            s1(count - 1, 0)
        s2(2 * trips + 1, 1)
        if (count - 2) % 2 == 1:
            s2(count - 1, 0)

    for gi, (window, dil) in enumerate(DILATED_PATTERNS):
        assert window // dil == blk and (seq // dil) % blk == 0
        per_residue = seq // dil // blk
        span = blk * dil
        pipelined(gi, dil, dil, lambda t: (t, None))
        if per_residue > 1:
            def coords(t, dil=dil, span=span):
                start = t % dil + (1 + t // dil) * span
                return start, start - span
            pipelined(gi, dil, dil * (per_residue - 1), coords)

    def merge(i, carry):
        rows = pl.ds(pl.multiple_of(i * blk, blk), blk)
        l0 = lse_scr[0, rows, :]
        l1 = lse_scr[1, rows, :]
        l2 = lse_scr[2, rows, :]
        m = jnp.maximum(jnp.maximum(l0, l1), l2)
        w0 = jnp.exp2(l0 - m)
        w1 = jnp.exp2(l1 - m)
        w2 = jnp.exp2(l2 - m)
        o = (w0 * ob_scr[0, rows, :] + w1 * ob_scr[1, rows, :] + w2 * ob_scr[2, rows, :]) / (w0 + w1 + w2)
        o_ref[0, rows, :] = o.astype(BF16)
        return carry

    lax.fori_loop(0, seq // blk, merge, 0)


def _attn_prompt(q, k, v):
    nb, seq, _ = q.shape
    npair = ATT_WIDTH // LANES
    spec = pl.BlockSpec((1, seq, LANES), lambda b, h: (b, 0, h))
    tspec = pl.BlockSpec((1, 2, ATT_HEAD_DIM, seq), lambda b, h: (b, h, 0, 0))
    tshape = jax.ShapeDtypeStruct((nb, ATT_HEADS, ATT_HEAD_DIM, seq), F32)
    return pl.pallas_call(
        functools.partial(_attn_prompt_body, seq=seq),
        grid=(nb, npair),
        in_specs=[spec, spec, spec],
        out_specs=[spec, tspec, tspec],
        out_shape=[jax.ShapeDtypeStruct((nb, seq, ATT_WIDTH), BF16), tshape, tshape],
        scratch_shapes=[
            pltpu.VMEM((len(DILATED_PATTERNS), seq, LANES), F32),
            pltpu.VMEM((len(DILATED_PATTERNS), seq, LANES), F32),
            pltpu.VMEM((2, 2 * ATT_BLOCK, 2 * ATT_BLOCK), BF16),
            pltpu.VMEM((2, ATT_BLOCK, LANES), F32),
        ],
        compiler_params=_cparams(("parallel", "parallel")),
        name="attn_prompt",
    )(q, k, v)


SAMPLE_HEADS_PER_STEP = 8
SAMPLE_ROWS = 2 * SUBLANES


def _attn_sample_body(q_ref, kn_ref, vn_ref, ck_ref, cv_ref, o_ref, nk_ref, nv_ref, *, past, new):
    rows = SAMPLE_ROWS
    keys = past + LANES
    lane = lax.broadcasted_iota(I32, (ATT_HEAD_DIM, LANES), 1)
    is_new = lane >= LANES - new
    t_idx = lax.broadcasted_iota(I32, (rows, keys), 0) & (new - 1)
    k_idx = lax.broadcasted_iota(I32, (rows, keys), 1)
    dist = past + t_idx - k_idx
    oks = [(dist >= 0) & (dist <= window) & ((dist & (dil - 1)) == 0) for window, dil in DILATED_PATTERNS]
    zpad = jnp.zeros((ATT_HEAD_DIM, LANES - new), F32)

    for h in range(SAMPLE_HEADS_PER_STEP):
        alls = []
        for c_ref, n_ref, out_ref in ((ck_ref, kn_ref, nk_ref), (cv_ref, vn_ref, nv_ref)):
            old = c_ref[0, h]
            fresh = n_ref[0, h]
            rolled = pltpu.roll(old, past - new, 1)
            out_ref[0, h] = rolled
            out_ref[0, h, :, past - LANES:past] = jnp.where(
                is_new, jnp.concatenate([zpad, fresh], axis=1), rolled[:, past - LANES:past])
            alls.append(jnp.concatenate([old, fresh, zpad], axis=1).astype(BF16))
        k_all, v_all = alls
        q = jnp.concatenate([q_ref[0, h], jnp.zeros((rows - new, ATT_HEAD_DIM), F32)], axis=0).astype(BF16)
        s = jnp.dot(q, k_all, preferred_element_type=F32)
        ps, dens, lses = [], [], []
        for ok in oks:
            sg = jnp.where(ok, s, MASKED)
            m = jnp.max(sg, axis=1, keepdims=True)
            p = jnp.exp2(sg - m)
            den = jnp.sum(p, axis=1, keepdims=True)
            ps.append(p.astype(BF16))
            dens.append(den)
            lses.append(m + jnp.log2(den))
        o_all = _dot_nt(jnp.concatenate(ps, axis=0), v_all)
        m = jnp.maximum(jnp.maximum(lses[0], lses[1]), lses[2])
        ws = [jnp.exp2(l - m) for l in lses]
        o = sum(w * o_all[i * rows:(i + 1) * rows] / d for i, (w, d) in enumerate(zip(ws, dens)))
        o = o / (ws[0] + ws[1] + ws[2])
        o_ref[0, h] = o[0:new].astype(BF16)


def _attn_sample(q, kn, vn, cache_k, cache_v):
    nb, new, _ = q.shape
    past = cache_k.shape[1]
    hps = SAMPLE_HEADS_PER_STEP
    assert past >= ATT_WINDOW and past % LANES == 0 and new == SUBLANES and ATT_HEADS % hps == 0
    heads = (nb, new, ATT_HEADS, ATT_HEAD_DIM)
    q4 = q.reshape(heads).transpose(0, 2, 1, 3)
    kn4 = kn.reshape(heads).transpose(0, 2, 3, 1)
    vn4 = vn.reshape(heads).transpose(0, 2, 3, 1)
    ck = cache_k.transpose(0, 2, 3, 1)
    cv = cache_v.transpose(0, 2, 3, 1)
    hmap = lambda b, g: (b, g, 0, 0)
    qspec = pl.BlockSpec((1, hps, new, ATT_HEAD_DIM), hmap)
    nspec = pl.BlockSpec((1, hps, ATT_HEAD_DIM, new), hmap)
    cspec = pl.BlockSpec((1, hps, ATT_HEAD_DIM, past), hmap)
    o4, nk, nv = pl.pallas_call(
        functools.partial(_attn_sample_body, past=past, new=new),
        grid=(nb, ATT_HEADS // hps),
        in_specs=[qspec, nspec, nspec, cspec, cspec],
        out_specs=[qspec, cspec, cspec],
        out_shape=[
            jax.ShapeDtypeStruct((nb, ATT_HEADS, new, ATT_HEAD_DIM), BF16),
            jax.ShapeDtypeStruct(ck.shape, F32),
            jax.ShapeDtypeStruct(cv.shape, F32),
        ],
        compiler_params=_cparams(("parallel", "parallel")),
        name="attn_sample",
    )(q4, kn4, vn4, ck, cv)
    o = o4.transpose(0, 2, 1, 3).reshape(nb, new, ATT_WIDTH)
    return o, nk.transpose(0, 3, 1, 2), nv.transpose(0, 3, 1, 2)


ROUTE_COLS = LANES
META_COLS = SUBLANES


def _proj_route_body(y_ref, w_ref, x_ref, gt_ref, g_ref, sc_ref, sh_ref, whi_ref, wlo_ref, b_ref,
                     xo_ref, h_ref, meta_ref):
    tm = h_ref.shape[0]
    mix = jnp.dot(y_ref[...], w_ref[...], preferred_element_type=F32)
    x = x_ref[...] + gt_ref[...] * mix.reshape(x_ref.shape)
    xo_ref[...] = x
    h = _norm_mod(x, g_ref[...], sc_ref[...], sh_ref[...]).reshape(tm, D_MODEL)
    h_ref[...] = h
    hi = h.astype(BF16)
    lo = (h - hi.astype(F32)).astype(BF16)
    dot = functools.partial(jnp.dot, preferred_element_type=F32)
    logits = dot(hi, whi_ref[...]) + dot(lo, whi_ref[...]) + dot(hi, wlo_ref[...]) + b_ref[...]
    lane = lax.broadcasted_iota(I32, (tm, ROUTE_COLS), 1).astype(F32)
    big = float(ROUTE_COLS)
    neg = -jnp.inf
    gl = jnp.where(lane < MOE_GROUPS, logits, neg)
    gmax = jnp.max(gl, axis=1, keepdims=True)
    g_idx = jnp.min(jnp.where(gl == gmax, lane, big), axis=1, keepdims=True)
    g_w = 1.0 / jnp.sum(jnp.exp(gl - gmax), axis=1, keepdims=True)
    first = MOE_GROUPS + MOE_PER_GROUP * g_idx
    el = jnp.where((lane >= first) & (lane < first + MOE_PER_GROUP), logits, neg)
    v1 = jnp.max(el, axis=1, keepdims=True)
    i1 = jnp.min(jnp.where(el == v1, lane, big), axis=1, keepdims=True)
    el2 = jnp.where(lane == i1, neg, el)
    v2 = jnp.max(el2, axis=1, keepdims=True)
    i2 = jnp.min(jnp.where(el2 == v2, lane, big), axis=1, keepdims=True)
    t = jnp.exp(v2 - v1)
    w1 = g_w / (1.0 + t)
    w2 = g_w * t / (1.0 + t)
    meta = jnp.where(lane == 0, i1 - MOE_GROUPS,
                     jnp.where(lane == 1, i2 - MOE_GROUPS, jnp.where(lane == 2, w1, jnp.where(lane == 3, w2, 0.0))))
    meta_ref[...] = meta[:, 0:META_COLS]


def _proj_route(y, w, x, gt, g, sc, sh, w_hi, w_lo, bias):
    nb, sl, _ = x.shape
    kdim = y.shape[1]
    bt, lt = _row_tiling(nb, sl, ROW_TILE)
    nl = sl // lt
    tm = bt * lt
    c2 = lambda i: (0, 0)
    xspec = pl.BlockSpec((bt, lt, D_MODEL), lambda i: (i // nl, i % nl, 0))
    mspec = pl.BlockSpec((bt, 1, D_MODEL), lambda i: (i // nl, 0, 0))
    return pl.pallas_call(
        _proj_route_body,
        grid=((nb // bt) * nl,),
        in_specs=[
            pl.BlockSpec((tm, kdim), lambda i: (i, 0)),
            pl.BlockSpec((kdim, D_MODEL), c2),
            xspec, mspec,
            pl.BlockSpec((1, 1, D_MODEL), lambda i: (0, 0, 0)),
            mspec, mspec,
            pl.BlockSpec((D_MODEL, ROUTE_COLS), c2),
            pl.BlockSpec((D_MODEL, ROUTE_COLS), c2),
            pl.BlockSpec((1, ROUTE_COLS), c2),
        ],
        out_specs=[xspec, pl.BlockSpec((tm, D_MODEL), lambda i: (i, 0)), pl.BlockSpec((tm, META_COLS), lambda i: (i, 0))],
        out_shape=[jax.ShapeDtypeStruct(x.shape, F32), jax.ShapeDtypeStruct((nb * sl, D_MODEL), F32),
                   jax.ShapeDtypeStruct((nb * sl, META_COLS), F32)],
        compiler_params=_cparams(("parallel",)),
        name="proj_route",
    )(y, w, x, gt, g.reshape(1, 1, D_MODEL), sc, sh, w_hi, w_lo, bias)


FLAG_VALID, FLAG_FIRST, FLAG_LAST = 1, 2, 4


def _moe_experts_body(tok_ref, item_tile_ref, item_exp_ref, item_flag_ref,
                      h_hbm, meta_ref, w1_ref, w3_ref, w2_ref, out_hbm,
                      xbuf, acc, gsem, ssem, *, n_tiles):
    tm = MOE_TILE
    g = pl.program_id(0)
    n_items = pl.num_programs(0)
    tile = item_tile_ref[g]
    e = item_exp_ref[g]
    flag = item_flag_ref[g]
    slot = tile % 2

    def gather_start(t, s):
        for r in range(tm):
            tok = tok_ref[t * tm + r]
            pltpu.make_async_copy(h_hbm.at[pl.ds(tok, 1)], xbuf.at[s, pl.ds(r, 1)], gsem.at[s]).start()

    def gather_wait(s):
        pltpu.make_async_copy(h_hbm.at[pl.ds(0, tm)], xbuf.at[s], gsem.at[s]).wait()

    def scatter_start(t, s):
        for r in range(tm):
            tok = tok_ref[t * tm + r]
            pltpu.make_async_copy(acc.at[s, pl.ds(r, 1)], out_hbm.at[pl.ds(tok, 1)], ssem.at[s]).start()

    def scatter_wait(s):
        pltpu.make_async_copy(acc.at[s], out_hbm.at[pl.ds(0, tm)], ssem.at[s]).wait()

    def per_slot(fn):
        for s in range(2):
            pl.when(slot == s)(functools.partial(fn, s))

    @pl.when(g == 0)
    def _():
        gather_start(0, 0)

    @pl.when((flag & FLAG_FIRST) != 0)
    def _():
        @pl.when(tile >= 2)
        def _():
            scatter_wait(slot)

        gather_wait(slot)

        @pl.when(tile + 1 < n_tiles)
        def _():
            per_slot(lambda s: gather_start(tile + 1, 1 - s))

    def expert_out():
        x = xbuf[slot].astype(BF16)
        a = jnp.dot(x, w1_ref[0], preferred_element_type=F32)
        u = jnp.dot(x, w3_ref[0], preferred_element_type=F32)
        ef = e.astype(F32)
        gate = (jnp.where(meta_ref[:, 0:1] == ef, meta_ref[:, 2:3], 0.0)
                + jnp.where(meta_ref[:, 1:2] == ef, meta_ref[:, 3:4], 0.0))
        hm = (_silu(a) * u * gate).astype(BF16)
        return jnp.dot(hm, w2_ref[0], preferred_element_type=F32)

    @pl.when((flag & FLAG_FIRST) != 0)
    def _():
        acc[slot] = expert_out()

    @pl.when((flag & (FLAG_VALID | FLAG_FIRST)) == FLAG_VALID)
    def _():
        acc[slot] += expert_out()

    @pl.when((flag & FLAG_LAST) != 0)
    def _():
        per_slot(lambda s: scatter_start(tile, s))

    @pl.when(g == n_items - 1)
    def _():
        if n_tiles >= 2:
            scatter_wait((n_tiles - 2) % 2)
        scatter_wait((n_tiles - 1) % 2)


def _moe_plan(meta, n_tiles, n_items):
    tm = MOE_TILE
    ea = meta[:, 0].astype(I32)
    eb = meta[:, 1].astype(I32)
    key = jnp.minimum(ea, eb) * MOE_EXPERTS + jnp.maximum(ea, eb)
    order = jnp.argsort(key).astype(I32)
    meta_s = meta[order]
    ea_s = ea[order].reshape(n_tiles, tm)
    eb_s = eb[order].reshape(n_tiles, tm)
    experts = jnp.arange(MOE_EXPERTS, dtype=I32)
    present = jnp.any((ea_s[:, :, None] == experts) | (eb_s[:, :, None] == experts), axis=1)
    flat = present.reshape(-1)
    count = jnp.sum(flat.astype(I32))
    (idx,) = jnp.nonzero(flat, size=n_items, fill_value=0)
    idx = idx.astype(I32)
    pos = jnp.arange(n_items, dtype=I32)
    valid = pos < count
    last_idx = idx[jnp.maximum(count - 1, 0)]
    idx = jnp.where(valid, idx, last_idx)
    item_tile = idx // MOE_EXPERTS
    item_exp = idx % MOE_EXPERTS
    prev_tile = jnp.concatenate([jnp.full((1,), -1, I32), item_tile[:-1]])
    next_tile = jnp.concatenate([item_tile[1:], jnp.full((1,), -1, I32)])
    first = valid & (item_tile != prev_tile)
    last = valid & ((item_tile != next_tile) | (pos == count - 1))
    flags = valid.astype(I32) * FLAG_VALID + first.astype(I32) * FLAG_FIRST + last.astype(I32) * FLAG_LAST
    return order, meta_s, item_tile, item_exp, flags


def _moe_experts(h, meta, w1, w3, w2, layer):
    tokens = h.shape[0]
    tm = MOE_TILE
    assert tokens % tm == 0
    n_tiles = tokens // tm
    pair_classes = MOE_GROUPS * (MOE_PER_GROUP * (MOE_PER_GROUP - 1) // 2)
    n_items = min(MOE_EXPERTS * n_tiles, 2 * (n_tiles + pair_classes - 1))
    order, meta_s, item_tile, item_exp, flags = _moe_plan(meta, n_tiles, n_items)
    wmap = lambda g, tok, it, ie, fl: (layer, ie[g], 0, 0)
    wspec1 = pl.BlockSpec((None, 1, D_MODEL, MOE_HIDDEN), wmap)
    wspec2 = pl.BlockSpec((None, 1, MOE_HIDDEN, D_MODEL), wmap)
    return pl.pallas_call(
        functools.partial(_moe_experts_body, n_tiles=n_tiles),
        grid_spec=pltpu.PrefetchScalarGridSpec(
            num_scalar_prefetch=4,
            grid=(n_items,),
            in_specs=[
                pl.BlockSpec(memory_space=pl.ANY),
                pl.BlockSpec((tm, META_COLS), lambda g, tok, it, ie, fl: (it[g], 0)),
                wspec1, wspec1, wspec2,
            ],
            out_specs=pl.BlockSpec(memory_space=pl.ANY),
            scratch_shapes=[
                pltpu.VMEM((2, tm, D_MODEL), F32),
                pltpu.VMEM((2, tm, D_MODEL), F32),
                pltpu.SemaphoreType.DMA((2,)),
                pltpu.SemaphoreType.DMA((2,)),
            ],
        ),
        out_shape=jax.ShapeDtypeStruct((tokens, D_MODEL), F32),
        compiler_params=_cparams(("arbitrary",)),
        name="moe_experts",
    )(order, item_tile, item_exp, flags, h, meta_s, w1, w3, w2)


def _final_combine_body(x_ref, m_ref, gt_ref, fg_ref, o_ref):
    x = x_ref[...] + gt_ref[...] * m_ref[...].reshape(x_ref.shape)
    o_ref[...] = x * lax.rsqrt(jnp.mean(x * x, axis=-1, keepdims=True) + EPS) * fg_ref[...]


def _final_combine(x, moe_out, gt, final_g):
    nb, sl, _ = x.shape
    bt, lt = _row_tiling(nb, sl, ROW_TILE)
    nl = sl // lt
    xmap = lambda i: (i // nl, i % nl, 0)
    return pl.pallas_call(
        _final_combine_body,
        grid=((nb // bt) * nl,),
        in_specs=[
            pl.BlockSpec((bt, lt, D_MODEL), xmap),
            pl.BlockSpec((bt * lt, D_MODEL), lambda i: (i, 0)),
            pl.BlockSpec((bt, 1, D_MODEL), lambda i: (i // nl, 0, 0)),
            pl.BlockSpec((1, 1, D_MODEL), lambda i: (0, 0, 0)),
        ],
        out_specs=pl.BlockSpec((bt, lt, D_MODEL), xmap),
        out_shape=jax.ShapeDtypeStruct(x.shape, F32),
        compiler_params=_cparams(("parallel",)),
        name="final_combine",
    )(x, moe_out, gt, final_g.reshape(1, 1, D_MODEL))


def _rope_tables(pos):
    half = ATT_HEAD_DIM // 2
    inv_freq = ROPE_THETA ** (-jnp.arange(half, dtype=F32) / half)
    ang = pos.astype(F32)[:, None] * inv_freq[None, :]
    cos = jnp.cos(ang)
    sin = jnp.sin(ang)
    cos_h = jnp.concatenate([cos, cos], axis=1)
    sin_h = jnp.concatenate([-sin, sin], axis=1)
    return jnp.tile(cos_h, (1, ATT_HEADS)), jnp.tile(sin_h, (1, ATT_HEADS))


def _router_weights(wg, bg, we, be):
    w = jnp.concatenate([wg, we], axis=1)
    w = jnp.pad(w, ((0, 0), (0, ROUTE_COLS - w.shape[1])))
    b = jnp.pad(jnp.concatenate([bg, be]), (0, ROUTE_COLS - MOE_GROUPS - MOE_EXPERTS)).reshape(1, ROUTE_COLS)
    hi = w.astype(BF16)
    lo = (w - hi.astype(F32)).astype(BF16)
    return hi, lo, b


def _even_w_in_cols(w):
    dt0 = SSD_INNER + SSD_CONV_DIM
    dt1 = dt0 + SSD_HEADS
    zeros = jnp.zeros((w.shape[0], LANES - SSD_HEADS), w.dtype)
    return jnp.concatenate([w[:, :dt0], w[:, dt1:], w[:, dt0:dt1], zeros], axis=1).astype(BF16)


def _run_group(x, mods, pos, states, caches, wts):
    nb, sl, _ = x.shape
    ssd_in, conv_in, ret_in = states
    cos, sin = _rope_tables(pos)

    def mod_parts(i):
        m = mods[i].reshape(nb, 1, N_MOD, D_MODEL)
        return [m[:, :, j] for j in range(N_MOD)]

    sh1, sc1, gt1, sh2, sc2, gt2 = mod_parts(0)
    p = _even_in(x, wts["norm_mix_g"][0], sc1, sh1, wts["even_w_in"]).reshape(nb, sl, P_WIDTH)
    y, ssd_new, conv_new, ret_new = _even_core(
        p, cos, sin, wts["ssd_conv_w"], wts["ssd_conv_b"], wts["ssd_dt_bias"], wts["ssd_a_log"],
        wts["ssd_d"], wts["ssd_norm_g"], wts["ret_norm_g"], ssd_in, conv_in, ret_in)
    y = y.reshape(nb * sl, EVEN_OUT)
    x, h, meta = _proj_route(y, wts["even_w_out"], x, gt1, wts["norm_ffn_g"][0], sc2, sh2, *wts["router"][0])
    mix = _moe_experts(h, meta, *wts["experts"], 0)
    gt_moe = gt2

    sh1, sc1, gt1, sh2, sc2, gt2 = mod_parts(1)
    if caches is None:
        x, q, k, v = _qkv_rope(x, mix, gt_moe, wts["norm_mix_g"][1], sc1, sh1, wts["odd_w_qkv"], cos, sin)
        q3, k3, v3 = (a.reshape(nb, sl, ATT_WIDTH) for a in (q, k, v))
        o, k_t, v_t = _attn_prompt(q3, k3, v3)
        keep = min(ATT_WINDOW, sl)
        new_k, new_v = (a.transpose(0, 3, 1, 2)[:, sl - keep:] for a in (k_t, v_t))
    else:
        cos_t, sin_t = jnp.tile(cos, (nb, 1)), jnp.tile(sin, (nb, 1))
        x, q, k, v = _qkv_rope(x, mix, gt_moe, wts["norm_mix_g"][1], sc1, sh1, wts["odd_w_qkv"], cos_t, sin_t)
        q3, k3, v3 = (a.reshape(nb, sl, ATT_WIDTH) for a in (q, k, v))
        o, new_k, new_v = _attn_sample(q3, k3, v3, *caches)
    x, h, meta = _proj_route(o.reshape(nb * sl, ATT_WIDTH), wts["odd_w_out"], x, gt1,
                             wts["norm_ffn_g"][1], sc2, sh2, *wts["router"][1])
    mix = _moe_experts(h, meta, *wts["experts"], 1)
    x = _final_combine(x, mix, gt2, wts["final_norm_g"])
    return x, ssd_new[None], conv_new[None], ret_new[None], new_k[None], new_v[None]


def kernel(x_prompt, x_sample, state_ssd, state_conv, state_ret, cache_k, cache_v, c_prompt, c_sample, ada_w, ada_b, norm_mix_g, norm_ffn_g, final_norm_g, even_w_in, even_w_out, ssd_conv_w, ssd_conv_b, ssd_dt_bias, ssd_a_log, ssd_d, ssd_norm_g, ret_norm_g, odd_w_qkv, odd_w_out, moe_wg, moe_bg, moe_we, moe_be, moe_w1, moe_w3, moe_w2):
    depth = ada_w.shape[0]
    assert depth == 2 and even_w_in.shape[0] == 1 and odd_w_qkv.shape[0] == 1
    bp, sp, _ = x_prompt.shape
    bs, ss, _ = x_sample.shape

    wts = {
        "norm_mix_g": norm_mix_g, "norm_ffn_g": norm_ffn_g, "final_norm_g": final_norm_g,
        "even_w_in": _even_w_in_cols(even_w_in[0]), "even_w_out": even_w_out[0].astype(BF16),
        "ssd_conv_w": ssd_conv_w[0], "ssd_conv_b": ssd_conv_b[0], "ssd_dt_bias": ssd_dt_bias[0],
        "ssd_a_log": ssd_a_log[0], "ssd_d": ssd_d[0], "ssd_norm_g": ssd_norm_g[0], "ret_norm_g": ret_norm_g[0],
        "odd_w_qkv": odd_w_qkv[0].astype(BF16), "odd_w_out": odd_w_out[0].astype(BF16),
        "router": [_router_weights(moe_wg[i], moe_bg[i], moe_we[i], moe_be[i]) for i in range(depth)],
        "experts": (moe_w1.astype(BF16), moe_w3.astype(BF16), moe_w2.astype(BF16)),
    }
    mods = _adaln(jnp.concatenate([c_prompt, c_sample], axis=0), ada_w, ada_b)

    zeros_p = (
        jnp.zeros((bp, SSD_HEADS, SSD_HEAD_DIM, SSD_STATE), F32),
        jnp.zeros((bp, SSD_CONV - 1, SSD_CONV_DIM), F32),
        jnp.zeros((bp, RET_HEADS, RET_DV, RET_DK), F32),
    )
    out_p = _run_group(x_prompt, mods[:, :bp], jnp.arange(sp, dtype=I32), zeros_p, None, wts)
    out_s = _run_group(x_sample, mods[:, bp:], PAST_LEN + jnp.arange(ss, dtype=I32),
                       (state_ssd[0], state_conv[0], state_ret[0]), (cache_k[0], cache_v[0]), wts)
    return (out_p[0], out_s[0]) + out_p[1:] + out_s[1:]
```

```python
import functools
import math

import jax
import jax.numpy as jnp
from jax import lax
from jax.experimental import pallas as pl
from jax.experimental.pallas import tpu as pltpu

F32 = jnp.float32
BF16 = jnp.bfloat16
I32 = jnp.int32

D_MODEL = 1024
EPS = 1e-6
N_MOD = 6
SSD_HEADS = 16
SSD_HEAD_DIM = 64
SSD_INNER = SSD_HEADS * SSD_HEAD_DIM
SSD_GROUPS = 4
SSD_STATE = 64
SSD_CONV = 4
SSD_CONV_DIM = SSD_INNER + 2 * SSD_GROUPS * SSD_STATE
RET_HEADS = 8
RET_DK = 64
RET_DV = 128
RET_QK = RET_HEADS * RET_DK
RET_V = RET_HEADS * RET_DV
EVEN_OUT = SSD_INNER + RET_V
ATT_HEADS = 16
ATT_HEAD_DIM = 64
ATT_WIDTH = ATT_HEADS * ATT_HEAD_DIM
DILATED_PATTERNS = ((128, 1), (512, 4), (2048, 16))
ATT_WINDOW = 2048
PAST_LEN = 16384
ROPE_THETA = 10000.0
MOE_GROUPS = 4
MOE_PER_GROUP = 8
MOE_EXPERTS = MOE_GROUPS * MOE_PER_GROUP
MOE_HIDDEN = 256

LANES = 128
SUBLANES = 8
VMEM_LIMIT = 56 * 1024 * 1024

P_Z = 0
P_XBC = P_Z + SSD_INNER
P_Q = P_XBC + SSD_CONV_DIM
P_K = P_Q + RET_QK
P_V = P_K + RET_QK
P_G = P_V + RET_V
P_DT = P_G + RET_V
P_WIDTH = P_DT + LANES

SCAN_CHUNK = 128
ATT_BLOCK = 128
Q_SCALE = ATT_HEAD_DIM ** -0.5 * math.log2(math.e)
MASKED = -1e30
MOE_TILE = 256
ROW_TILE = 1024
QKV_ROWS = 512
EVEN_IN_ROWS = 512
EVEN_IN_COLS = P_WIDTH
ADALN_COLS = 1024


def _cparams(sem):
    return pltpu.CompilerParams(dimension_semantics=sem, vmem_limit_bytes=VMEM_LIMIT)


def _row_tiling(nb, sl, target):
    if sl >= target:
        assert sl % target == 0
        return 1, target
    bt = min(nb, target // sl)
    assert nb % bt == 0
    return bt, sl


def _silu(x):
    return x * (0.5 * jnp.tanh(0.5 * x) + 0.5)


def _norm_mod(x, g, sc, sh):
    ms = jnp.mean(x * x, axis=-1, keepdims=True)
    return x * lax.rsqrt(ms + EPS) * g * (1.0 + sc) + sh


def _split3(x):
    hi = x.astype(BF16)
    r1 = x - hi.astype(F32)
    mid = r1.astype(BF16)
    lo = (r1 - mid.astype(F32)).astype(BF16)
    return hi, mid, lo


def _dot_exact_rhs(x, m_bf16):
    hi, mid, lo = _split3(x)
    dot = functools.partial(jnp.dot, preferred_element_type=F32)
    return dot(hi, m_bf16) + dot(mid, m_bf16) + dot(lo, m_bf16)


def _dot_nt(a, b):
    return lax.dot_general(a, b, (((1,), (1,)), ((), ())), preferred_element_type=F32)


def _rope(a, cos, sin_signed):
    w = a.shape[-1]
    lane = lax.broadcasted_iota(I32, a.shape, 1)
    first = (lane & (ATT_HEAD_DIM // 2)) == 0
    rot = jnp.where(first, pltpu.roll(a, w - ATT_HEAD_DIM // 2, 1), pltpu.roll(a, ATT_HEAD_DIM // 2, 1))
    return a * cos + rot * sin_signed


def _adaln_body(c_ref, w_ref, b_ref, o_ref):
    a = _silu(c_ref[...]).astype(BF16)
    o_ref[0] = jnp.dot(a, w_ref[0].astype(BF16), preferred_element_type=F32) + b_ref[0]


def _adaln(c_all, ada_w, ada_b):
    nb = c_all.shape[0]
    depth, _, n6 = ada_w.shape
    tn = ADALN_COLS
    return pl.pallas_call(
        _adaln_body,
        grid=(depth, n6 // tn),
        in_specs=[
            pl.BlockSpec((nb, D_MODEL), lambda i, j: (0, 0)),
            pl.BlockSpec((1, D_MODEL, tn), lambda i, j: (i, 0, j)),
            pl.BlockSpec((1, 1, tn), lambda i, j: (i, 0, j)),
        ],
        out_specs=pl.BlockSpec((1, nb, tn), lambda i, j: (i, 0, j)),
        out_shape=jax.ShapeDtypeStruct((depth, nb, n6), F32),
        compiler_params=_cparams(("parallel", "parallel")),
        name="adaln",
    )(c_all, ada_w, ada_b.reshape(depth, 1, n6))


def _even_in_body(x_ref, g_ref, sc_ref, sh_ref, w_ref, o_ref, h_scr):
    @pl.when(pl.program_id(1) == 0)
    def _():
        h = _norm_mod(x_ref[...], g_ref[...], sc_ref[...], sh_ref[...])
        h_scr[...] = h.reshape(h_scr.shape).astype(BF16)

    o_ref[...] = jnp.dot(h_scr[...], w_ref[...], preferred_element_type=F32)


def _even_in(x, g, sc, sh, w):
    nb, sl, _ = x.shape
    n = w.shape[1]
    bt, lt = _row_tiling(nb, sl, EVEN_IN_ROWS)
    nl = sl // lt
    tm = bt * lt
    tn = EVEN_IN_COLS
    assert n % tn == 0
    return pl.pallas_call(
        _even_in_body,
        grid=((nb // bt) * nl, n // tn),
        in_specs=[
            pl.BlockSpec((bt, lt, D_MODEL), lambda i, j: (i // nl, i % nl, 0)),
            pl.BlockSpec((1, 1, D_MODEL), lambda i, j: (0, 0, 0)),
            pl.BlockSpec((bt, 1, D_MODEL), lambda i, j: (i // nl, 0, 0)),
            pl.BlockSpec((bt, 1, D_MODEL), lambda i, j: (i // nl, 0, 0)),
            pl.BlockSpec((D_MODEL, tn), lambda i, j: (0, j), pipeline_mode=pl.Buffered(1 if tn == n else 2)),
        ],
        out_specs=pl.BlockSpec((tm, tn), lambda i, j: (i, j)),
        out_shape=jax.ShapeDtypeStruct((nb * sl, n), F32),
        scratch_shapes=[pltpu.VMEM((tm, D_MODEL), BF16)],
        compiler_params=_cparams(("parallel", "arbitrary")),
        name="even_in",
    )(x, g.reshape(1, 1, D_MODEL), sc, sh, w)


def _ret_log_gamma(h):
    return math.log1p(-(2.0 ** (-5.0 - h)))


def _ret_log_gamma_exp(h, steps):
    return math.exp(_ret_log_gamma(h) * steps)


def _ret_tables(valid_len):
    cs = SCAN_CHUNK
    lg = jnp.asarray([_ret_log_gamma(h) for h in range(RET_HEADS)], F32)[:, None, None]
    steps = jnp.minimum(jnp.arange(cs) + 1, valid_len).astype(F32)
    gap = steps[:, None] - steps[None, :]
    causal = jnp.arange(cs)[None, :] <= jnp.arange(cs)[:, None]
    decay = jnp.exp(jnp.where(causal, lg * gap, -jnp.inf))
    wide = jnp.broadcast_to(steps[:, None], (cs, LANES))
    return decay, jnp.exp(lg * wide), jnp.exp(lg * (float(valid_len) - wide))


def _even_core_body(p_ref, cos_ref, sin_ref, cw_ref, cb_ref, dtb_ref, alog_ref, dskip_ref, ssdg_ref, retg_ref,
                    rdec_ref, rgrow_ref, rend_ref, ssd0_ref, conv0_ref, ret0_ref,
                    y_ref, ssd_out_ref, conv_out_ref, ret_out_ref,
                    xpad_scr, ssd_scr, ret_scr, yacc_scr, *, valid_len):
    cs = SCAN_CHUNK
    c = pl.program_id(1)
    nc = pl.num_programs(1)
    hd = SSD_HEAD_DIM

    @pl.when(c == 0)
    def _():
        ssd_scr[...] = ssd0_ref[0]
        ret_scr[...] = ret0_ref[0]
        xpad_scr[0:SUBLANES, :] = conv0_ref[0]

    row = lax.broadcasted_iota(I32, (cs, LANES), 0)
    col = lax.broadcasted_iota(I32, (cs, cs), 1)
    rowc = lax.broadcasted_iota(I32, (cs, cs), 0)
    causal = col <= rowc
    live = row < valid_len

    def proj(lo, hi):
        val = p_ref[0, :, lo:hi]
        if valid_len < cs:
            val = jnp.concatenate([val, jnp.zeros((cs - valid_len, hi - lo), F32)], axis=0)
        return val

    def emit(lo, hi, val):
        y_ref[0, :, lo:hi] = val[0:valid_len].astype(BF16)

    xpad_scr[SUBLANES:SUBLANES + cs, :] = proj(P_XBC, P_XBC + SSD_CONV_DIM)
    conv = cb_ref[...] + xpad_scr[SUBLANES:SUBLANES + cs, :] * cw_ref[SSD_CONV - 1:SSD_CONV, :]
    for back in range(1, SSD_CONV):
        tap = SSD_CONV - 1 - back
        conv = conv + xpad_scr[SUBLANES - back:SUBLANES - back + cs, :] * cw_ref[tap:tap + 1, :]
    xc = _silu(conv)

    @pl.when(c == nc - 1)
    def _():
        conv_out_ref[0] = xpad_scr[valid_len:valid_len + SUBLANES, :]

    xpad_scr[0:SUBLANES, :] = xpad_scr[cs:cs + SUBLANES, :]

    xh = xc[:, 0:SSD_INNER]
    if valid_len < cs:
        xh = jnp.where(live[:, 0:1], xh, 0.0)

    dt_in = proj(P_DT, P_DT + LANES).T[0:SSD_HEADS, :] + dtb_ref[...]
    dt = jnp.maximum(dt_in, 0.0) + jnp.log1p(jnp.exp(-jnp.abs(dt_in)))
    la = dt * (-jnp.exp(alog_ref[...]))
    if valid_len < cs:
        la = jnp.where(lax.broadcasted_iota(I32, (SSD_HEADS, cs), 1) < valid_len, la, 0.0)
    acum = _dot_exact_rhs(la, (rowc <= col).astype(BF16))
    acum_last = acum[:, cs - 1:cs]
    w_state = dt * jnp.exp(acum_last - acum)
    cdecay = jnp.exp(acum_last)
    acum_c = jnp.concatenate([acum, jnp.zeros((LANES - SSD_HEADS, cs), F32)], axis=0).T
    eacum_c = jnp.exp(acum_c)

    xh_b = xh.astype(BF16)
    xh_t = xh.T

    for grp in range(SSD_GROUPS):
        b_g = xc[:, SSD_INNER + grp * SSD_STATE:SSD_INNER + (grp + 1) * SSD_STATE].astype(BF16)
        c_g = xc[:, SSD_INNER + (SSD_GROUPS + grp) * SSD_STATE:SSD_INNER + (SSD_GROUPS + grp + 1) * SSD_STATE].astype(BF16)
        s_g = _dot_nt(c_g, b_g)
        for hh in range(SSD_HEADS // SSD_GROUPS):
            h = grp * (SSD_HEADS // SSD_GROUPS) + hh
            lo, hi = h * hd, (h + 1) * hd
            decay = jnp.exp(jnp.where(causal, acum_c[:, h:h + 1] - acum[h:h + 1, :], -jnp.inf))
            m = (s_g * decay * dt[h:h + 1, :]).astype(BF16)
            y_h = jnp.dot(m, xh_b[:, lo:hi], preferred_element_type=F32)
            h_prev = ssd_scr[h]
            y_h = y_h + _dot_nt(c_g, h_prev.astype(BF16)) * eacum_c[:, h:h + 1]
            yacc_scr[:, lo:hi] = y_h
            xw_t = (xh_t[lo:hi, :] * w_state[h:h + 1, :]).astype(BF16)
            ssd_scr[h] = h_prev * cdecay[h:h + 1, :] + jnp.dot(xw_t, b_g, preferred_element_type=F32)

    y = yacc_scr[...] + dskip_ref[...] * xh
    y = y * _silu(proj(P_Z, P_Z + SSD_INNER))
    y = y * lax.rsqrt(jnp.mean(y * y, axis=-1, keepdims=True) + EPS) * ssdg_ref[...]
    emit(0, SSD_INNER, y)

    cos = cos_ref[:, 0:RET_QK]
    sin = sin_ref[:, 0:RET_QK]
    rq = _rope(proj(P_Q, P_Q + RET_QK), cos, sin).astype(BF16)
    rk = (_rope(proj(P_K, P_K + RET_QK), cos, sin) * RET_DK ** -0.5).astype(BF16)
    rv = proj(P_V, P_V + RET_V)
    if valid_len < cs:
        rv = jnp.where(live[:, 0:1], rv, 0.0)
    for h in range(RET_HEADS):
        q_h = rq[:, h * RET_DK:(h + 1) * RET_DK]
        k_h = rk[:, h * RET_DK:(h + 1) * RET_DK]
        v_h = rv[:, h * RET_DV:(h + 1) * RET_DV]
        m = (_dot_nt(q_h, k_h) * rdec_ref[h]).astype(BF16)
        o_h = jnp.dot(m, v_h.astype(BF16), preferred_element_type=F32)
        s_prev = ret_scr[h]
        o_h = o_h + _dot_nt(q_h, s_prev.astype(BF16)) * rgrow_ref[h]
        v_te_t = (v_h * rend_ref[h]).T.astype(BF16)
        ret_scr[h] = s_prev * _ret_log_gamma_exp(h, valid_len) + jnp.dot(v_te_t, k_h, preferred_element_type=F32)
        o_h = o_h * lax.rsqrt(jnp.mean(o_h * o_h, axis=-1, keepdims=True) + EPS) * retg_ref[:, h * RET_DV:(h + 1) * RET_DV]
        o_h = o_h * _silu(proj(P_G + h * RET_DV, P_G + (h + 1) * RET_DV))
        emit(SSD_INNER + h * RET_DV, SSD_INNER + (h + 1) * RET_DV, o_h)

    @pl.when(c == nc - 1)
    def _():
        ssd_out_ref[0] = ssd_scr[...]
        ret_out_ref[0] = ret_scr[...]


def _even_core(p, cos, sin, conv_w, conv_b, dt_bias, a_log, d_skip, ssd_norm_g, ret_norm_g, ssd0, conv0, ret0):
    nb, sl, _ = p.shape
    cs = SCAN_CHUNK
    if sl % cs == 0:
        nc, valid_len = sl // cs, cs
    else:
        assert sl < cs and sl % SUBLANES == 0
        nc, valid_len = 1, sl
        grow = ((0, cs - sl), (0, 0))
        cos, sin = jnp.pad(cos, grow), jnp.pad(sin, grow)
    dtb = jnp.broadcast_to(dt_bias[:, None], (SSD_HEADS, cs))
    alog = jnp.broadcast_to(a_log[:, None], (SSD_HEADS, cs))
    dskip_x = jnp.repeat(d_skip, SSD_HEAD_DIM).reshape(1, SSD_INNER)
    conv0p = jnp.pad(conv0, ((0, 0), (SUBLANES - (SSD_CONV - 1), 0), (0, 0)))
    assert RET_DV == LANES
    const2 = lambda b, c: (0, 0)
    const3 = lambda b, c: (0, 0, 0)
    st4 = lambda b, c: (b, 0, 0, 0)
    y, ssd_new, conv_new, ret_new = pl.pallas_call(
        functools.partial(_even_core_body, valid_len=valid_len),
        grid=(nb, nc),
        in_specs=[
            pl.BlockSpec((1, valid_len, P_WIDTH), lambda b, c: (b, c, 0)),
            pl.BlockSpec((cs, ATT_WIDTH), lambda b, c: (c, 0)),
            pl.BlockSpec((cs, ATT_WIDTH), lambda b, c: (c, 0)),
            pl.BlockSpec((SSD_CONV, SSD_CONV_DIM), const2),
            pl.BlockSpec((1, SSD_CONV_DIM), const2),
            pl.BlockSpec((SSD_HEADS, cs), const2),
            pl.BlockSpec((SSD_HEADS, cs), const2),
            pl.BlockSpec((1, SSD_INNER), const2),
            pl.BlockSpec((1, SSD_INNER), const2),
            pl.BlockSpec((1, RET_V), const2),
            pl.BlockSpec((RET_HEADS, cs, cs), const3),
            pl.BlockSpec((RET_HEADS, cs, LANES), const3),
            pl.BlockSpec((RET_HEADS, cs, LANES), const3),
            pl.BlockSpec((1, SSD_HEADS, SSD_HEAD_DIM, SSD_STATE), st4),
            pl.BlockSpec((1, SUBLANES, SSD_CONV_DIM), lambda b, c: (b, 0, 0)),
            pl.BlockSpec((1, RET_HEADS, RET_DV, RET_DK), st4),
        ],
        out_specs=[
            pl.BlockSpec((1, valid_len, EVEN_OUT), lambda b, c: (b, c, 0)),
            pl.BlockSpec((1, SSD_HEADS, SSD_HEAD_DIM, SSD_STATE), st4),
            pl.BlockSpec((1, SUBLANES, SSD_CONV_DIM), lambda b, c: (b, 0, 0)),
            pl.BlockSpec((1, RET_HEADS, RET_DV, RET_DK), st4),
        ],
        out_shape=[
            jax.ShapeDtypeStruct((nb, sl, EVEN_OUT), BF16),
            jax.ShapeDtypeStruct((nb, SSD_HEADS, SSD_HEAD_DIM, SSD_STATE), F32),
            jax.ShapeDtypeStruct((nb, SUBLANES, SSD_CONV_DIM), F32),
            jax.ShapeDtypeStruct((nb, RET_HEADS, RET_DV, RET_DK), F32),
        ],
        scratch_shapes=[
            pltpu.VMEM((cs + SUBLANES, SSD_CONV_DIM), F32),
            pltpu.VMEM((SSD_HEADS, SSD_HEAD_DIM, SSD_STATE), F32),
            pltpu.VMEM((RET_HEADS, RET_DV, RET_DK), F32),
            pltpu.VMEM((cs, SSD_INNER), F32),
        ],
        compiler_params=_cparams(("parallel", "arbitrary")),
        name="even_core",
    )(p, cos, sin, conv_w, conv_b.reshape(1, SSD_CONV_DIM), dtb, alog, dskip_x,
      ssd_norm_g.reshape(1, SSD_INNER), ret_norm_g.reshape(1, RET_V), *_ret_tables(valid_len), ssd0, conv0p, ret0)
    return y, ssd_new, conv_new[:, SUBLANES - (SSD_CONV - 1):], ret_new


def _qkv_body(x_ref, m_ref, gt_ref, g_ref, sc_ref, sh_ref, w_ref, cos_ref, sin_ref,
              xo_ref, q_ref, k_ref, v_ref):
    x = x_ref[...] + gt_ref[...] * m_ref[...].reshape(x_ref.shape)
    xo_ref[...] = x
    h = _norm_mod(x, g_ref[...], sc_ref[...], sh_ref[...])
    h = h.reshape(q_ref.shape).astype(BF16)
    cos = cos_ref[...]
    sin = sin_ref[...]
    dot = functools.partial(jnp.dot, preferred_element_type=F32)
    q_ref[...] = _rope(dot(h, w_ref[:, 0:ATT_WIDTH]), cos, sin) * Q_SCALE
    k_ref[...] = _rope(dot(h, w_ref[:, ATT_WIDTH:2 * ATT_WIDTH]), cos, sin)
    v_ref[...] = dot(h, w_ref[:, 2 * ATT_WIDTH:3 * ATT_WIDTH])


def _qkv_rope(x, mix, gt, g, sc, sh, w, cos, sin):
    nb, sl, _ = x.shape
    bt, lt = _row_tiling(nb, sl, QKV_ROWS)
    nl = sl // lt
    tm = bt * lt
    assert cos.shape[0] in (sl, nb * sl) and (cos.shape[0] == sl) == (bt == 1)
    ntab = cos.shape[0] // tm
    out = jax.ShapeDtypeStruct((nb * sl, ATT_WIDTH), F32)
    ospec = pl.BlockSpec((tm, ATT_WIDTH), lambda i: (i, 0))
    tspec = pl.BlockSpec((tm, ATT_WIDTH), lambda i: (i % ntab, 0))
    xspec = pl.BlockSpec((bt, lt, D_MODEL), lambda i: (i // nl, i % nl, 0))
    mspec = pl.BlockSpec((bt, 1, D_MODEL), lambda i: (i // nl, 0, 0))
    out_specs = [xspec, ospec, ospec, ospec]
    out_shape = [jax.ShapeDtypeStruct(x.shape, F32), out, out, out]
    return pl.pallas_call(
        _qkv_body,
        grid=((nb // bt) * nl,),
        in_specs=[
            xspec, ospec, mspec,
            pl.BlockSpec((1, 1, D_MODEL), lambda i: (0, 0, 0)),
            mspec, mspec,
            pl.BlockSpec((D_MODEL, 3 * ATT_WIDTH), lambda i: (0, 0)),
            tspec, tspec,
        ],
        out_specs=out_specs,
        out_shape=out_shape,
        compiler_params=_cparams(("parallel",)),
        name="qkv_rope",
    )(x, mix, gt, g.reshape(1, 1, D_MODEL), sc, sh, w, cos, sin)


def _attn_prompt_body(q_ref, k_ref, v_ref, o_ref, kt_ref, vt_ref, ob_scr, lse_scr, p_scr, m_scr, *, seq):
    blk = ATT_BLOCK
    half = ATT_HEAD_DIM
    lane = lax.broadcasted_iota(I32, (blk, LANES), 1)
    head_a = lane < half
    rowi = lax.broadcasted_iota(I32, (2 * blk, blk), 0) & (blk - 1)
    coli = lax.broadcasted_iota(I32, (2 * blk, blk), 1)
    mask_cur = coli <= rowi
    mask_prev = coli >= rowi
    ones = jnp.ones((blk, LANES), BF16)

    for i in range(seq // blk):
        rows = slice(i * blk, (i + 1) * blk)
        kt_ref[0, :, :, rows] = k_ref[0, rows, :].T.reshape(2, half, blk)
        vt_ref[0, :, :, rows] = v_ref[0, rows, :].T.reshape(2, half, blk)

    def ld(ref, dil, s0):
        if dil == 1:
            return ref[0, pl.ds(s0, blk), :]
        return ref[0, pl.ds(s0, blk, stride=dil), :]

    def probs(dil, start, pstart, slot):
        q = ld(q_ref, dil, start)
        q2 = jnp.concatenate([jnp.where(head_a, q, 0.0), jnp.where(head_a, 0.0, q)], axis=0).astype(BF16)
        sc = jnp.where(mask_cur, _dot_nt(q2, ld(k_ref, dil, start).astype(BF16)), MASKED)
        if pstart is None:
            m = jnp.max(sc, axis=1, keepdims=True)
        else:
            sp = jnp.where(mask_prev, _dot_nt(q2, ld(k_ref, dil, pstart).astype(BF16)), MASKED)
            m = jnp.max(jnp.maximum(sc, sp), axis=1, keepdims=True)
            p_scr[slot, :, blk:2 * blk] = jnp.exp2(sp - m).astype(BF16)
        p_scr[slot, :, 0:blk] = jnp.exp2(sc - m).astype(BF16)
        m_scr[slot] = jnp.where(head_a, m[0:blk], m[blk:2 * blk])

    def weigh(gi, dil, start, pstart, slot):
        ve = jnp.concatenate([ld(v_ref, dil, start).astype(BF16), ones], axis=1)
        if pstart is None:
            oe = jnp.dot(p_scr[slot, :, 0:blk], ve, preferred_element_type=F32)
        else:
            vpe = jnp.concatenate([ld(v_ref, dil, pstart).astype(BF16), ones], axis=1)
            oe = jnp.dot(p_scr[slot], jnp.concatenate([ve, vpe], axis=0), preferred_element_type=F32)
        num = jnp.where(head_a, oe[0:blk, 0:LANES], oe[blk:2 * blk, 0:LANES])
        den = jnp.where(head_a, oe[0:blk, LANES:2 * LANES], oe[blk:2 * blk, LANES:2 * LANES])
        rows = pl.ds(start, blk) if dil == 1 else pl.ds(start, blk, stride=dil)
        ob_scr[gi, rows, :] = num / den
        lse_scr[gi, rows, :] = m_scr[slot] + jnp.log2(den)

    def pipelined(gi, dil, count, coords):
        s1 = lambda t, slot: probs(dil, *coords(t), slot)
        s2 = lambda t, slot: weigh(gi, dil, *coords(t), slot)
        s1(0, 0)
        if count == 1:
            s2(0, 0)
            return
        s1(1, 1)
        trips = (count - 2) // 2

        def body(i, carry):
            a = 2 * i
            s2(a, 0)
            s2(a + 1, 1)
            s1(a + 2, 0)
            s1(a + 3, 1)
            return carry

        if trips > 0:
            lax.fori_loop(0, trips, body, 0)
        s2(2 * trips, 0)
        if (count - 2) % 2 == 1:
            s1(count - 1, 0)
        s2(2 * trips + 1, 1)
        if (count - 2) % 2 == 1:
            s2(count - 1, 0)

    for gi, (window, dil) in enumerate(DILATED_PATTERNS):
        assert window // dil == blk and (seq // dil) % blk == 0
        per_residue = seq // dil // blk
        span = blk * dil
        pipelined(gi, dil, dil, lambda t: (t, None))
        if per_residue > 1:
            def coords(t, dil=dil, span=span):
                start = t % dil + (1 + t // dil) * span
                return start, start - span
            pipelined(gi, dil, dil * (per_residue - 1), coords)

    def merge(i, carry):
        rows = pl.ds(pl.multiple_of(i * blk, blk), blk)
        l0 = lse_scr[0, rows, :]
        l1 = lse_scr[1, rows, :]
        l2 = lse_scr[2, rows, :]
        m = jnp.maximum(jnp.maximum(l0, l1), l2)
        w0 = jnp.exp2(l0 - m)
        w1 = jnp.exp2(l1 - m)
        w2 = jnp.exp2(l2 - m)
        o = (w0 * ob_scr[0, rows, :] + w1 * ob_scr[1, rows, :] + w2 * ob_scr[2, rows, :]) / (w0 + w1 + w2)
        o_ref[0, rows, :] = o.astype(BF16)
        return carry

    lax.fori_loop(0, seq // blk, merge, 0)


def _attn_prompt(q, k, v):
    nb, seq, _ = q.shape
    npair = ATT_WIDTH // LANES
    spec = pl.BlockSpec((1, seq, LANES), lambda b, h: (b, 0, h))
    tspec = pl.BlockSpec((1, 2, ATT_HEAD_DIM, seq), lambda b, h: (b, h, 0, 0))
    tshape = jax.ShapeDtypeStruct((nb, ATT_HEADS, ATT_HEAD_DIM, seq), F32)
    return pl.pallas_call(
        functools.partial(_attn_prompt_body, seq=seq),
        grid=(nb, npair),
        in_specs=[spec, spec, spec],
        out_specs=[spec, tspec, tspec],
        out_shape=[jax.ShapeDtypeStruct((nb, seq, ATT_WIDTH), BF16), tshape, tshape],
        scratch_shapes=[
            pltpu.VMEM((len(DILATED_PATTERNS), seq, LANES), F32),
            pltpu.VMEM((len(DILATED_PATTERNS), seq, LANES), F32),
            pltpu.VMEM((2, 2 * ATT_BLOCK, 2 * ATT_BLOCK), BF16),
            pltpu.VMEM((2, ATT_BLOCK, LANES), F32),
        ],
        compiler_params=_cparams(("parallel", "parallel")),
        name="attn_prompt",
    )(q, k, v)


SAMPLE_HEADS_PER_STEP = 8
SAMPLE_ROWS = 2 * SUBLANES


def _attn_sample_body(q_ref, kn_ref, vn_ref, ck_ref, cv_ref, o_ref, nk_ref, nv_ref, *, past, new):
    rows = SAMPLE_ROWS
    keys = past + LANES
    lane = lax.broadcasted_iota(I32, (ATT_HEAD_DIM, LANES), 1)
    is_new = lane >= LANES - new
    t_idx = lax.broadcasted_iota(I32, (rows, keys), 0) & (new - 1)
    k_idx = lax.broadcasted_iota(I32, (rows, keys), 1)
    dist = past + t_idx - k_idx
    oks = [(dist >= 0) & (dist <= window) & ((dist & (dil - 1)) == 0) for window, dil in DILATED_PATTERNS]
    zpad = jnp.zeros((ATT_HEAD_DIM, LANES - new), F32)

    for h in range(SAMPLE_HEADS_PER_STEP):
        alls = []
        for c_ref, n_ref, out_ref in ((ck_ref, kn_ref, nk_ref), (cv_ref, vn_ref, nv_ref)):
            old = c_ref[0, h]
            fresh = n_ref[0, h]
            rolled = pltpu.roll(old, past - new, 1)
            out_ref[0, h] = rolled
            out_ref[0, h, :, past - LANES:past] = jnp.where(
                is_new, jnp.concatenate([zpad, fresh], axis=1), rolled[:, past - LANES:past])
            alls.append(jnp.concatenate([old, fresh, zpad], axis=1).astype(BF16))
        k_all, v_all = alls
        q = jnp.concatenate([q_ref[0, h], jnp.zeros((rows - new, ATT_HEAD_DIM), F32)], axis=0).astype(BF16)
        s = jnp.dot(q, k_all, preferred_element_type=F32)
        ps, dens, lses = [], [], []
        for ok in oks:
            sg = jnp.where(ok, s, MASKED)
            m = jnp.max(sg, axis=1, keepdims=True)
            p = jnp.exp2(sg - m)
            den = jnp.sum(p, axis=1, keepdims=True)
            ps.append(p.astype(BF16))
            dens.append(den)
            lses.append(m + jnp.log2(den))
        o_all = _dot_nt(jnp.concatenate(ps, axis=0), v_all)
        m = jnp.maximum(jnp.maximum(lses[0], lses[1]), lses[2])
        ws = [jnp.exp2(l - m) for l in lses]
        o = sum(w * o_all[i * rows:(i + 1) * rows] / d for i, (w, d) in enumerate(zip(ws, dens)))
        o = o / (ws[0] + ws[1] + ws[2])
        o_ref[0, h] = o[0:new].astype(BF16)


def _attn_sample(q, kn, vn, cache_k, cache_v):
    nb, new, _ = q.shape
    past = cache_k.shape[1]
    hps = SAMPLE_HEADS_PER_STEP
    assert past >= ATT_WINDOW and past % LANES == 0 and new == SUBLANES and ATT_HEADS % hps == 0
    heads = (nb, new, ATT_HEADS, ATT_HEAD_DIM)
    q4 = q.reshape(heads).transpose(0, 2, 1, 3)
    kn4 = kn.reshape(heads).transpose(0, 2, 3, 1)
    vn4 = vn.reshape(heads).transpose(0, 2, 3, 1)
    ck = cache_k.transpose(0, 2, 3, 1)
    cv = cache_v.transpose(0, 2, 3, 1)
    hmap = lambda b, g: (b, g, 0, 0)
    qspec = pl.BlockSpec((1, hps, new, ATT_HEAD_DIM), hmap)
    nspec = pl.BlockSpec((1, hps, ATT_HEAD_DIM, new), hmap)
    cspec = pl.BlockSpec((1, hps, ATT_HEAD_DIM, past), hmap)
    o4, nk, nv = pl.pallas_call(
        functools.partial(_attn_sample_body, past=past, new=new),
        grid=(nb, ATT_HEADS // hps),
        in_specs=[qspec, nspec, nspec, cspec, cspec],
        out_specs=[qspec, cspec, cspec],
        out_shape=[
            jax.ShapeDtypeStruct((nb, ATT_HEADS, new, ATT_HEAD_DIM), BF16),
            jax.ShapeDtypeStruct(ck.shape, F32),
            jax.ShapeDtypeStruct(cv.shape, F32),
        ],
        compiler_params=_cparams(("parallel", "parallel")),
        name="attn_sample",
    )(q4, kn4, vn4, ck, cv)
    o = o4.transpose(0, 2, 1, 3).reshape(nb, new, ATT_WIDTH)
    return o, nk.transpose(0, 3, 1, 2), nv.transpose(0, 3, 1, 2)


ROUTE_COLS = LANES
META_COLS = SUBLANES


def _proj_route_body(y_ref, w_ref, x_ref, gt_ref, g_ref, sc_ref, sh_ref, whi_ref, wlo_ref, b_ref,
                     xo_ref, h_ref, meta_ref):
    tm = h_ref.shape[0]
    mix = jnp.dot(y_ref[...], w_ref[...], preferred_element_type=F32)
    x = x_ref[...] + gt_ref[...] * mix.reshape(x_ref.shape)
    xo_ref[...] = x
    h = _norm_mod(x, g_ref[...], sc_ref[...], sh_ref[...]).reshape(tm, D_MODEL)
    h_ref[...] = h
    hi = h.astype(BF16)
    lo = (h - hi.astype(F32)).astype(BF16)
    dot = functools.partial(jnp.dot, preferred_element_type=F32)
    logits = dot(hi, whi_ref[...]) + dot(lo, whi_ref[...]) + dot(hi, wlo_ref[...]) + b_ref[...]
    lane = lax.broadcasted_iota(I32, (tm, ROUTE_COLS), 1).astype(F32)
    big = float(ROUTE_COLS)
    neg = -jnp.inf
    gl = jnp.where(lane < MOE_GROUPS, logits, neg)
    gmax = jnp.max(gl, axis=1, keepdims=True)
    g_idx = jnp.min(jnp.where(gl == gmax, lane, big), axis=1, keepdims=True)
    g_w = 1.0 / jnp.sum(jnp.exp(gl - gmax), axis=1, keepdims=True)
    first = MOE_GROUPS + MOE_PER_GROUP * g_idx
    el = jnp.where((lane >= first) & (lane < first + MOE_PER_GROUP), logits, neg)
    v1 = jnp.max(el, axis=1, keepdims=True)
    i1 = jnp.min(jnp.where(el == v1, lane, big), axis=1, keepdims=True)
    el2 = jnp.where(lane == i1, neg, el)
    v2 = jnp.max(el2, axis=1, keepdims=True)
    i2 = jnp.min(jnp.where(el2 == v2, lane, big), axis=1, keepdims=True)
    t = jnp.exp(v2 - v1)
    w1 = g_w / (1.0 + t)
    w2 = g_w * t / (1.0 + t)
    meta = jnp.where(lane == 0, i1 - MOE_GROUPS,
                     jnp.where(lane == 1, i2 - MOE_GROUPS, jnp.where(lane == 2, w1, jnp.where(lane == 3, w2, 0.0))))
    meta_ref[...] = meta[:, 0:META_COLS]


def _proj_route(y, w, x, gt, g, sc, sh, w_hi, w_lo, bias):
    nb, sl, _ = x.shape
    kdim = y.shape[1]
    bt, lt = _row_tiling(nb, sl, ROW_TILE)
    nl = sl // lt
    tm = bt * lt
    c2 = lambda i: (0, 0)
    xspec = pl.BlockSpec((bt, lt, D_MODEL), lambda i: (i // nl, i % nl, 0))
    mspec = pl.BlockSpec((bt, 1, D_MODEL), lambda i: (i // nl, 0, 0))
    return pl.pallas_call(
        _proj_route_body,
        grid=((nb // bt) * nl,),
        in_specs=[
            pl.BlockSpec((tm, kdim), lambda i: (i, 0)),
            pl.BlockSpec((kdim, D_MODEL), c2),
            xspec, mspec,
            pl.BlockSpec((1, 1, D_MODEL), lambda i: (0, 0, 0)),
            mspec, mspec,
            pl.BlockSpec((D_MODEL, ROUTE_COLS), c2),
            pl.BlockSpec((D_MODEL, ROUTE_COLS), c2),
            pl.BlockSpec((1, ROUTE_COLS), c2),
        ],
        out_specs=[xspec, pl.BlockSpec((tm, D_MODEL), lambda i: (i, 0)), pl.BlockSpec((tm, META_COLS), lambda i: (i, 0))],
        out_shape=[jax.ShapeDtypeStruct(x.shape, F32), jax.ShapeDtypeStruct((nb * sl, D_MODEL), F32),
                   jax.ShapeDtypeStruct((nb * sl, META_COLS), F32)],
        compiler_params=_cparams(("parallel",)),
        name="proj_route",
    )(y, w, x, gt, g.reshape(1, 1, D_MODEL), sc, sh, w_hi, w_lo, bias)


FLAG_VALID, FLAG_FIRST, FLAG_LAST = 1, 2, 4


def _moe_experts_body(tok_ref, item_tile_ref, item_exp_ref, item_flag_ref,
                      h_hbm, meta_ref, w1_ref, w3_ref, w2_ref, out_hbm,
                      xbuf, acc, gsem, ssem, *, n_tiles):
    tm = MOE_TILE
    g = pl.program_id(0)
    n_items = pl.num_programs(0)
    tile = item_tile_ref[g]
    e = item_exp_ref[g]
    flag = item_flag_ref[g]
    slot = tile % 2

    def gather_start(t, s):
        for r in range(tm):
            tok = tok_ref[t * tm + r]
            pltpu.make_async_copy(h_hbm.at[pl.ds(tok, 1)], xbuf.at[s, pl.ds(r, 1)], gsem.at[s]).start(priority=1)

    def gather_wait(s):
        pltpu.make_async_copy(h_hbm.at[pl.ds(0, tm)], xbuf.at[s], gsem.at[s]).wait()

    def scatter_start(t, s):
        for r in range(tm):
            tok = tok_ref[t * tm + r]
            pltpu.make_async_copy(acc.at[s, pl.ds(r, 1)], out_hbm.at[pl.ds(tok, 1)], ssem.at[s]).start(priority=r % 2)

    def scatter_wait(s):
        pltpu.make_async_copy(acc.at[s], out_hbm.at[pl.ds(0, tm)], ssem.at[s]).wait()

    def per_slot(fn):
        for s in range(2):
            pl.when(slot == s)(functools.partial(fn, s))

    @pl.when(g == 0)
    def _():
        gather_start(0, 0)

    @pl.when((flag & FLAG_FIRST) != 0)
    def _():
        @pl.when(tile >= 2)
        def _():
            scatter_wait(slot)

        gather_wait(slot)

        @pl.when(tile + 1 < n_tiles)
        def _():
            per_slot(lambda s: gather_start(tile + 1, 1 - s))

    def expert_out():
        x = xbuf[slot].astype(BF16)
        a = jnp.dot(x, w1_ref[0], preferred_element_type=F32)
        u = jnp.dot(x, w3_ref[0], preferred_element_type=F32)
        ef = e.astype(F32)
        gate = (jnp.where(meta_ref[:, 0:1] == ef, meta_ref[:, 2:3], 0.0)
                + jnp.where(meta_ref[:, 1:2] == ef, meta_ref[:, 3:4], 0.0))
        hm = (_silu(a) * u * gate).astype(BF16)
        return jnp.dot(hm, w2_ref[0], preferred_element_type=F32)

    @pl.when((flag & FLAG_FIRST) != 0)
    def _():
        acc[slot] = expert_out()

    @pl.when((flag & (FLAG_VALID | FLAG_FIRST)) == FLAG_VALID)
    def _():
        acc[slot] += expert_out()

    @pl.when((flag & FLAG_LAST) != 0)
    def _():
        per_slot(lambda s: scatter_start(tile, s))

    @pl.when(g == n_items - 1)
    def _():
        if n_tiles >= 2:
            scatter_wait((n_tiles - 2) % 2)
        scatter_wait((n_tiles - 1) % 2)


def _moe_plan(meta, n_tiles, n_items):
    tm = MOE_TILE
    ea = meta[:, 0].astype(I32)
    eb = meta[:, 1].astype(I32)
    key = jnp.minimum(ea, eb) * MOE_EXPERTS + jnp.maximum(ea, eb)
    order = jnp.argsort(key).astype(I32)
    meta_s = meta[order]
    ea_s = ea[order].reshape(n_tiles, tm)
    eb_s = eb[order].reshape(n_tiles, tm)
    experts = jnp.arange(MOE_EXPERTS, dtype=I32)
    present = jnp.any((ea_s[:, :, None] == experts) | (eb_s[:, :, None] == experts), axis=1)
    flat = present.reshape(-1)
    count = jnp.sum(flat.astype(I32))
    (idx,) = jnp.nonzero(flat, size=n_items, fill_value=0)
    idx = idx.astype(I32)
    pos = jnp.arange(n_items, dtype=I32)
    valid = pos < count
    last_idx = idx[jnp.maximum(count - 1, 0)]
    idx = jnp.where(valid, idx, last_idx)
    item_tile = idx // MOE_EXPERTS
    item_exp = idx % MOE_EXPERTS
    prev_tile = jnp.concatenate([jnp.full((1,), -1, I32), item_tile[:-1]])
    next_tile = jnp.concatenate([item_tile[1:], jnp.full((1,), -1, I32)])
    first = valid & (item_tile != prev_tile)
    last = valid & ((item_tile != next_tile) | (pos == count - 1))
    flags = valid.astype(I32) * FLAG_VALID + first.astype(I32) * FLAG_FIRST + last.astype(I32) * FLAG_LAST
    return order, meta_s, item_tile, item_exp, flags


def _moe_experts(h, meta, w1, w3, w2, layer):
    tokens = h.shape[0]
    tm = MOE_TILE
    assert tokens % tm == 0
    n_tiles = tokens // tm
    pair_classes = MOE_GROUPS * (MOE_PER_GROUP * (MOE_PER_GROUP - 1) // 2)
    n_items = min(MOE_EXPERTS * n_tiles, 2 * (n_tiles + pair_classes - 1))
    order, meta_s, item_tile, item_exp, flags = _moe_plan(meta, n_tiles, n_items)
    wmap = lambda g, tok, it, ie, fl: (layer, ie[g], 0, 0)
    wspec1 = pl.BlockSpec((None, 1, D_MODEL, MOE_HIDDEN), wmap)
    wspec2 = pl.BlockSpec((None, 1, MOE_HIDDEN, D_MODEL), wmap)
    return pl.pallas_call(
        functools.partial(_moe_experts_body, n_tiles=n_tiles),
        grid_spec=pltpu.PrefetchScalarGridSpec(
            num_scalar_prefetch=4,
            grid=(n_items,),
            in_specs=[
                pl.BlockSpec(memory_space=pl.ANY),
                pl.BlockSpec((tm, META_COLS), lambda g, tok, it, ie, fl: (it[g], 0)),
                wspec1, wspec1, wspec2,
            ],
            out_specs=pl.BlockSpec(memory_space=pl.ANY),
            scratch_shapes=[
                pltpu.VMEM((2, tm, D_MODEL), F32),
                pltpu.VMEM((2, tm, D_MODEL), F32),
                pltpu.SemaphoreType.DMA((2,)),
                pltpu.SemaphoreType.DMA((2,)),
            ],
        ),
        out_shape=jax.ShapeDtypeStruct((tokens, D_MODEL), F32),
        compiler_params=_cparams(("arbitrary",)),
        name="moe_experts",
    )(order, item_tile, item_exp, flags, h, meta_s, w1, w3, w2)


def _final_combine_body(x_ref, m_ref, gt_ref, fg_ref, o_ref):
    x = x_ref[...] + gt_ref[...] * m_ref[...].reshape(x_ref.shape)
    o_ref[...] = x * lax.rsqrt(jnp.mean(x * x, axis=-1, keepdims=True) + EPS) * fg_ref[...]


def _final_combine(x, moe_out, gt, final_g):
    nb, sl, _ = x.shape
    bt, lt = _row_tiling(nb, sl, ROW_TILE)
    nl = sl // lt
    xmap = lambda i: (i // nl, i % nl, 0)
    return pl.pallas_call(
        _final_combine_body,
        grid=((nb // bt) * nl,),
        in_specs=[
            pl.BlockSpec((bt, lt, D_MODEL), xmap),
            pl.BlockSpec((bt * lt, D_MODEL), lambda i: (i, 0)),
            pl.BlockSpec((bt, 1, D_MODEL), lambda i: (i // nl, 0, 0)),
            pl.BlockSpec((1, 1, D_MODEL), lambda i: (0, 0, 0)),
        ],
        out_specs=pl.BlockSpec((bt, lt, D_MODEL), xmap),
        out_shape=jax.ShapeDtypeStruct(x.shape, F32),
        compiler_params=_cparams(("parallel",)),
        name="final_combine",
    )(x, moe_out, gt, final_g.reshape(1, 1, D_MODEL))


def _rope_tables(pos):
    half = ATT_HEAD_DIM // 2
    inv_freq = ROPE_THETA ** (-jnp.arange(half, dtype=F32) / half)
    ang = pos.astype(F32)[:, None] * inv_freq[None, :]
    cos = jnp.cos(ang)
    sin = jnp.sin(ang)
    cos_h = jnp.concatenate([cos, cos], axis=1)
    sin_h = jnp.concatenate([-sin, sin], axis=1)
    return jnp.tile(cos_h, (1, ATT_HEADS)), jnp.tile(sin_h, (1, ATT_HEADS))


def _router_weights(wg, bg, we, be):
    w = jnp.concatenate([wg, we], axis=1)
    w = jnp.pad(w, ((0, 0), (0, ROUTE_COLS - w.shape[1])))
    b = jnp.pad(jnp.concatenate([bg, be]), (0, ROUTE_COLS - MOE_GROUPS - MOE_EXPERTS)).reshape(1, ROUTE_COLS)
    hi = w.astype(BF16)
    lo = (w - hi.astype(F32)).astype(BF16)
    return hi, lo, b


def _even_w_in_cols(w):
    dt0 = SSD_INNER + SSD_CONV_DIM
    dt1 = dt0 + SSD_HEADS
    zeros = jnp.zeros((w.shape[0], LANES - SSD_HEADS), w.dtype)
    return jnp.concatenate([w[:, :dt0], w[:, dt1:], w[:, dt0:dt1], zeros], axis=1).astype(BF16)


def _run_group(x, mods, pos, states, caches, wts):
    nb, sl, _ = x.shape
    ssd_in, conv_in, ret_in = states
    cos, sin = _rope_tables(pos)

    def mod_parts(i):
        m = mods[i].reshape(nb, 1, N_MOD, D_MODEL)
        return [m[:, :, j] for j in range(N_MOD)]

    sh1, sc1, gt1, sh2, sc2, gt2 = mod_parts(0)
    p = _even_in(x, wts["norm_mix_g"][0], sc1, sh1, wts["even_w_in"]).reshape(nb, sl, P_WIDTH)
    y, ssd_new, conv_new, ret_new = _even_core(
        p, cos, sin, wts["ssd_conv_w"], wts["ssd_conv_b"], wts["ssd_dt_bias"], wts["ssd_a_log"],
        wts["ssd_d"], wts["ssd_norm_g"], wts["ret_norm_g"], ssd_in, conv_in, ret_in)
    y = y.reshape(nb * sl, EVEN_OUT)
    x, h, meta = _proj_route(y, wts["even_w_out"], x, gt1, wts["norm_ffn_g"][0], sc2, sh2, *wts["router"][0])
    mix = _moe_experts(h, meta, *wts["experts"], 0)
    gt_moe = gt2

    sh1, sc1, gt1, sh2, sc2, gt2 = mod_parts(1)
    if caches is None:
        x, q, k, v = _qkv_rope(x, mix, gt_moe, wts["norm_mix_g"][1], sc1, sh1, wts["odd_w_qkv"], cos, sin)
        q3, k3, v3 = (a.reshape(nb, sl, ATT_WIDTH) for a in (q, k, v))
        o, k_t, v_t = _attn_prompt(q3, k3, v3)
        keep = min(ATT_WINDOW, sl)
        new_k, new_v = (a.transpose(0, 3, 1, 2)[:, sl - keep:] for a in (k_t, v_t))
    else:
        cos_t, sin_t = jnp.tile(cos, (nb, 1)), jnp.tile(sin, (nb, 1))
        x, q, k, v = _qkv_rope(x, mix, gt_moe, wts["norm_mix_g"][1], sc1, sh1, wts["odd_w_qkv"], cos_t, sin_t)
        q3, k3, v3 = (a.reshape(nb, sl, ATT_WIDTH) for a in (q, k, v))
        o, new_k, new_v = _attn_sample(q3, k3, v3, *caches)
    x, h, meta = _proj_route(o.reshape(nb * sl, ATT_WIDTH), wts["odd_w_out"], x, gt1,
                             wts["norm_ffn_g"][1], sc2, sh2, *wts["router"][1])
    mix = _moe_experts(h, meta, *wts["experts"], 1)
    x = _final_combine(x, mix, gt2, wts["final_norm_g"])
    return x, ssd_new[None], conv_new[None], ret_new[None], new_k[None], new_v[None]


def kernel(x_prompt, x_sample, state_ssd, state_conv, state_ret, cache_k, cache_v, c_prompt, c_sample, ada_w, ada_b, norm_mix_g, norm_ffn_g, final_norm_g, even_w_in, even_w_out, ssd_conv_w, ssd_conv_b, ssd_dt_bias, ssd_a_log, ssd_d, ssd_norm_g, ret_norm_g, odd_w_qkv, odd_w_out, moe_wg, moe_bg, moe_we, moe_be, moe_w1, moe_w3, moe_w2):
    depth = ada_w.shape[0]
    assert depth == 2 and even_w_in.shape[0] == 1 and odd_w_qkv.shape[0] == 1
    bp, sp, _ = x_prompt.shape
    bs, ss, _ = x_sample.shape

    wts = {
        "norm_mix_g": norm_mix_g, "norm_ffn_g": norm_ffn_g, "final_norm_g": final_norm_g,
        "even_w_in": _even_w_in_cols(even_w_in[0]), "even_w_out": even_w_out[0].astype(BF16),
        "ssd_conv_w": ssd_conv_w[0], "ssd_conv_b": ssd_conv_b[0], "ssd_dt_bias": ssd_dt_bias[0],
        "ssd_a_log": ssd_a_log[0], "ssd_d": ssd_d[0], "ssd_norm_g": ssd_norm_g[0], "ret_norm_g": ret_norm_g[0],
        "odd_w_qkv": odd_w_qkv[0].astype(BF16), "odd_w_out": odd_w_out[0].astype(BF16),
        "router": [_router_weights(moe_wg[i], moe_bg[i], moe_we[i], moe_be[i]) for i in range(depth)],
        "experts": (moe_w1.astype(BF16), moe_w3.astype(BF16), moe_w2.astype(BF16)),
    }
    mods = _adaln(jnp.concatenate([c_prompt, c_sample], axis=0), ada_w, ada_b)

    zeros_p = (
        jnp.zeros((bp, SSD_HEADS, SSD_HEAD_DIM, SSD_STATE), F32),
        jnp.zeros((bp, SSD_CONV - 1, SSD_CONV_DIM), F32),
        jnp.zeros((bp, RET_HEADS, RET_DV, RET_DK), F32),
    )
    out_p = _run_group(x_prompt, mods[:, :bp], jnp.arange(sp, dtype=I32), zeros_p, None, wts)
    out_s = _run_group(x_sample, mods[:, bp:], PAST_LEN + jnp.arange(ss, dtype=I32),
                       (state_ssd[0], state_conv[0], state_ret[0]), (cache_k[0], cache_v[0]), wts)
    return (out_p[0], out_s[0]) + out_p[1:] + out_s[1:]
```
